```python
import math
import jax, jax.numpy as jnp
from jax import lax
import numpy as np

D_MODEL = 1024
BATCH = 8
SEQ = 2048
DEPTH = 1

GRID_W = 64
CTX_LEN = 256
HEAD_DIM = 64
MIX_WIDTH = D_MODEL
DIFF_HEADS = (MIX_WIDTH // 2) // (2 * HEAD_DIM)
NA_HEADS = (MIX_WIDTH // 2) // HEAD_DIM
DIFF_WIDTH = DIFF_HEADS * 2 * HEAD_DIM
NA_WIDTH = NA_HEADS * HEAD_DIM
IN_WIDTH = 3 * DIFF_WIDTH + 3 * NA_WIDTH
NA_KH_MAX = 8
NA_KW = 16
NA_QCOL_BLOCK = 16
NA_KCOL_BAND = 32
D_FF = 2816
ROPE_BASE = 10000.0
NORM_EPS = 1e-6
Q_BLOCK = 128
N_MOD = 9
NEG_INF = -1e30

kernel_name = "hybrid_diffattn_natten_macaron_dit"


def rms_norm(x, g):
    xf = x.astype(jnp.float32)
    y = xf * lax.rsqrt(jnp.mean(xf * xf, axis=-1, keepdims=True) + NORM_EPS)
    return (y * g.astype(jnp.float32)).astype(x.dtype)


def modulate(h, shift, scale):
    return h * (1.0 + scale) + shift


def adaln(cond, w_ada, b_ada):
    return jax.nn.silu(cond) @ w_ada + b_ada


def swiglu(h, w_gu, w_down):
    g, u = jnp.split(h @ w_gu, 2, axis=-1)
    return (jax.nn.silu(g) * u) @ w_down


def ffn_half_step(h, g, shift, scale, gate, w_gu, w_down):
    return h + 0.5 * gate * swiglu(modulate(rms_norm(h, g), shift, scale), w_gu, w_down)


def axial_rope_angles(n_tokens):
    t = jnp.arange(n_tokens, dtype=jnp.int32)
    row = (t // GRID_W).astype(jnp.float32)
    col = (t % GRID_W).astype(jnp.float32)
    n_freq = HEAD_DIM // 4
    inv_freq = ROPE_BASE ** (-jnp.arange(n_freq, dtype=jnp.float32) / n_freq)
    return row[:, None] * inv_freq, col[:, None] * inv_freq


def rope_rotate(x, ang):
    x1, x2 = jnp.split(x, 2, axis=-1)
    cos = jnp.cos(ang).astype(x.dtype)
    sin = jnp.sin(ang).astype(x.dtype)
    return jnp.concatenate([x1 * cos - x2 * sin, x2 * cos + x1 * sin], axis=-1)


def apply_axial_rope(x, ang_row, ang_col):
    xr, xc = jnp.split(x, 2, axis=-1)
    return jnp.concatenate([rope_rotate(xr, ang_row), rope_rotate(xc, ang_col)], axis=-1)


def split_heads(t, n_heads, hd):
    B, T, _ = t.shape
    return t.reshape(B, T, n_heads, hd).transpose(0, 2, 1, 3)


def merge_heads(t):
    B, H, T, e = t.shape
    return t.transpose(0, 2, 1, 3).reshape(B, T, H * e)


def diff_heads_qk(t, gain):
    B, T, _ = t.shape
    t = t.reshape(B, T, DIFF_HEADS, 2, HEAD_DIM).transpose(0, 2, 3, 1, 4)
    return rms_norm(t, gain)


def diff_lambda(lq1, lk1, lq2, lk2, lam_init):
    f = lambda a, b: jnp.exp(jnp.sum(a.astype(jnp.float32) * b.astype(jnp.float32)))
    return f(lq1, lk1) - f(lq2, lk2) + lam_init


def diff_core(q, k, v, lam):
    s = jnp.einsum('bhcqd,bhckd->bhcqk', q, k).astype(jnp.float32) * (HEAD_DIM ** -0.5)
    p = jax.nn.softmax(s, axis=-1)
    p = p[:, :, 0] - lam * p[:, :, 1]
    return jnp.einsum('bhqk,bhke->bhqe', p.astype(v.dtype), v)


def diff_attention_blocks(q, k, v, lam):
    B, H, _, S, d = q.shape
    nb = S // Q_BLOCK
    qb = jnp.moveaxis(q.reshape(B, H, 2, nb, Q_BLOCK, d), 3, 0)
    out = lax.map(lambda qq: diff_core(qq, k, v, lam), qb)
    return jnp.moveaxis(out, 0, 2).reshape(B, H, S, v.shape[-1])


def dense_attention(q, k, v):
    s = jnp.einsum('bhqd,bhkd->bhqk', q, k).astype(jnp.float32) * (HEAD_DIM ** -0.5)
    p = jax.nn.softmax(s, axis=-1)
    return jnp.einsum('bhqk,bhkd->bhqd', p.astype(v.dtype), v)


def na_column_tables():
    n_cb = GRID_W // NA_QCOL_BLOCK
    qcol = np.arange(GRID_W).reshape(n_cb, NA_QCOL_BLOCK)
    cs = np.clip(qcol - NA_KW // 2, 0, GRID_W - NA_KW)
    bs = np.clip(np.arange(n_cb) * NA_QCOL_BLOCK - NA_KW // 2, 0, GRID_W - NA_KCOL_BAND)
    kcol = bs[:, None] + np.arange(NA_KCOL_BAND)
    valid = (kcol[:, None, :] >= cs[:, :, None]) & (kcol[:, None, :] < cs[:, :, None] + NA_KW)
    col_off = np.clip(kcol[:, None, :] - qcol[:, :, None] + NA_KW - 1, 0, 2 * NA_KW - 2)
    return kcol.astype(np.int32), valid, col_off.astype(np.int32)


def neighborhood_attention(q, k, v, k_ctx, v_ctx, rpb):
    B, H, S, d = q.shape
    rows = S // GRID_W
    kh = min(NA_KH_MAX, rows)
    n_cb = GRID_W // NA_QCOL_BLOCK
    kcol, valid, col_off = na_column_tables()
    qg = q.reshape(B, H, rows, GRID_W, d)
    kg = k.reshape(B, H, rows, GRID_W, d)
    vg = v.reshape(B, H, rows, GRID_W, d)
    rpb_col = jnp.take(rpb, col_off, axis=2).astype(jnp.float32)
    scale = HEAD_DIM ** -0.5
    n_win = kh * NA_KCOL_BAND

    def row_step(r):
        rs = jnp.clip(r - kh // 2, 0, rows - kh)
        k_rows = lax.dynamic_slice_in_dim(kg, rs, kh, axis=2)
        v_rows = lax.dynamic_slice_in_dim(vg, rs, kh, axis=2)
        k_win = jnp.take(k_rows, kcol, axis=3)
        v_win = jnp.take(v_rows, kcol, axis=3)
        q_r = lax.dynamic_index_in_dim(qg, r, axis=2, keepdims=False).reshape(B, H, n_cb, NA_QCOL_BLOCK, d)
        s_win = jnp.einsum('bhnqd,bhinkd->bhnqik', q_r, k_win).astype(jnp.float32) * scale
        row_idx = rs + jnp.arange(kh, dtype=jnp.int32) - r + (NA_KH_MAX - 1)
        bias = jnp.take(rpb_col, row_idx, axis=1).transpose(0, 2, 3, 1, 4)
        s_win = jnp.where(valid[:, :, None, :], s_win + bias[None], NEG_INF)
        s_ctx = jnp.einsum('bhnqd,bhcd->bhnqc', q_r, k_ctx).astype(jnp.float32) * scale
        s = jnp.concatenate([s_win.reshape(B, H, n_cb, NA_QCOL_BLOCK, n_win), s_ctx], axis=-1)
        p = jax.nn.softmax(s, axis=-1).astype(v.dtype)
        p_win = p[..., :n_win].reshape(B, H, n_cb, NA_QCOL_BLOCK, kh, NA_KCOL_BAND)
        p_ctx = p[..., n_win:]
        out = (jnp.einsum('bhnqik,bhinkd->bhnqd', p_win, v_win)
               + jnp.einsum('bhnqc,bhcd->bhnqd', p_ctx, v_ctx))
        return out.reshape(B, H, GRID_W, d)

    out = lax.map(row_step, jnp.arange(rows, dtype=jnp.int32))
    return out.transpose(1, 2, 0, 3, 4).reshape(B, H, S, d)


def setup_inputs(seed: int = 0) -> dict:
    key = jax.random.key(seed)
    ks = jax.random.split(key, 32)
    L, D = DEPTH, D_MODEL
    nrm = lambda k, shape, s: jax.random.normal(k, shape, jnp.float32) * s
    gain = lambda k, shape: 1.0 + 0.02 * jax.random.normal(k, shape, jnp.float32)
    return {
        "x": nrm(ks[0], (BATCH, SEQ, D), 1.0),
        "c": nrm(ks[1], (BATCH, D), 1.0),
        "ctx": nrm(ks[2], (BATCH, CTX_LEN, D), 1.0),
        "c_ctx": nrm(ks[3], (D,), 1.0),
        "w_ada": nrm(ks[4], (L, D, N_MOD * D), 0.5 * D ** -0.5),
        "b_ada": nrm(ks[5], (L, N_MOD * D), 0.01),
        "norm1": gain(ks[6], (L, D)),
        "norm2": gain(ks[7], (L, D)),
        "norm3": gain(ks[8], (L, D)),
        "ffn1_w_gu": nrm(ks[9], (L, D, 2 * D_FF), D ** -0.5),
        "ffn1_w_down": nrm(ks[10], (L, D_FF, D), D_FF ** -0.5),
        "w_in": nrm(ks[11], (L, D, IN_WIDTH), D ** -0.5),
        "diff_q_norm": gain(ks[12], (L, HEAD_DIM)),
        "diff_k_norm": gain(ks[13], (L, HEAD_DIM)),
        "lam_q1": nrm(ks[14], (L, HEAD_DIM), 0.1),
        "lam_k1": nrm(ks[15], (L, HEAD_DIM), 0.1),
        "lam_q2": nrm(ks[16], (L, HEAD_DIM), 0.1),
        "lam_k2": nrm(ks[17], (L, HEAD_DIM), 0.1),
        "diff_out_norm": gain(ks[18], (L, 2 * HEAD_DIM)),
        "na_q_norm": gain(ks[19], (L, HEAD_DIM)),
        "na_k_norm": gain(ks[20], (L, HEAD_DIM)),
        "na_rpb": nrm(ks[21], (L, NA_HEADS, 2 * NA_KH_MAX - 1, 2 * NA_KW - 1), 0.1),
        "w_out": nrm(ks[22], (L, MIX_WIDTH, D), MIX_WIDTH ** -0.5),
        "ffn2_w_gu": nrm(ks[23], (L, D, 2 * D_FF), D ** -0.5),
        "ffn2_w_down": nrm(ks[24], (L, D_FF, D), D_FF ** -0.5),
    }


def reference(x, c, ctx, c_ctx, w_ada, b_ada, norm1, norm2, norm3, ffn1_w_gu, ffn1_w_down, w_in,
              diff_q_norm, diff_k_norm, lam_q1, lam_k1, lam_q2, lam_k2, diff_out_norm,
              na_q_norm, na_k_norm, na_rpb, w_out, ffn2_w_gu, ffn2_w_down):
    S = x.shape[1]
    ang_row, ang_col = axial_rope_angles(S)
    splits = (DIFF_WIDTH, 2 * DIFF_WIDTH, 3 * DIFF_WIDTH, 3 * DIFF_WIDTH + NA_WIDTH, 3 * DIFF_WIDTH + 2 * NA_WIDTH)
    h, hc = x, ctx
    for l in range(DEPTH):
        update_ctx = l < DEPTH - 1
        lam_init = 0.8 - 0.6 * math.exp(-0.3 * l)
        mod = jnp.split(adaln(c, w_ada[l], b_ada[l])[:, None, :], N_MOD, axis=-1)
        mod_c = jnp.split(adaln(c_ctx, w_ada[l], b_ada[l]), N_MOD, axis=-1)

        h = ffn_half_step(h, norm1[l], mod[0], mod[1], mod[2], ffn1_w_gu[l], ffn1_w_down[l])
        hc = ffn_half_step(hc, norm1[l], mod_c[0], mod_c[1], mod_c[2], ffn1_w_gu[l], ffn1_w_down[l])

        y = modulate(rms_norm(h, norm2[l]), mod[3], mod[4]) @ w_in[l]
        yc = modulate(rms_norm(hc, norm2[l]), mod_c[3], mod_c[4]) @ w_in[l]
        dq, dk, dv, nq, nk, nv = jnp.split(y, splits, axis=-1)
        dq_c, dk_c, dv_c, nq_c, nk_c, nv_c = jnp.split(yc, splits, axis=-1)

        lam = diff_lambda(lam_q1[l], lam_k1[l], lam_q2[l], lam_k2[l], lam_init)
        a_q = apply_axial_rope(diff_heads_qk(dq, diff_q_norm[l]), ang_row, ang_col)
        a_k = apply_axial_rope(diff_heads_qk(dk, diff_k_norm[l]), ang_row, ang_col)
        a_v = split_heads(dv, DIFF_HEADS, 2 * HEAD_DIM)
        a_k_c = diff_heads_qk(dk_c, diff_k_norm[l])
        a_v_c = split_heads(dv_c, DIFF_HEADS, 2 * HEAD_DIM)
        k_all = jnp.concatenate([a_k_c, a_k], axis=3)
        v_all = jnp.concatenate([a_v_c, a_v], axis=2)
        o_a = diff_attention_blocks(a_q, k_all, v_all, lam)
        o_a = merge_heads(rms_norm(o_a, diff_out_norm[l]) * (1.0 - lam_init))

        b_q = rms_norm(split_heads(nq, NA_HEADS, HEAD_DIM), na_q_norm[l])
        b_k = rms_norm(split_heads(nk, NA_HEADS, HEAD_DIM), na_k_norm[l])
        b_v = split_heads(nv, NA_HEADS, HEAD_DIM)
        b_k_c = rms_norm(split_heads(nk_c, NA_HEADS, HEAD_DIM), na_k_norm[l])
        b_v_c = split_heads(nv_c, NA_HEADS, HEAD_DIM)
        o_b = merge_heads(neighborhood_attention(b_q, b_k, b_v, b_k_c, b_v_c, na_rpb[l]))

        h = h + mod[5] * (jnp.concatenate([o_a, o_b], axis=-1) @ w_out[l])
        h = ffn_half_step(h, norm3[l], mod[6], mod[7], mod[8], ffn2_w_gu[l], ffn2_w_down[l])

        if update_ctx:
            oc_a = diff_core(diff_heads_qk(dq_c, diff_q_norm[l]), a_k_c, a_v_c, lam)
            oc_a = merge_heads(rms_norm(oc_a, diff_out_norm[l]) * (1.0 - lam_init))
            oc_b = merge_heads(dense_attention(rms_norm(split_heads(nq_c, NA_HEADS, HEAD_DIM), na_q_norm[l]), b_k_c, b_v_c))
            hc = hc + mod_c[5] * (jnp.concatenate([oc_a, oc_b], axis=-1) @ w_out[l])
            hc = ffn_half_step(hc, norm3[l], mod_c[6], mod_c[7], mod_c[8], ffn2_w_gu[l], ffn2_w_down[l])
    return h
```

```python
import functools
import math

import numpy as np
import jax
import jax.numpy as jnp
from jax import lax
from jax.experimental import pallas as pl
from jax.experimental.pallas import tpu as pltpu

D_MODEL = 1024
GRID_W = 64
HEAD_DIM = 64
DIFF_HEADS = 4
NA_HEADS = 8
GROUP_W = 512
D_FF = 2816
N_MOD = 9
NA_KH = 8
NA_KW = 16
ROPE_BASE = 10000.0
NORM_EPS = 1e-6
NEG_INF = -1e30
LAM_INIT = 0.8 - 0.6 * math.exp(-0.3 * 0)

LANES = 128
VMEM_LIMIT = 56 * 1024 * 1024

F32 = jnp.float32
BF16 = jnp.bfloat16

_NT = (((1,), (1,)), ((), ()))


def _cparams(sem):
    return pltpu.CompilerParams(dimension_semantics=sem, vmem_limit_bytes=VMEM_LIMIT)


def _adaln_kernel(cond_ref, w_ref, b_ref, o_ref):
    c = cond_ref[...]
    a = (c * jax.nn.sigmoid(c)).astype(BF16)
    o_ref[...] = jnp.dot(a, w_ref[...].astype(BF16), preferred_element_type=F32) + b_ref[...]


def _adaln(cond, w_ada, b_ada):
    rows, d = cond.shape
    n = w_ada.shape[1]
    tn = 1024
    return pl.pallas_call(
        _adaln_kernel,
        grid=(n // tn,),
        in_specs=[pl.BlockSpec((rows, d), lambda j: (0, 0)),
                  pl.BlockSpec((d, tn), lambda j: (0, j)),
                  pl.BlockSpec((1, tn), lambda j: (0, j))],
        out_specs=pl.BlockSpec((rows, tn), lambda j: (0, j)),
        out_shape=jax.ShapeDtypeStruct((rows, n), F32),
        compiler_params=_cparams(("parallel",)),
        name="adaln",
    )(cond, w_ada, b_ada)


def _modulated_norm(x, g, shift, scale):
    ms = jnp.mean(x * x, axis=-1, keepdims=True)
    xn = x * lax.rsqrt(ms + NORM_EPS) * g
    return xn * (1.0 + scale) + shift


def _mod_spec(k, ctx):
    if ctx:
        return pl.BlockSpec((1, 1, 1, D_MODEL), lambda b, *_: (8, k, 0, 0))
    return pl.BlockSpec((1, 1, 1, D_MODEL), lambda b, *_: (b, k, 0, 0))


def _ffn_kernel(h_ref, shift_ref, scale_ref, gate_ref, g_ref, wg_ref, wu_ref, wd_ref, o_ref,
                xm_ref, acc_ref, *, nj):
    j = pl.program_id(2)

    @pl.when(j == 0)
    def _():
        xm = _modulated_norm(h_ref[0], g_ref[...], shift_ref[0, 0], scale_ref[0, 0])
        xm_ref[...] = xm.astype(BF16)
        acc_ref[...] = jnp.zeros_like(acc_ref)

    xm = xm_ref[...]
    g = jnp.dot(xm, wg_ref[...], preferred_element_type=F32)
    u = jnp.dot(xm, wu_ref[...], preferred_element_type=F32)
    a = (g * jax.nn.sigmoid(g) * u).astype(BF16)
    acc_ref[...] += jnp.dot(a, wd_ref[...], preferred_element_type=F32)

    @pl.when(j == nj - 1)
    def _():
        o_ref[0] = h_ref[0] + (0.5 * gate_ref[0, 0]) * acc_ref[...]


def _ffn(h, mod4, k0, ctx, g, w_gu, w_down, *, tm, tf=256):
    bsz, t, d = h.shape
    f = w_down.shape[0]
    nj = f // tf
    return pl.pallas_call(
        functools.partial(_ffn_kernel, nj=nj),
        grid=(bsz, t // tm, nj),
        in_specs=[pl.BlockSpec((1, tm, d), lambda b, i, j: (b, i, 0)),
                  _mod_spec(k0, ctx), _mod_spec(k0 + 1, ctx), _mod_spec(k0 + 2, ctx),
                  pl.BlockSpec((1, d), lambda b, i, j: (0, 0)),
                  pl.BlockSpec((d, tf), lambda b, i, j: (0, j)),
                  pl.BlockSpec((d, tf), lambda b, i, j: (0, nj + j)),
                  pl.BlockSpec((tf, d), lambda b, i, j: (j, 0))],
        out_specs=pl.BlockSpec((1, tm, d), lambda b, i, j: (b, i, 0)),
        out_shape=jax.ShapeDtypeStruct(h.shape, F32),
        scratch_shapes=[pltpu.VMEM((tm, d), BF16), pltpu.VMEM((tm, d), F32)],
        compiler_params=_cparams(("parallel", "parallel", "arbitrary")),
        name="ffn_ctx" if ctx else "ffn",
    )(h, mod4, mod4, mod4, g, w_gu, w_gu, w_down)


def _head_rms_scale(y, bd_ref):
    y2 = y * y
    hi = y2.astype(BF16)
    lo = (y2 - hi.astype(F32)).astype(BF16)
    bd = bd_ref[...]
    ss = jnp.dot(hi, bd, preferred_element_type=F32) + jnp.dot(lo, bd, preferred_element_type=F32)
    return lax.rsqrt(ss * (1.0 / HEAD_DIM) + NORM_EPS)


def _rope(z, cos, sin):
    lane = lax.broadcasted_iota(jnp.int32, z.shape, 1)
    up = pltpu.roll(z, LANES - 16, 1)
    dn = pltpu.roll(z, 16, 1)
    return z * cos + jnp.where((lane & 16) == 0, up, dn) * sin


def _in_proj_kernel(*refs, groups, n_gain):
    h_ref, shift_ref, scale_ref, g_ref, w_ref, bd_ref, cos_ref, sin_ref = refs[:8]
    gain_refs = refs[8:8 + n_gain]
    out_refs = refs[8 + n_gain:]
    xm = _modulated_norm(h_ref[0], g_ref[...], shift_ref[0, 0], scale_ref[0, 0]).astype(BF16)
    for (col, gain_idx, rope, qscale), o_ref in zip(groups, out_refs):
        y = jnp.dot(xm, w_ref[:, col * GROUP_W:(col + 1) * GROUP_W], preferred_element_type=F32)
        if gain_idx is not None:
            gain = gain_refs[gain_idx][...]
            if qscale != 1.0:
                gain = gain * qscale
            y = y * _head_rms_scale(y, bd_ref) * gain
        if rope:
            cos = cos_ref[...]
            sin = sin_ref[...]
            for c in range(GROUP_W // LANES):
                sl = slice(c * LANES, (c + 1) * LANES)
                o_ref[0, :, sl] = _rope(y[:, sl], cos, sin).astype(BF16)
        else:
            o_ref[0] = y.astype(BF16)


def _in_proj(h, mod4, ctx, g, w_in, bd, cos, sin, gains, groups, *, tm):
    bsz, t, d = h.shape
    n_out = len(groups)
    in_specs = [pl.BlockSpec((1, tm, d), lambda b, i: (b, i, 0)),
                _mod_spec(3, ctx), _mod_spec(4, ctx),
                pl.BlockSpec((1, d), lambda b, i: (0, 0)),
                pl.BlockSpec(w_in.shape, lambda b, i: (0, 0)),
                pl.BlockSpec(bd.shape, lambda b, i: (0, 0)),
                pl.BlockSpec((tm, LANES), lambda b, i: (i, 0)),
                pl.BlockSpec((tm, LANES), lambda b, i: (i, 0))]
    in_specs += [pl.BlockSpec((1, GROUP_W), lambda b, i: (0, 0)) for _ in gains]
    return pl.pallas_call(
        functools.partial(_in_proj_kernel, groups=groups, n_gain=len(gains)),
        grid=(bsz, t // tm),
        in_specs=in_specs,
        out_specs=[pl.BlockSpec((1, tm, GROUP_W), lambda b, i: (b, i, 0)) for _ in range(n_out)],
        out_shape=[jax.ShapeDtypeStruct((bsz, t, GROUP_W), BF16) for _ in range(n_out)],
        compiler_params=_cparams(("parallel", "parallel")),
        name="in_proj_ctx" if ctx else "in_proj",
    )(h, mod4, mod4, g, w_in, bd, cos, sin, *gains)


def _split_halves(q):
    lane = lax.broadcasted_iota(jnp.int32, q.shape, 1)
    qf = q.astype(F32)
    lo = jnp.where(lane < HEAD_DIM, qf, 0.0).astype(BF16)
    hi = jnp.where(lane >= HEAD_DIM, qf, 0.0).astype(BF16)
    return lo, hi


def _softmax_parts(qm, kc, kl):
    sc = lax.dot_general(qm, kc, _NT, preferred_element_type=F32)
    sl = lax.dot_general(qm, kl, _NT, preferred_element_type=F32)
    m = jnp.maximum(jnp.max(sc, axis=-1, keepdims=True), jnp.max(sl, axis=-1, keepdims=True))
    pc = jnp.exp(sc - m)
    pl_ = jnp.exp(sl - m)
    l = jnp.sum(pc, axis=-1, keepdims=True) + jnp.sum(pl_, axis=-1, keepdims=True)
    return pc, pl_, l


def _diff_kernel(q_ref, kc_ref, kl_ref, vc_ref, vl_ref, lamv_ref, gout_ref, o_ref):
    q = q_ref[0]
    kc = kc_ref[0]
    kl = kl_ref[0]
    lv = lamv_ref[...]
    lam = (jnp.exp(jnp.sum(lv[0:1] * lv[1:2], axis=-1, keepdims=True))
           - jnp.exp(jnp.sum(lv[2:3] * lv[3:4], axis=-1, keepdims=True)) + LAM_INIT)
    q0, q1 = _split_halves(q)
    pc0, pl0, l0 = _softmax_parts(q0, kc, kl)
    pc1, pl1, l1 = _softmax_parts(q1, kc, kl)
    a0 = 1.0 / l0
    a1 = lam / l1
    wc = (pc0 * a0 - pc1 * a1).astype(BF16)
    wl = (pl0 * a0 - pl1 * a1).astype(BF16)
    o = (jnp.dot(wc, vc_ref[0], preferred_element_type=F32)
         + jnp.dot(wl, vl_ref[0], preferred_element_type=F32))
    ms = jnp.mean(o * o, axis=-1, keepdims=True)
    o_ref[0] = (o * lax.rsqrt(ms + NORM_EPS) * (gout_ref[...] * (1.0 - LAM_INIT))).astype(BF16)


def _diff_attn(aq, ak, av, akc, avc, lamv, gout, *, tq):
    bsz, s, _ = aq.shape
    c = akc.shape[1]
    return pl.pallas_call(
        _diff_kernel,
        grid=(bsz, DIFF_HEADS, s // tq),
        in_specs=[pl.BlockSpec((1, tq, LANES), lambda b, h, i: (b, i, h)),
                  pl.BlockSpec((1, c, LANES), lambda b, h, i: (b, 0, h)),
                  pl.BlockSpec((1, s, LANES), lambda b, h, i: (b, 0, h)),
                  pl.BlockSpec((1, c, LANES), lambda b, h, i: (b, 0, h)),
                  pl.BlockSpec((1, s, LANES), lambda b, h, i: (b, 0, h)),
                  pl.BlockSpec(lamv.shape, lambda b, h, i: (0, 0)),
                  pl.BlockSpec((1, LANES), lambda b, h, i: (0, 0))],
        out_specs=pl.BlockSpec((1, tq, LANES), lambda b, h, i: (b, i, h)),
        out_shape=jax.ShapeDtypeStruct(aq.shape, BF16),
        compiler_params=_cparams(("parallel", "parallel", "parallel")),
        name="diff_attn",
    )(aq, akc, ak, avc, av, lamv, gout)


NA_QROWS = 4
NA_KROWS = 12


def _na_window_start(rb, rows):
    return jnp.clip(rb * NA_QROWS - NA_KH // 2, 0, rows - NA_KROWS)


def _na_kernel(q_ref, kc_ref, k_ref, vc_ref, v_ref, bias_ref, o_ref, *, rows):
    rb = pl.program_id(2)
    n_rb = rows // NA_QROWS
    variant = jnp.where(rb == 0, 0, jnp.where(rb == n_rb - 1, 2, 1))
    start = pl.multiple_of(_na_window_start(rb, rows) * GRID_W, GRID_W)
    q = q_ref[0]
    kw = k_ref[0, pl.ds(start, NA_KROWS * GRID_W), :]
    vw = v_ref[0, pl.ds(start, NA_KROWS * GRID_W), :]
    kc = kc_ref[0]
    vc = vc_ref[0]
    lane = lax.broadcasted_iota(jnp.int32, q.shape, 1)
    out = None
    for hh, qm in enumerate(_split_halves(q)):
        sel = (lane < HEAD_DIM) if hh == 0 else (lane >= HEAD_DIM)
        sw = lax.dot_general(qm, kw, _NT, preferred_element_type=F32) + bias_ref[0, hh, variant]
        sc = lax.dot_general(qm, kc, _NT, preferred_element_type=F32)
        m = jnp.maximum(jnp.max(sw, axis=-1, keepdims=True), jnp.max(sc, axis=-1, keepdims=True))
        pw = jnp.exp(sw - m)
        pc = jnp.exp(sc - m)
        inv = 1.0 / (jnp.sum(pw, axis=-1, keepdims=True) + jnp.sum(pc, axis=-1, keepdims=True))
        o = (jnp.dot((pw * inv).astype(BF16), vw, preferred_element_type=F32)
             + jnp.dot((pc * inv).astype(BF16), vc, preferred_element_type=F32))
        out = jnp.where(sel, o, 0.0) if out is None else jnp.where(sel, o, out)
    o_ref[0] = out.astype(BF16)


def _na_attn(nq, nk, nv, nkc, nvc, bias):
    bsz, s, _ = nq.shape
    c = nkc.shape[1]
    rows = s // GRID_W
    tq = NA_QROWS * GRID_W
    n_pairs = NA_HEADS // 2
    return pl.pallas_call(
        functools.partial(_na_kernel, rows=rows),
        grid=(n_pairs, bsz, rows // NA_QROWS),
        in_specs=[pl.BlockSpec((1, tq, LANES), lambda p, b, i: (b, i, p)),
                  pl.BlockSpec((1, c, LANES), lambda p, b, i: (b, 0, p)),
                  pl.BlockSpec((1, s, LANES), lambda p, b, i: (b, 0, p)),
                  pl.BlockSpec((1, c, LANES), lambda p, b, i: (b, 0, p)),
                  pl.BlockSpec((1, s, LANES), lambda p, b, i: (b, 0, p)),
                  pl.BlockSpec((1,) + bias.shape[1:], lambda p, b, i: (p, 0, 0, 0, 0))],
        out_specs=pl.BlockSpec((1, tq, LANES), lambda p, b, i: (b, i, p)),
        out_shape=jax.ShapeDtypeStruct(nq.shape, BF16),
        compiler_params=_cparams(("parallel", "parallel", "parallel")),
        name="na_attn",
    )(nq, nkc, nk, nvc, nv, bias)


def _na_bias_indices(rows):
    n_rb = rows // NA_QROWS
    idx, valid = [], []
    for rb in (0, 1, n_rb - 1):
        ws = int(np.clip(rb * NA_QROWS - NA_KH // 2, 0, rows - NA_KROWS))
        r = rb * NA_QROWS + np.arange(NA_QROWS)[:, None, None, None]
        qc = np.arange(GRID_W)[None, :, None, None]
        kr = ws + np.arange(NA_KROWS)[None, None, :, None]
        kcol = np.arange(GRID_W)[None, None, None, :]
        rs = np.clip(r - NA_KH // 2, 0, rows - NA_KH)
        cs = np.clip(qc - NA_KW // 2, 0, GRID_W - NA_KW)
        ok = (kr >= rs) & (kr < rs + NA_KH) & (kcol >= cs) & (kcol < cs + NA_KW)
        ri = np.clip(kr - r + NA_KH - 1, 0, 2 * NA_KH - 2)
        ci = np.clip(kcol - qc + NA_KW - 1, 0, 2 * NA_KW - 2)
        flat = ri * (2 * NA_KW - 1) + ci + 0 * ok
        n_q, n_k = NA_QROWS * GRID_W, NA_KROWS * GRID_W
        idx.append(np.broadcast_to(flat, ok.shape).reshape(n_q, n_k))
        valid.append(ok.reshape(n_q, n_k))
    return np.stack(idx).astype(np.int32), np.stack(valid)


def _out_proj_kernel(h_ref, gate_ref, oa_ref, ob_ref, wa_ref, wb_ref, o_ref):
    acc = (jnp.dot(oa_ref[0], wa_ref[...], preferred_element_type=F32)
           + jnp.dot(ob_ref[0], wb_ref[...], preferred_element_type=F32))
    o_ref[0] = h_ref[0] + gate_ref[0, 0] * acc


def _out_proj(h, mod4, oa, ob, w_out, *, tm):
    bsz, t, d = h.shape
    half = w_out.shape[0] // 2
    return pl.pallas_call(
        _out_proj_kernel,
        grid=(bsz, t // tm),
        in_specs=[pl.BlockSpec((1, tm, d), lambda b, i: (b, i, 0)),
                  _mod_spec(5, False),
                  pl.BlockSpec((1, tm, half), lambda b, i: (b, i, 0)),
                  pl.BlockSpec((1, tm, half), lambda b, i: (b, i, 0)),
                  pl.BlockSpec((half, d), lambda b, i: (0, 0)),
                  pl.BlockSpec((half, d), lambda b, i: (1, 0))],
        out_specs=pl.BlockSpec((1, tm, d), lambda b, i: (b, i, 0)),
        out_shape=jax.ShapeDtypeStruct(h.shape, F32),
        compiler_params=_cparams(("parallel", "parallel")),
        name="out_proj",
    )(h, mod4, oa, ob, w_out, w_out)


def _rope_tables(n_tokens):
    t = jnp.arange(n_tokens, dtype=jnp.int32)
    row = (t // GRID_W).astype(F32)
    col = (t % GRID_W).astype(F32)
    n_freq = HEAD_DIM // 4
    inv_freq = ROPE_BASE ** (-jnp.arange(n_freq, dtype=F32) / n_freq)
    ar = row[:, None] * inv_freq
    ac = col[:, None] * inv_freq
    cos = jnp.concatenate([jnp.cos(ar), jnp.cos(ar), jnp.cos(ac), jnp.cos(ac)], axis=-1)
    sin = jnp.concatenate([-jnp.sin(ar), jnp.sin(ar), -jnp.sin(ac), jnp.sin(ac)], axis=-1)
    reps = LANES // HEAD_DIM
    return jnp.tile(cos, (1, reps)), jnp.tile(sin, (1, reps))


def kernel(x, c, ctx, c_ctx, w_ada, b_ada, norm1, norm2, norm3, ffn1_w_gu, ffn1_w_down, w_in,
           diff_q_norm, diff_k_norm, lam_q1, lam_k1, lam_q2, lam_k2, diff_out_norm,
           na_q_norm, na_k_norm, na_rpb, w_out, ffn2_w_gu, ffn2_w_down):
    bsz, s, d = x.shape
    assert w_ada.shape[0] == 1 and d == D_MODEL and s % (NA_QROWS * GRID_W) == 0
    rows = s // GRID_W

    cond = jnp.zeros((16, d), F32).at[:bsz].set(c).at[bsz].set(c_ctx)
    mod4 = _adaln(cond, w_ada[0], b_ada).reshape(16, N_MOD, 1, d)

    w_gu1 = ffn1_w_gu[0].astype(BF16)
    w_dn1 = ffn1_w_down[0].astype(BF16)
    h = _ffn(x, mod4, 0, False, norm1, w_gu1, w_dn1, tm=1024)
    n_ctx = ctx.shape[1]
    hc = _ffn(ctx.reshape(1, bsz * n_ctx, d), mod4, 0, True, norm1, w_gu1, w_dn1, tm=1024)

    w_in_b = w_in[0].astype(BF16)
    seg = np.arange(GROUP_W) // HEAD_DIM
    bd = jnp.asarray(seg[:, None] == seg[None, :], BF16)
    cos, sin = _rope_tables(s)
    reps = GROUP_W // HEAD_DIM
    gains = [jnp.tile(g, (1, reps)) for g in (diff_q_norm, diff_k_norm, na_q_norm, na_k_norm)]
    qs = HEAD_DIM ** -0.5
    lat_groups = ((0, 0, True, qs), (1, 1, True, 1.0), (2, None, False, 1.0),
                  (3, 2, False, qs), (4, 3, False, 1.0), (5, None, False, 1.0))
    aq, ak, av, nq, nk, nv = _in_proj(h, mod4, False, norm2, w_in_b, bd, cos, sin, gains,
                                      lat_groups, tm=512)
    ctx_groups = ((1, 1, False, 1.0), (2, None, False, 1.0), (4, 3, False, 1.0), (5, None, False, 1.0))
    ctx_kv = _in_proj(hc, mod4, True, norm2, w_in_b, bd, cos, sin, gains, ctx_groups, tm=512)
    akc, avc, nkc, nvc = (a.reshape(bsz, n_ctx, GROUP_W) for a in ctx_kv)

    lamv = jnp.concatenate([lam_q1, lam_k1, lam_q2, lam_k2], axis=0)
    oa = _diff_attn(aq, ak, av, akc, avc, lamv, diff_out_norm, tq=256)

    idx, valid = _na_bias_indices(rows)
    rpb = na_rpb[0].reshape(NA_HEADS, -1)
    bias = jnp.where(valid[None], jnp.take(rpb, idx, axis=1), NEG_INF)
    bias = bias.reshape((NA_HEADS // 2, 2) + bias.shape[1:])
    ob = _na_attn(nq, nk, nv, nkc, nvc, bias)

    h = _out_proj(h, mod4, oa, ob, w_out[0].astype(BF16), tm=1024)
    return _ffn(h, mod4, 6, False, norm3, ffn2_w_gu[0].astype(BF16), ffn2_w_down[0].astype(BF16), tm=1024)
```

```python
import functools
import math

import numpy as np
import jax
import jax.numpy as jnp
from jax import lax
from jax.experimental import pallas as pl
from jax.experimental.pallas import tpu as pltpu

D_MODEL = 1024
GRID_W = 64
HEAD_DIM = 64
DIFF_HEADS = 4
NA_HEADS = 8
GROUP_W = 512
D_FF = 2816
N_MOD = 9
NA_KH = 8
NA_KW = 16
ROPE_BASE = 10000.0
NORM_EPS = 1e-6
NEG_INF = -1e30
LAM_INIT = 0.8 - 0.6 * math.exp(-0.3 * 0)

LANES = 128
VMEM_LIMIT = 56 * 1024 * 1024

F32 = jnp.float32
BF16 = jnp.bfloat16

_NT = (((1,), (1,)), ((), ()))


def _cparams(sem):
    return pltpu.CompilerParams(dimension_semantics=sem, vmem_limit_bytes=VMEM_LIMIT)


def _adaln_kernel(cond_ref, w_ref, b_ref, o_ref):
    c = cond_ref[...]
    a = (c * jax.nn.sigmoid(c)).astype(BF16)
    o_ref[...] = jnp.dot(a, w_ref[...].astype(BF16), preferred_element_type=F32) + b_ref[...]


def _adaln(cond, w_ada, b_ada):
    rows, d = cond.shape
    n = w_ada.shape[1]
    tn = 1024
    return pl.pallas_call(
        _adaln_kernel,
        grid=(n // tn,),
        in_specs=[pl.BlockSpec((rows, d), lambda j: (0, 0)),
                  pl.BlockSpec((d, tn), lambda j: (0, j)),
                  pl.BlockSpec((1, tn), lambda j: (0, j))],
        out_specs=pl.BlockSpec((rows, tn), lambda j: (0, j)),
        out_shape=jax.ShapeDtypeStruct((rows, n), F32),
        compiler_params=_cparams(("parallel",)),
        name="adaln",
    )(cond, w_ada, b_ada)


def _modulated_norm(x, g, shift, scale):
    ms = jnp.mean(x * x, axis=-1, keepdims=True)
    xn = x * lax.rsqrt(ms + NORM_EPS) * g
    return xn * (1.0 + scale) + shift


def _mod_spec(k, ctx):
    if ctx:
        return pl.BlockSpec((1, 1, 1, D_MODEL), lambda b, *_: (8, k, 0, 0))
    return pl.BlockSpec((1, 1, 1, D_MODEL), lambda b, *_: (b, k, 0, 0))


def _ffn_kernel(h_ref, shift_ref, scale_ref, gate_ref, g_ref, wg_ref, wu_ref, wd_ref, o_ref,
                xm_ref, acc_ref, *, nj):
    j = pl.program_id(2)

    @pl.when(j == 0)
    def _():
        xm = _modulated_norm(h_ref[0], g_ref[...], shift_ref[0, 0], scale_ref[0, 0])
        xm_ref[...] = xm.astype(BF16)
        acc_ref[...] = jnp.zeros_like(acc_ref)

    xm = xm_ref[...]
    g = jnp.dot(xm, wg_ref[...], preferred_element_type=F32)
    u = jnp.dot(xm, wu_ref[...], preferred_element_type=F32)
    a = (g * jax.nn.sigmoid(g) * u).astype(BF16)
    acc_ref[...] += jnp.dot(a, wd_ref[...], preferred_element_type=F32)

    @pl.when(j == nj - 1)
    def _():
        o_ref[0] = h_ref[0] + (0.5 * gate_ref[0, 0]) * acc_ref[...]


def _ffn(h, mod4, k0, ctx, g, w_gu, w_down, *, tm, tf=256):
    bsz, t, d = h.shape
    f = w_down.shape[0]
    nj = f // tf
    return pl.pallas_call(
        functools.partial(_ffn_kernel, nj=nj),
        grid=(bsz, t // tm, nj),
        in_specs=[pl.BlockSpec((1, tm, d), lambda b, i, j: (b, i, 0)),
                  _mod_spec(k0, ctx), _mod_spec(k0 + 1, ctx), _mod_spec(k0 + 2, ctx),
                  pl.BlockSpec((1, d), lambda b, i, j: (0, 0)),
                  pl.BlockSpec((d, tf), lambda b, i, j: (0, j)),
                  pl.BlockSpec((d, tf), lambda b, i, j: (0, nj + j)),
                  pl.BlockSpec((tf, d), lambda b, i, j: (j, 0))],
        out_specs=pl.BlockSpec((1, tm, d), lambda b, i, j: (b, i, 0)),
        out_shape=jax.ShapeDtypeStruct(h.shape, F32),
        scratch_shapes=[pltpu.VMEM((tm, d), BF16), pltpu.VMEM((tm, d), F32)],
        compiler_params=_cparams(("parallel", "parallel", "arbitrary")),
        name="ffn_ctx" if ctx else "ffn",
    )(h, mod4, mod4, mod4, g, w_gu, w_gu, w_down)


def _head_rms_scale(y, bd_ref):
    y2 = y * y
    hi = y2.astype(BF16)
    lo = (y2 - hi.astype(F32)).astype(BF16)
    bd = bd_ref[...]
    ss = jnp.dot(hi, bd, preferred_element_type=F32) + jnp.dot(lo, bd, preferred_element_type=F32)
    return lax.rsqrt(ss * (1.0 / HEAD_DIM) + NORM_EPS)


def _rope(z, cos, sin):
    lane = lax.broadcasted_iota(jnp.int32, z.shape, 1)
    up = pltpu.roll(z, LANES - 16, 1)
    dn = pltpu.roll(z, 16, 1)
    return z * cos + jnp.where((lane & 16) == 0, up, dn) * sin


def _in_proj_kernel(*refs, groups, n_gain):
    h_ref, shift_ref, scale_ref, g_ref, w_ref, bd_ref, cos_ref, sin_ref = refs[:8]
    gain_refs = refs[8:8 + n_gain]
    out_refs = refs[8 + n_gain:]
    xm = _modulated_norm(h_ref[0], g_ref[...], shift_ref[0, 0], scale_ref[0, 0]).astype(BF16)
    for (col, gain_idx, rope, qscale), o_ref in zip(groups, out_refs):
        y = jnp.dot(xm, w_ref[:, col * GROUP_W:(col + 1) * GROUP_W], preferred_element_type=F32)
        if gain_idx is not None:
            gain = gain_refs[gain_idx][...]
            if qscale != 1.0:
                gain = gain * qscale
            y = y * _head_rms_scale(y, bd_ref) * gain
        if rope:
            cos = cos_ref[...]
            sin = sin_ref[...]
            for c in range(GROUP_W // LANES):
                sl = slice(c * LANES, (c + 1) * LANES)
                o_ref[0, :, sl] = _rope(y[:, sl], cos, sin).astype(BF16)
        else:
            o_ref[0] = y.astype(BF16)


def _in_proj(h, mod4, ctx, g, w_in, bd, cos, sin, gains, groups, *, tm):
    bsz, t, d = h.shape
    n_out = len(groups)
    in_specs = [pl.BlockSpec((1, tm, d), lambda b, i: (b, i, 0)),
                _mod_spec(3, ctx), _mod_spec(4, ctx),
                pl.BlockSpec((1, d), lambda b, i: (0, 0)),
                pl.BlockSpec(w_in.shape, lambda b, i: (0, 0)),
                pl.BlockSpec(bd.shape, lambda b, i: (0, 0)),
                pl.BlockSpec((tm, LANES), lambda b, i: (i, 0)),
                pl.BlockSpec((tm, LANES), lambda b, i: (i, 0))]
    in_specs += [pl.BlockSpec((1, GROUP_W), lambda b, i: (0, 0)) for _ in gains]
    return pl.pallas_call(
        functools.partial(_in_proj_kernel, groups=groups, n_gain=len(gains)),
        grid=(bsz, t // tm),
        in_specs=in_specs,
        out_specs=[pl.BlockSpec((1, tm, GROUP_W), lambda b, i: (b, i, 0)) for _ in range(n_out)],
        out_shape=[jax.ShapeDtypeStruct((bsz, t, GROUP_W), BF16) for _ in range(n_out)],
        compiler_params=_cparams(("parallel", "parallel")),
        name="in_proj_ctx" if ctx else "in_proj",
    )(h, mod4, mod4, g, w_in, bd, cos, sin, *gains)


def _split_halves(q):
    lane = lax.broadcasted_iota(jnp.int32, q.shape, 1)
    qf = q.astype(F32)
    lo = jnp.where(lane < HEAD_DIM, qf, 0.0).astype(BF16)
    hi = jnp.where(lane >= HEAD_DIM, qf, 0.0).astype(BF16)
    return lo, hi


def _softmax_parts(qm, kc, kl):
    sc = lax.dot_general(qm, kc, _NT, preferred_element_type=F32)
    sl = lax.dot_general(qm, kl, _NT, preferred_element_type=F32)
    m = jnp.maximum(jnp.max(sc, axis=-1, keepdims=True), jnp.max(sl, axis=-1, keepdims=True))
    pc = jnp.exp(sc - m)
    pl_ = jnp.exp(sl - m)
    l = jnp.sum(pc, axis=-1, keepdims=True) + jnp.sum(pl_, axis=-1, keepdims=True)
    return pc, pl_, l


def _diff_kernel(q_ref, kc_ref, kl_ref, vc_ref, vl_ref, lamv_ref, gout_ref, o_ref):
    q = q_ref[0]
    kc = kc_ref[0]
    kl = kl_ref[0]
    lv = lamv_ref[...]
    lam = (jnp.exp(jnp.sum(lv[0:1] * lv[1:2], axis=-1, keepdims=True))
           - jnp.exp(jnp.sum(lv[2:3] * lv[3:4], axis=-1, keepdims=True)) + LAM_INIT)
    q0, q1 = _split_halves(q)
    pc0, pl0, l0 = _softmax_parts(q0, kc, kl)
    pc1, pl1, l1 = _softmax_parts(q1, kc, kl)
    a0 = 1.0 / l0
    a1 = lam / l1
    wc = (pc0 * a0 - pc1 * a1).astype(BF16)
    wl = (pl0 * a0 - pl1 * a1).astype(BF16)
    o = (jnp.dot(wc, vc_ref[0], preferred_element_type=F32)
         + jnp.dot(wl, vl_ref[0], preferred_element_type=F32))
    ms = jnp.mean(o * o, axis=-1, keepdims=True)
    o_ref[0] = (o * lax.rsqrt(ms + NORM_EPS) * (gout_ref[...] * (1.0 - LAM_INIT))).astype(BF16)


def _diff_attn(aq, ak, av, akc, avc, lamv, gout, *, tq):
    bsz, s, _ = aq.shape
    c = akc.shape[1]
    return pl.pallas_call(
        _diff_kernel,
        grid=(bsz, DIFF_HEADS, s // tq),
        in_specs=[pl.BlockSpec((1, tq, LANES), lambda b, h, i: (b, i, h)),
                  pl.BlockSpec((1, c, LANES), lambda b, h, i: (b, 0, h)),
                  pl.BlockSpec((1, s, LANES), lambda b, h, i: (b, 0, h)),
                  pl.BlockSpec((1, c, LANES), lambda b, h, i: (b, 0, h)),
                  pl.BlockSpec((1, s, LANES), lambda b, h, i: (b, 0, h)),
                  pl.BlockSpec(lamv.shape, lambda b, h, i: (0, 0)),
                  pl.BlockSpec((1, LANES), lambda b, h, i: (0, 0))],
        out_specs=pl.BlockSpec((1, tq, LANES), lambda b, h, i: (b, i, h)),
        out_shape=jax.ShapeDtypeStruct(aq.shape, BF16),
        compiler_params=_cparams(("parallel", "parallel", "parallel")),
        name="diff_attn",
    )(aq, akc, ak, avc, av, lamv, gout)


NA_QROWS = 4
NA_KROWS = 12


def _na_window_start(rb, rows):
    return jnp.clip(rb * NA_QROWS - NA_KH // 2, 0, rows - NA_KROWS)


def _na_variants(rows):
    n_rb = rows // NA_QROWS
    out = []
    for rb in (0, 1, n_rb - 1):
        ws = min(max(rb * NA_QROWS - NA_KH // 2, 0), rows - NA_KROWS)
        out.append((rb * NA_QROWS, ws))
    return out


def _na_build_bias(rpb_ref, bias_ref, rows):
    shape = (GRID_W, LANES)
    qc = lax.broadcasted_iota(jnp.int32, shape, 0)
    lane = lax.broadcasted_iota(jnp.int32, shape, 1)
    kc = lane & (GRID_W - 1)
    cs = jnp.clip(qc - NA_KW // 2, 0, GRID_W - NA_KW)
    col_ok = (kc >= cs) & (kc < cs + NA_KW)
    side_ok = {(True, True): col_ok, (True, False): col_ok & (lane < GRID_W),
               (False, True): col_ok & (lane >= GRID_W)}
    neg = jnp.full(shape, NEG_INF, F32)
    for hh in range(2):
        toeplitz = {}
        for v, (r0, ws) in enumerate(_na_variants(rows)):
            for j in range(NA_QROWS):
                r = r0 + j
                rs = min(max(r - NA_KH // 2, 0), rows - NA_KH)
                for ip in range(NA_KROWS // 2):
                    kr = ws + 2 * ip
                    ok = (rs <= kr < rs + NA_KH, rs <= kr + 1 < rs + NA_KH)
                    tile = neg
                    if ok != (False, False):
                        e = kr - r + NA_KH
                        if e not in toeplitz:
                            vec = jnp.broadcast_to(rpb_ref[0, hh, e:e + 1, :], shape)
                            toeplitz[e] = pltpu.roll(vec, LANES - (NA_KW - 1), 1, stride=1, stride_axis=0)
                        tile = jnp.where(side_ok[ok], toeplitz[e], neg)
                    bias_ref[hh, v, j * GRID_W:(j + 1) * GRID_W, ip * LANES:(ip + 1) * LANES] = tile


def _na_kernel(q_ref, kc_ref, k_ref, vc_ref, v_ref, rpb_ref, o_ref, bias_ref, *, rows):
    rb = pl.program_id(2)
    n_rb = rows // NA_QROWS

    @pl.when((pl.program_id(1) == 0) & (rb == 0))
    def _():
        _na_build_bias(rpb_ref, bias_ref, rows)

    variant = jnp.where(rb == 0, 0, jnp.where(rb == n_rb - 1, 2, 1))
    start = pl.multiple_of(_na_window_start(rb, rows) * GRID_W, GRID_W)
    q = q_ref[0]
    kw = k_ref[0, pl.ds(start, NA_KROWS * GRID_W), :]
    vw = v_ref[0, pl.ds(start, NA_KROWS * GRID_W), :]
    kc = kc_ref[0]
    vc = vc_ref[0]
    lane = lax.broadcasted_iota(jnp.int32, q.shape, 1)
    out = None
    for hh, qm in enumerate(_split_halves(q)):
        sel = (lane < HEAD_DIM) if hh == 0 else (lane >= HEAD_DIM)
        sw = lax.dot_general(qm, kw, _NT, preferred_element_type=F32) + bias_ref[hh, variant]
        sc = lax.dot_general(qm, kc, _NT, preferred_element_type=F32)
        m = jnp.maximum(jnp.max(sw, axis=-1, keepdims=True), jnp.max(sc, axis=-1, keepdims=True))
        pw = jnp.exp(sw - m)
        pc = jnp.exp(sc - m)
        inv = 1.0 / (jnp.sum(pw, axis=-1, keepdims=True) + jnp.sum(pc, axis=-1, keepdims=True))
        o = (jnp.dot((pw * inv).astype(BF16), vw, preferred_element_type=F32)
             + jnp.dot((pc * inv).astype(BF16), vc, preferred_element_type=F32))
        out = jnp.where(sel, o, 0.0) if out is None else jnp.where(sel, o, out)
    o_ref[0] = out.astype(BF16)


def _na_attn(nq, nk, nv, nkc, nvc, rpb):
    bsz, s, _ = nq.shape
    c = nkc.shape[1]
    rows = s // GRID_W
    assert rows // NA_QROWS >= 3 and rows >= NA_KROWS
    tq = NA_QROWS * GRID_W
    n_pairs = NA_HEADS // 2
    n_r, n_c = rpb.shape[1:]
    padded = jnp.pad(rpb, ((0, 0), (1, 1), (0, GRID_W - n_c)))
    rpb2 = jnp.concatenate([padded[:, :-1], padded[:, 1:]], axis=-1).reshape(n_pairs, 2, n_r + 1, LANES)
    return pl.pallas_call(
        functools.partial(_na_kernel, rows=rows),
        grid=(n_pairs, bsz, rows // NA_QROWS),
        in_specs=[pl.BlockSpec((1, tq, LANES), lambda p, b, i: (b, i, p)),
                  pl.BlockSpec((1, c, LANES), lambda p, b, i: (b, 0, p)),
                  pl.BlockSpec((1, s, LANES), lambda p, b, i: (b, 0, p)),
                  pl.BlockSpec((1, c, LANES), lambda p, b, i: (b, 0, p)),
                  pl.BlockSpec((1, s, LANES), lambda p, b, i: (b, 0, p)),
                  pl.BlockSpec((1,) + rpb2.shape[1:], lambda p, b, i: (p, 0, 0, 0))],
        out_specs=pl.BlockSpec((1, tq, LANES), lambda p, b, i: (b, i, p)),
        out_shape=jax.ShapeDtypeStruct(nq.shape, BF16),
        scratch_shapes=[pltpu.VMEM((2, 3, tq, NA_KROWS * GRID_W), F32)],
        compiler_params=_cparams(("arbitrary", "arbitrary", "arbitrary")),
        name="na_attn",
    )(nq, nkc, nk, nvc, nv, rpb2)


def _out_proj_kernel(h_ref, gate_ref, oa_ref, ob_ref, wa_ref, wb_ref, o_ref):
    acc = (jnp.dot(oa_ref[0], wa_ref[...], preferred_element_type=F32)
           + jnp.dot(ob_ref[0], wb_ref[...], preferred_element_type=F32))
    o_ref[0] = h_ref[0] + gate_ref[0, 0] * acc


def _out_proj(h, mod4, oa, ob, w_out, *, tm):
    bsz, t, d = h.shape
    half = w_out.shape[0] // 2
    return pl.pallas_call(
        _out_proj_kernel,
        grid=(bsz, t // tm),
        in_specs=[pl.BlockSpec((1, tm, d), lambda b, i: (b, i, 0)),
                  _mod_spec(5, False),
                  pl.BlockSpec((1, tm, half), lambda b, i: (b, i, 0)),
                  pl.BlockSpec((1, tm, half), lambda b, i: (b, i, 0)),
                  pl.BlockSpec((half, d), lambda b, i: (0, 0)),
                  pl.BlockSpec((half, d), lambda b, i: (1, 0))],
        out_specs=pl.BlockSpec((1, tm, d), lambda b, i: (b, i, 0)),
        out_shape=jax.ShapeDtypeStruct(h.shape, F32),
        compiler_params=_cparams(("parallel", "parallel")),
        name="out_proj",
    )(h, mod4, oa, ob, w_out, w_out)


def _rope_tables(n_tokens):
    t = jnp.arange(n_tokens, dtype=jnp.int32)
    row = (t // GRID_W).astype(F32)
    col = (t % GRID_W).astype(F32)
    n_freq = HEAD_DIM // 4
    inv_freq = ROPE_BASE ** (-jnp.arange(n_freq, dtype=F32) / n_freq)
    ar = row[:, None] * inv_freq
    ac = col[:, None] * inv_freq
    cos = jnp.concatenate([jnp.cos(ar), jnp.cos(ar), jnp.cos(ac), jnp.cos(ac)], axis=-1)
    sin = jnp.concatenate([-jnp.sin(ar), jnp.sin(ar), -jnp.sin(ac), jnp.sin(ac)], axis=-1)
    reps = LANES // HEAD_DIM
    return jnp.tile(cos, (1, reps)), jnp.tile(sin, (1, reps))


def kernel(x, c, ctx, c_ctx, w_ada, b_ada, norm1, norm2, norm3, ffn1_w_gu, ffn1_w_down, w_in,
           diff_q_norm, diff_k_norm, lam_q1, lam_k1, lam_q2, lam_k2, diff_out_norm,
           na_q_norm, na_k_norm, na_rpb, w_out, ffn2_w_gu, ffn2_w_down):
    bsz, s, d = x.shape
    assert w_ada.shape[0] == 1 and d == D_MODEL and s % (NA_QROWS * GRID_W) == 0
    rows = s // GRID_W

    cond = jnp.zeros((16, d), F32).at[:bsz].set(c).at[bsz].set(c_ctx)
    mod4 = _adaln(cond, w_ada[0], b_ada).reshape(16, N_MOD, 1, d)

    w_gu1 = ffn1_w_gu[0].astype(BF16)
    w_dn1 = ffn1_w_down[0].astype(BF16)
    h = _ffn(x, mod4, 0, False, norm1, w_gu1, w_dn1, tm=1024)
    n_ctx = ctx.shape[1]
    hc = _ffn(ctx.reshape(1, bsz * n_ctx, d), mod4, 0, True, norm1, w_gu1, w_dn1, tm=1024)

    w_in_b = w_in[0].astype(BF16)
    seg = np.arange(GROUP_W) // HEAD_DIM
    bd = jnp.asarray(seg[:, None] == seg[None, :], BF16)
    cos, sin = _rope_tables(s)
    reps = GROUP_W // HEAD_DIM
    gains = [jnp.tile(g, (1, reps)) for g in (diff_q_norm, diff_k_norm, na_q_norm, na_k_norm)]
    qs = HEAD_DIM ** -0.5
    lat_groups = ((0, 0, True, qs), (1, 1, True, 1.0), (2, None, False, 1.0),
                  (3, 2, False, qs), (4, 3, False, 1.0), (5, None, False, 1.0))
    aq, ak, av, nq, nk, nv = _in_proj(h, mod4, False, norm2, w_in_b, bd, cos, sin, gains,
                                      lat_groups, tm=512)
    ctx_groups = ((1, 1, False, 1.0), (2, None, False, 1.0), (4, 3, False, 1.0), (5, None, False, 1.0))
    ctx_kv = _in_proj(hc, mod4, True, norm2, w_in_b, bd, cos, sin, gains, ctx_groups, tm=512)
    akc, avc, nkc, nvc = (a.reshape(bsz, n_ctx, GROUP_W) for a in ctx_kv)

    lamv = jnp.concatenate([lam_q1, lam_k1, lam_q2, lam_k2], axis=0)
    oa = _diff_attn(aq, ak, av, akc, avc, lamv, diff_out_norm, tq=256)

    ob = _na_attn(nq, nk, nv, nkc, nvc, na_rpb[0])

    h = _out_proj(h, mod4, oa, ob, w_out[0].astype(BF16), tm=1024)
    return _ffn(h, mod4, 6, False, norm3, ffn2_w_gu[0].astype(BF16), ffn2_w_down[0].astype(BF16), tm=1024)
```

```python
import functools
import math

import numpy as np
import jax
import jax.numpy as jnp
from jax import lax
from jax.experimental import pallas as pl
from jax.experimental.pallas import tpu as pltpu

D_MODEL = 1024
GRID_W = 64
HEAD_DIM = 64
DIFF_HEADS = 4
NA_HEADS = 8
GROUP_W = 512
D_FF = 2816
N_MOD = 9
NA_KH = 8
NA_KW = 16
ROPE_BASE = 10000.0
NORM_EPS = 1e-6
NEG_INF = -1e30
LAM_INIT = 0.8 - 0.6 * math.exp(-0.3 * 0)

LANES = 128
VMEM_LIMIT = 56 * 1024 * 1024

F32 = jnp.float32
BF16 = jnp.bfloat16

_NT = (((1,), (1,)), ((), ()))


def _cparams(sem):
    return pltpu.CompilerParams(dimension_semantics=sem, vmem_limit_bytes=VMEM_LIMIT)


def _adaln_kernel(cond_ref, w_ref, b_ref, o_ref):
    c = cond_ref[...]
    a = (c * jax.nn.sigmoid(c)).astype(BF16)
    o_ref[...] = jnp.dot(a, w_ref[...].astype(BF16), preferred_element_type=F32) + b_ref[...]


def _adaln(cond, w_ada, b_ada):
    rows, d = cond.shape
    n = w_ada.shape[1]
    tn = 1024
    return pl.pallas_call(
        _adaln_kernel,
        grid=(n // tn,),
        in_specs=[pl.BlockSpec((rows, d), lambda j: (0, 0)),
                  pl.BlockSpec((d, tn), lambda j: (0, j)),
                  pl.BlockSpec((1, tn), lambda j: (0, j))],
        out_specs=pl.BlockSpec((rows, tn), lambda j: (0, j)),
        out_shape=jax.ShapeDtypeStruct((rows, n), F32),
        compiler_params=_cparams(("parallel",)),
        name="adaln",
    )(cond, w_ada, b_ada)


def _modulated_norm(x, g, shift, scale):
    ms = jnp.mean(x * x, axis=-1, keepdims=True)
    xn = x * lax.rsqrt(ms + NORM_EPS) * g
    return xn * (1.0 + scale) + shift


def _mod_spec(k, ctx):
    if ctx:
        return pl.BlockSpec((1, 1, 1, D_MODEL), lambda b, *_: (8, k, 0, 0))
    return pl.BlockSpec((1, 1, 1, D_MODEL), lambda b, *_: (b, k, 0, 0))


def _ffn_kernel(h_ref, shift_ref, scale_ref, gate_ref, g_ref, wg_ref, wu_ref, wd_ref, o_ref,
                xm_ref, acc_ref):
    nj = wd_ref.shape[0]
    xm_ref[...] = _modulated_norm(h_ref[0], g_ref[...], shift_ref[0, 0], scale_ref[0, 0]).astype(BF16)

    def act(j):
        xm = xm_ref[...]
        g = jnp.dot(xm, wg_ref[j], preferred_element_type=F32)
        u = jnp.dot(xm, wu_ref[j], preferred_element_type=F32)
        return (g * jax.nn.sigmoid(g) * u).astype(BF16)

    def step(j, a, first=False):
        down = jnp.dot(a, wd_ref[j], preferred_element_type=F32)
        if first:
            acc_ref[...] = down
        else:
            acc_ref[...] += down
        return act(j + 1)

    a = step(0, act(0), first=True)
    a = lax.fori_loop(1, nj - 1, step, a)
    acc = acc_ref[...] + jnp.dot(a, wd_ref[nj - 1], preferred_element_type=F32)
    o_ref[0] = h_ref[0] + (0.5 * gate_ref[0, 0]) * acc


def _ffn_weights(w_gu, w_down, tf=256):
    d, f2 = w_gu.shape
    f = f2 // 2
    w = w_gu.astype(BF16).reshape(d, 2, f // tf, tf)
    return (w[:, 0].transpose(1, 0, 2), w[:, 1].transpose(1, 0, 2),
            w_down.astype(BF16).reshape(f // tf, tf, d))


def _resident(shape):
    return pl.BlockSpec(shape, lambda *_: (0,) * len(shape), pipeline_mode=pl.Buffered(1))


def _ffn(h, mod4, k0, ctx, g, weights, *, tm):
    bsz, t, d = h.shape
    wg, wu, wd = weights
    return pl.pallas_call(
        _ffn_kernel,
        grid=(bsz, t // tm),
        in_specs=[pl.BlockSpec((1, tm, d), lambda b, i: (b, i, 0)),
                  _mod_spec(k0, ctx), _mod_spec(k0 + 1, ctx), _mod_spec(k0 + 2, ctx),
                  pl.BlockSpec((1, d), lambda b, i: (0, 0)),
                  _resident(wg.shape), _resident(wu.shape), _resident(wd.shape)],
        out_specs=pl.BlockSpec((1, tm, d), lambda b, i: (b, i, 0)),
        out_shape=jax.ShapeDtypeStruct(h.shape, F32),
        scratch_shapes=[pltpu.VMEM((tm, d), BF16), pltpu.VMEM((tm, d), F32)],
        compiler_params=_cparams(("parallel", "parallel")),
        name="ffn_ctx" if ctx else "ffn",
    )(h, mod4, mod4, mod4, g, wg, wu, wd)


def _head_rms_scale(y, bd_ref):
    y2 = y * y
    hi = y2.astype(BF16)
    lo = (y2 - hi.astype(F32)).astype(BF16)
    bd = bd_ref[...]
    ss = jnp.dot(hi, bd, preferred_element_type=F32) + jnp.dot(lo, bd, preferred_element_type=F32)
    return lax.rsqrt(ss * (1.0 / HEAD_DIM) + NORM_EPS)


def _rope(z, cos, sin):
    lane = lax.broadcasted_iota(jnp.int32, z.shape, 1)
    up = pltpu.roll(z, LANES - 16, 1)
    dn = pltpu.roll(z, 16, 1)
    return z * cos + jnp.where((lane & 16) == 0, up, dn) * sin


def _in_proj_kernel(*refs, groups, n_gain):
    h_ref, shift_ref, scale_ref, g_ref, w_ref, bd_ref, cos_ref, sin_ref = refs[:8]
    gain_refs = refs[8:8 + n_gain]
    out_refs = refs[8 + n_gain:]
    xm = _modulated_norm(h_ref[0], g_ref[...], shift_ref[0, 0], scale_ref[0, 0]).astype(BF16)
    for (col, gain_idx, rope, qscale), o_ref in zip(groups, out_refs):
        y = jnp.dot(xm, w_ref[:, col * GROUP_W:(col + 1) * GROUP_W], preferred_element_type=F32)
        if gain_idx is not None:
            gain = gain_refs[gain_idx][...]
            if qscale != 1.0:
                gain = gain * qscale
            y = y * _head_rms_scale(y, bd_ref) * gain
        if rope:
            cos = cos_ref[...]
            sin = sin_ref[...]
            for c in range(GROUP_W // LANES):
                sl = slice(c * LANES, (c + 1) * LANES)
                o_ref[0, :, sl] = _rope(y[:, sl], cos, sin).astype(BF16)
        else:
            o_ref[0] = y.astype(BF16)


def _in_proj(h, mod4, ctx, g, w_in, bd, cos, sin, gains, groups, *, tm):
    bsz, t, d = h.shape
    n_out = len(groups)
    in_specs = [pl.BlockSpec((1, tm, d), lambda b, i: (b, i, 0)),
                _mod_spec(3, ctx), _mod_spec(4, ctx),
                pl.BlockSpec((1, d), lambda b, i: (0, 0)),
                pl.BlockSpec(w_in.shape, lambda b, i: (0, 0)),
                pl.BlockSpec(bd.shape, lambda b, i: (0, 0)),
                pl.BlockSpec((tm, LANES), lambda b, i: (i, 0)),
                pl.BlockSpec((tm, LANES), lambda b, i: (i, 0))]
    in_specs += [pl.BlockSpec((1, GROUP_W), lambda b, i: (0, 0)) for _ in gains]
    return pl.pallas_call(
        functools.partial(_in_proj_kernel, groups=groups, n_gain=len(gains)),
        grid=(bsz, t // tm),
        in_specs=in_specs,
        out_specs=[pl.BlockSpec((1, tm, GROUP_W), lambda b, i: (b, i, 0)) for _ in range(n_out)],
        out_shape=[jax.ShapeDtypeStruct((bsz, t, GROUP_W), BF16) for _ in range(n_out)],
        compiler_params=_cparams(("parallel", "parallel")),
        name="in_proj_ctx" if ctx else "in_proj",
    )(h, mod4, mod4, g, w_in, bd, cos, sin, *gains)


def _split_halves(q):
    lane = lax.broadcasted_iota(jnp.int32, q.shape, 1)
    qf = q.astype(F32)
    lo = jnp.where(lane < HEAD_DIM, qf, 0.0).astype(BF16)
    hi = jnp.where(lane >= HEAD_DIM, qf, 0.0).astype(BF16)
    return lo, hi


def _softmax_parts(qm, kc, kl):
    sc = lax.dot_general(qm, kc, _NT, preferred_element_type=F32)
    sl = lax.dot_general(qm, kl, _NT, preferred_element_type=F32)
    m = jnp.maximum(jnp.max(sc, axis=-1, keepdims=True), jnp.max(sl, axis=-1, keepdims=True))
    pc = jnp.exp(sc - m)
    pl_ = jnp.exp(sl - m)
    l = jnp.sum(pc, axis=-1, keepdims=True) + jnp.sum(pl_, axis=-1, keepdims=True)
    return pc, pl_, l


def _diff_kernel(q_ref, kc_ref, kl_ref, vc_ref, vl_ref, lamv_ref, gout_ref, o_ref):
    q = q_ref[0]
    kc = kc_ref[0]
    kl = kl_ref[0]
    lv = lamv_ref[...]
    lam = (jnp.exp(jnp.sum(lv[0:1] * lv[1:2], axis=-1, keepdims=True))
           - jnp.exp(jnp.sum(lv[2:3] * lv[3:4], axis=-1, keepdims=True)) + LAM_INIT)
    q0, q1 = _split_halves(q)
    pc0, pl0, l0 = _softmax_parts(q0, kc, kl)
    pc1, pl1, l1 = _softmax_parts(q1, kc, kl)
    a0 = 1.0 / l0
    a1 = lam / l1
    wc = (pc0 * a0 - pc1 * a1).astype(BF16)
    wl = (pl0 * a0 - pl1 * a1).astype(BF16)
    o = (jnp.dot(wc, vc_ref[0], preferred_element_type=F32)
         + jnp.dot(wl, vl_ref[0], preferred_element_type=F32))
    ms = jnp.mean(o * o, axis=-1, keepdims=True)
    o_ref[0] = (o * lax.rsqrt(ms + NORM_EPS) * (gout_ref[...] * (1.0 - LAM_INIT))).astype(BF16)


def _diff_attn(aq, ak, av, akc, avc, lamv, gout, *, tq):
    bsz, s, _ = aq.shape
    c = akc.shape[1]
    return pl.pallas_call(
        _diff_kernel,
        grid=(bsz, DIFF_HEADS, s // tq),
        in_specs=[pl.BlockSpec((1, tq, LANES), lambda b, h, i: (b, i, h)),
                  pl.BlockSpec((1, c, LANES), lambda b, h, i: (b, 0, h)),
                  pl.BlockSpec((1, s, LANES), lambda b, h, i: (b, 0, h)),
                  pl.BlockSpec((1, c, LANES), lambda b, h, i: (b, 0, h)),
                  pl.BlockSpec((1, s, LANES), lambda b, h, i: (b, 0, h)),
                  pl.BlockSpec(lamv.shape, lambda b, h, i: (0, 0)),
                  pl.BlockSpec((1, LANES), lambda b, h, i: (0, 0))],
        out_specs=pl.BlockSpec((1, tq, LANES), lambda b, h, i: (b, i, h)),
        out_shape=jax.ShapeDtypeStruct(aq.shape, BF16),
        compiler_params=_cparams(("parallel", "parallel", "parallel")),
        name="diff_attn",
    )(aq, akc, ak, avc, av, lamv, gout)


NA_QROWS = 4
NA_KROWS = 12


def _na_window_start(rb, rows):
    return jnp.clip(rb * NA_QROWS - NA_KH // 2, 0, rows - NA_KROWS)


def _na_variants(rows):
    n_rb = rows // NA_QROWS
    out = []
    for rb in (0, 1, n_rb - 1):
        ws = min(max(rb * NA_QROWS - NA_KH // 2, 0), rows - NA_KROWS)
        out.append((rb * NA_QROWS, ws))
    return out


def _na_build_bias(rpb_ref, bias_ref, rows):
    shape = (GRID_W, LANES)
    qc = lax.broadcasted_iota(jnp.int32, shape, 0)
    lane = lax.broadcasted_iota(jnp.int32, shape, 1)
    kc = lane & (GRID_W - 1)
    cs = jnp.clip(qc - NA_KW // 2, 0, GRID_W - NA_KW)
    col_ok = (kc >= cs) & (kc < cs + NA_KW)
    side_ok = {(True, True): col_ok, (True, False): col_ok & (lane < GRID_W),
               (False, True): col_ok & (lane >= GRID_W)}
    neg = jnp.full(shape, NEG_INF, F32)
    for hh in range(2):
        toeplitz = {}
        for v, (r0, ws) in enumerate(_na_variants(rows)):
            for j in range(NA_QROWS):
                r = r0 + j
                rs = min(max(r - NA_KH // 2, 0), rows - NA_KH)
                for ip in range(NA_KROWS // 2):
                    kr = ws + 2 * ip
                    ok = (rs <= kr < rs + NA_KH, rs <= kr + 1 < rs + NA_KH)
                    tile = neg
                    if ok != (False, False):
                        e = kr - r + NA_KH
                        if e not in toeplitz:
                            vec = jnp.broadcast_to(rpb_ref[0, hh, e:e + 1, :], shape)
                            toeplitz[e] = pltpu.roll(vec, LANES - (NA_KW - 1), 1, stride=1, stride_axis=0)
                        tile = jnp.where(side_ok[ok], toeplitz[e], neg)
                    bias_ref[hh, v, j * GRID_W:(j + 1) * GRID_W, ip * LANES:(ip + 1) * LANES] = tile


def _na_kernel(q_ref, kc_ref, k_ref, vc_ref, v_ref, rpb_ref, o_ref, bias_ref, *, rows):
    rb = pl.program_id(2)
    n_rb = rows // NA_QROWS

    @pl.when((pl.program_id(1) == 0) & (rb == 0))
    def _():
        _na_build_bias(rpb_ref, bias_ref, rows)

    variant = jnp.where(rb == 0, 0, jnp.where(rb == n_rb - 1, 2, 1))
    start = pl.multiple_of(_na_window_start(rb, rows) * GRID_W, GRID_W)
    q = q_ref[0]
    kw = k_ref[0, pl.ds(start, NA_KROWS * GRID_W), :]
    vw = v_ref[0, pl.ds(start, NA_KROWS * GRID_W), :]
    kc = kc_ref[0]
    vc = vc_ref[0]
    lane = lax.broadcasted_iota(jnp.int32, q.shape, 1)
    out = None
    for hh, qm in enumerate(_split_halves(q)):
        sel = (lane < HEAD_DIM) if hh == 0 else (lane >= HEAD_DIM)
        sw = lax.dot_general(qm, kw, _NT, preferred_element_type=F32) + bias_ref[hh, variant]
        sc = lax.dot_general(qm, kc, _NT, preferred_element_type=F32)
        m = jnp.maximum(jnp.max(sw, axis=-1, keepdims=True), jnp.max(sc, axis=-1, keepdims=True))
        pw = jnp.exp(sw - m)
        pc = jnp.exp(sc - m)
        inv = 1.0 / (jnp.sum(pw, axis=-1, keepdims=True) + jnp.sum(pc, axis=-1, keepdims=True))
        o = (jnp.dot((pw * inv).astype(BF16), vw, preferred_element_type=F32)
             + jnp.dot((pc * inv).astype(BF16), vc, preferred_element_type=F32))
        out = jnp.where(sel, o, 0.0) if out is None else jnp.where(sel, o, out)
    o_ref[0] = out.astype(BF16)


def _na_attn(nq, nk, nv, nkc, nvc, rpb):
    bsz, s, _ = nq.shape
    c = nkc.shape[1]
    rows = s // GRID_W
    assert rows // NA_QROWS >= 3 and rows >= NA_KROWS
    tq = NA_QROWS * GRID_W
    n_pairs = NA_HEADS // 2
    n_r, n_c = rpb.shape[1:]
    padded = jnp.pad(rpb, ((0, 0), (1, 1), (0, GRID_W - n_c)))
    rpb2 = jnp.concatenate([padded[:, :-1], padded[:, 1:]], axis=-1).reshape(n_pairs, 2, n_r + 1, LANES)
    return pl.pallas_call(
        functools.partial(_na_kernel, rows=rows),
        grid=(n_pairs, bsz, rows // NA_QROWS),
        in_specs=[pl.BlockSpec((1, tq, LANES), lambda p, b, i: (b, i, p)),
                  pl.BlockSpec((1, c, LANES), lambda p, b, i: (b, 0, p)),
                  pl.BlockSpec((1, s, LANES), lambda p, b, i: (b, 0, p)),
                  pl.BlockSpec((1, c, LANES), lambda p, b, i: (b, 0, p)),
                  pl.BlockSpec((1, s, LANES), lambda p, b, i: (b, 0, p)),
                  pl.BlockSpec((1,) + rpb2.shape[1:], lambda p, b, i: (p, 0, 0, 0))],
        out_specs=pl.BlockSpec((1, tq, LANES), lambda p, b, i: (b, i, p)),
        out_shape=jax.ShapeDtypeStruct(nq.shape, BF16),
        scratch_shapes=[pltpu.VMEM((2, 3, tq, NA_KROWS * GRID_W), F32)],
        compiler_params=_cparams(("arbitrary", "arbitrary", "arbitrary")),
        name="na_attn",
    )(nq, nkc, nk, nvc, nv, rpb2)


def _out_proj_kernel(h_ref, gate_ref, oa_ref, ob_ref, wa_ref, wb_ref, o_ref):
    acc = (jnp.dot(oa_ref[0], wa_ref[...], preferred_element_type=F32)
           + jnp.dot(ob_ref[0], wb_ref[...], preferred_element_type=F32))
    o_ref[0] = h_ref[0] + gate_ref[0, 0] * acc


def _out_proj(h, mod4, oa, ob, w_out, *, tm):
    bsz, t, d = h.shape
    half = w_out.shape[0] // 2
    return pl.pallas_call(
        _out_proj_kernel,
        grid=(bsz, t // tm),
        in_specs=[pl.BlockSpec((1, tm, d), lambda b, i: (b, i, 0)),
                  _mod_spec(5, False),
                  pl.BlockSpec((1, tm, half), lambda b, i: (b, i, 0)),
                  pl.BlockSpec((1, tm, half), lambda b, i: (b, i, 0)),
                  pl.BlockSpec((half, d), lambda b, i: (0, 0)),
                  pl.BlockSpec((half, d), lambda b, i: (1, 0))],
        out_specs=pl.BlockSpec((1, tm, d), lambda b, i: (b, i, 0)),
        out_shape=jax.ShapeDtypeStruct(h.shape, F32),
        compiler_params=_cparams(("parallel", "parallel")),
        name="out_proj",
    )(h, mod4, oa, ob, w_out, w_out)


def _rope_tables(n_tokens):
    t = jnp.arange(n_tokens, dtype=jnp.int32)
    row = (t // GRID_W).astype(F32)
    col = (t % GRID_W).astype(F32)
    n_freq = HEAD_DIM // 4
    inv_freq = ROPE_BASE ** (-jnp.arange(n_freq, dtype=F32) / n_freq)
    ar = row[:, None] * inv_freq
    ac = col[:, None] * inv_freq
    cos = jnp.concatenate([jnp.cos(ar), jnp.cos(ar), jnp.cos(ac), jnp.cos(ac)], axis=-1)
    sin = jnp.concatenate([-jnp.sin(ar), jnp.sin(ar), -jnp.sin(ac), jnp.sin(ac)], axis=-1)
    reps = LANES // HEAD_DIM
    return jnp.tile(cos, (1, reps)), jnp.tile(sin, (1, reps))


def kernel(x, c, ctx, c_ctx, w_ada, b_ada, norm1, norm2, norm3, ffn1_w_gu, ffn1_w_down, w_in,
           diff_q_norm, diff_k_norm, lam_q1, lam_k1, lam_q2, lam_k2, diff_out_norm,
           na_q_norm, na_k_norm, na_rpb, w_out, ffn2_w_gu, ffn2_w_down):
    bsz, s, d = x.shape
    assert w_ada.shape[0] == 1 and d == D_MODEL and s % (NA_QROWS * GRID_W) == 0
    rows = s // GRID_W

    cond = jnp.zeros((16, d), F32).at[:bsz].set(c).at[bsz].set(c_ctx)
    mod4 = _adaln(cond, w_ada[0], b_ada).reshape(16, N_MOD, 1, d)

    ffn1_w = _ffn_weights(ffn1_w_gu[0], ffn1_w_down[0])
    h = _ffn(x, mod4, 0, False, norm1, ffn1_w, tm=1024)
    n_ctx = ctx.shape[1]
    hc = _ffn(ctx.reshape(1, bsz * n_ctx, d), mod4, 0, True, norm1, ffn1_w, tm=1024)

    w_in_b = w_in[0].astype(BF16)
    seg = np.arange(GROUP_W) // HEAD_DIM
    bd = jnp.asarray(seg[:, None] == seg[None, :], BF16)
    cos, sin = _rope_tables(s)
    reps = GROUP_W // HEAD_DIM
    gains = [jnp.tile(g, (1, reps)) for g in (diff_q_norm, diff_k_norm, na_q_norm, na_k_norm)]
    qs = HEAD_DIM ** -0.5
    lat_groups = ((0, 0, True, qs), (1, 1, True, 1.0), (2, None, False, 1.0),
                  (3, 2, False, qs), (4, 3, False, 1.0), (5, None, False, 1.0))
    aq, ak, av, nq, nk, nv = _in_proj(h, mod4, False, norm2, w_in_b, bd, cos, sin, gains,
                                      lat_groups, tm=512)
    ctx_groups = ((1, 1, False, 1.0), (2, None, False, 1.0), (4, 3, False, 1.0), (5, None, False, 1.0))
    ctx_kv = _in_proj(hc, mod4, True, norm2, w_in_b, bd, cos, sin, gains, ctx_groups, tm=512)
    akc, avc, nkc, nvc = (a.reshape(bsz, n_ctx, GROUP_W) for a in ctx_kv)

    lamv = jnp.concatenate([lam_q1, lam_k1, lam_q2, lam_k2], axis=0)
    oa = _diff_attn(aq, ak, av, akc, avc, lamv, diff_out_norm, tq=256)

    ob = _na_attn(nq, nk, nv, nkc, nvc, na_rpb[0])

    h = _out_proj(h, mod4, oa, ob, w_out[0].astype(BF16), tm=1024)
    return _ffn(h, mod4, 6, False, norm3, _ffn_weights(ffn2_w_gu[0], ffn2_w_down[0]), tm=1024)
```

```python
import functools
import math

import numpy as np
import jax
import jax.numpy as jnp
from jax import lax
from jax.experimental import pallas as pl
from jax.experimental.pallas import tpu as pltpu

D_MODEL = 1024
GRID_W = 64
HEAD_DIM = 64
DIFF_HEADS = 4
NA_HEADS = 8
GROUP_W = 512
D_FF = 2816
N_MOD = 9
NA_KH = 8
NA_KW = 16
ROPE_BASE = 10000.0
NORM_EPS = 1e-6
NEG_INF = -1e30
LAM_INIT = 0.8 - 0.6 * math.exp(-0.3 * 0)
LOG2E = math.log2(math.e)

LANES = 128
VMEM_LIMIT = 56 * 1024 * 1024

F32 = jnp.float32
BF16 = jnp.bfloat16

_NT = (((1,), (1,)), ((), ()))


def _cparams(sem):
    return pltpu.CompilerParams(dimension_semantics=sem, vmem_limit_bytes=VMEM_LIMIT)


def _adaln_kernel(cond_ref, w_ref, b_ref, o_ref):
    c = cond_ref[...]
    a = (c * jax.nn.sigmoid(c)).astype(BF16)
    o_ref[...] = jnp.dot(a, w_ref[...].astype(BF16), preferred_element_type=F32) + b_ref[...]


def _adaln(cond, w_ada, b_ada):
    rows, d = cond.shape
    n = w_ada.shape[1]
    tn = 1024
    return pl.pallas_call(
        _adaln_kernel,
        grid=(n // tn,),
        in_specs=[pl.BlockSpec((rows, d), lambda j: (0, 0)),
                  pl.BlockSpec((d, tn), lambda j: (0, j)),
                  pl.BlockSpec((1, tn), lambda j: (0, j))],
        out_specs=pl.BlockSpec((rows, tn), lambda j: (0, j)),
        out_shape=jax.ShapeDtypeStruct((rows, n), F32),
        compiler_params=_cparams(("parallel",)),
        name="adaln",
    )(cond, w_ada, b_ada)


def _modulated_norm(x, g, shift, scale):
    ms = jnp.mean(x * x, axis=-1, keepdims=True)
    xn = x * lax.rsqrt(ms + NORM_EPS) * g
    return xn * (1.0 + scale) + shift


def _mod_spec(k, ctx):
    if ctx:
        return pl.BlockSpec((1, 1, 1, D_MODEL), lambda b, *_: (8, k, 0, 0))
    return pl.BlockSpec((1, 1, 1, D_MODEL), lambda b, *_: (b, k, 0, 0))


def _ffn_kernel(h_ref, shift_ref, scale_ref, gate_ref, g_ref, wg_ref, wu_ref, wd_ref, o_ref,
                xm_ref, acc_ref):
    nj = wd_ref.shape[0]
    xm_ref[...] = _modulated_norm(h_ref[0], g_ref[...], shift_ref[0, 0], scale_ref[0, 0]).astype(BF16)

    def act(j):
        xm = xm_ref[...]
        g = jnp.dot(xm, wg_ref[j], preferred_element_type=F32)
        u = jnp.dot(xm, wu_ref[j], preferred_element_type=F32)
        return (g * jax.nn.sigmoid(g) * u).astype(BF16)

    def step(j, a, first=False):
        down = jnp.dot(a, wd_ref[j], preferred_element_type=F32)
        if first:
            acc_ref[...] = down
        else:
            acc_ref[...] += down
        return act(j + 1)

    a = step(0, act(0), first=True)
    a = lax.fori_loop(1, nj - 1, step, a)
    acc = acc_ref[...] + jnp.dot(a, wd_ref[nj - 1], preferred_element_type=F32)
    o_ref[0] = h_ref[0] + (0.5 * gate_ref[0, 0]) * acc


def _ffn_weights(w_gu, w_down, tf=256):
    d, f2 = w_gu.shape
    f = f2 // 2
    w = w_gu.astype(BF16).reshape(d, 2, f // tf, tf)
    return (w[:, 0].transpose(1, 0, 2), w[:, 1].transpose(1, 0, 2),
            w_down.astype(BF16).reshape(f // tf, tf, d))


def _resident(shape):
    return pl.BlockSpec(shape, lambda *_: (0,) * len(shape), pipeline_mode=pl.Buffered(1))


def _ffn(h, mod4, k0, ctx, g, weights, *, tm):
    bsz, t, d = h.shape
    wg, wu, wd = weights
    return pl.pallas_call(
        _ffn_kernel,
        grid=(bsz, t // tm),
        in_specs=[pl.BlockSpec((1, tm, d), lambda b, i: (b, i, 0)),
                  _mod_spec(k0, ctx), _mod_spec(k0 + 1, ctx), _mod_spec(k0 + 2, ctx),
                  pl.BlockSpec((1, d), lambda b, i: (0, 0)),
                  _resident(wg.shape), _resident(wu.shape), _resident(wd.shape)],
        out_specs=pl.BlockSpec((1, tm, d), lambda b, i: (b, i, 0)),
        out_shape=jax.ShapeDtypeStruct(h.shape, F32),
        scratch_shapes=[pltpu.VMEM((tm, d), BF16), pltpu.VMEM((tm, d), F32)],
        compiler_params=_cparams(("parallel", "parallel")),
        name="ffn_ctx" if ctx else "ffn",
    )(h, mod4, mod4, mod4, g, wg, wu, wd)


def _head_rms_scale(y, bd_ref):
    y2 = y * y
    hi = y2.astype(BF16)
    lo = (y2 - hi.astype(F32)).astype(BF16)
    bd = bd_ref[...]
    ss = jnp.dot(hi, bd, preferred_element_type=F32) + jnp.dot(lo, bd, preferred_element_type=F32)
    return lax.rsqrt(ss * (1.0 / HEAD_DIM) + NORM_EPS)


def _rope(z, cos, sin):
    lane = lax.broadcasted_iota(jnp.int32, z.shape, 1)
    up = pltpu.roll(z, LANES - 16, 1)
    dn = pltpu.roll(z, 16, 1)
    return z * cos + jnp.where((lane & 16) == 0, up, dn) * sin


def _in_proj_kernel(*refs, groups, n_gain):
    h_ref, shift_ref, scale_ref, g_ref, w_ref, bd_ref, cos_ref, sin_ref = refs[:8]
    gain_refs = refs[8:8 + n_gain]
    out_refs = refs[8 + n_gain:]
    xm = _modulated_norm(h_ref[0], g_ref[...], shift_ref[0, 0], scale_ref[0, 0]).astype(BF16)
    for (col, gain_idx, rope, qscale), o_ref in zip(groups, out_refs):
        y = jnp.dot(xm, w_ref[:, col * GROUP_W:(col + 1) * GROUP_W], preferred_element_type=F32)
        if gain_idx is not None:
            gain = gain_refs[gain_idx][...]
            if qscale != 1.0:
                gain = gain * qscale
            y = y * _head_rms_scale(y, bd_ref) * gain
        if rope:
            cos = cos_ref[...]
            sin = sin_ref[...]
            for c in range(GROUP_W // LANES):
                sl = slice(c * LANES, (c + 1) * LANES)
                o_ref[0, :, sl] = _rope(y[:, sl], cos, sin).astype(BF16)
        else:
            o_ref[0] = y.astype(BF16)


def _in_proj(h, mod4, ctx, g, w_in, bd, cos, sin, gains, groups, *, tm):
    bsz, t, d = h.shape
    n_out = len(groups)
    in_specs = [pl.BlockSpec((1, tm, d), lambda b, i: (b, i, 0)),
                _mod_spec(3, ctx), _mod_spec(4, ctx),
                pl.BlockSpec((1, d), lambda b, i: (0, 0)),
                pl.BlockSpec(w_in.shape, lambda b, i: (0, 0)),
                pl.BlockSpec(bd.shape, lambda b, i: (0, 0)),
                pl.BlockSpec((tm, LANES), lambda b, i: (i, 0)),
                pl.BlockSpec((tm, LANES), lambda b, i: (i, 0))]
    in_specs += [pl.BlockSpec((1, GROUP_W), lambda b, i: (0, 0)) for _ in gains]
    return pl.pallas_call(
        functools.partial(_in_proj_kernel, groups=groups, n_gain=len(gains)),
        grid=(bsz, t // tm),
        in_specs=in_specs,
        out_specs=[pl.BlockSpec((1, tm, GROUP_W), lambda b, i: (b, i, 0)) for _ in range(n_out)],
        out_shape=[jax.ShapeDtypeStruct((bsz, t, GROUP_W), BF16) for _ in range(n_out)],
        compiler_params=_cparams(("parallel", "parallel")),
        name="in_proj_ctx" if ctx else "in_proj",
    )(h, mod4, mod4, g, w_in, bd, cos, sin, *gains)


def _split_halves(q):
    lane = lax.broadcasted_iota(jnp.int32, q.shape, 1)
    qf = q.astype(F32)
    lo = jnp.where(lane < HEAD_DIM, qf, 0.0).astype(BF16)
    hi = jnp.where(lane >= HEAD_DIM, qf, 0.0).astype(BF16)
    return lo, hi


DIFF_TQ = 256
DIFF_TK = 256
SUBLANES = 8
V_ROWS = LANES + 16


def _diff_kernel(q_ref, kc_ref, kl_ref, vc_ref, vl_ref, lamv_ref, gcol_ref, o_ref,
                 kk_ref, vte_ref, qm_ref, s_ref, m_ref, acc_ref):
    n_ctx = kc_ref.shape[1]
    n_tick = kk_ref.shape[0] // DIFF_TK
    n_blk = q_ref.shape[1] // DIFF_TQ
    kk_ref[0:n_ctx] = kc_ref[0]
    kk_ref[n_ctx:] = kl_ref[0]
    one_row = lax.broadcasted_iota(jnp.int32, (V_ROWS - LANES, DIFF_TK), 0) == 0
    for kc in range(n_tick):
        lo = kc * DIFF_TK
        v = vc_ref[0, lo:lo + DIFF_TK] if lo < n_ctx else vl_ref[0, lo - n_ctx:lo - n_ctx + DIFF_TK]
        vte_ref[kc, 0:LANES, :] = v.astype(F32).T.astype(BF16)
        vte_ref[kc, LANES:, :] = jnp.where(one_row, 1.0, 0.0).astype(BF16)
    lv = lamv_ref[...]
    lam = (jnp.exp(jnp.sum(lv[0:1] * lv[1:2], axis=-1, keepdims=True))
           - jnp.exp(jnp.sum(lv[2:3] * lv[3:4], axis=-1, keepdims=True)) + LAM_INIT)
    acc_ref[...] = jnp.zeros_like(acc_ref)

    def rows(i):
        return pl.ds(pl.multiple_of(i * DIFF_TQ, DIFF_TQ), DIFF_TQ)

    def keys(kc):
        return pl.ds(pl.multiple_of(kc * DIFF_TK, DIFF_TK), DIFF_TK)

    def prep(i, slot):
        lo, hi = _split_halves(q_ref[0, rows(i), :])
        qm_ref[slot, 0] = lo
        qm_ref[slot, 1] = hi
        m_ref[slot] = jnp.full(m_ref.shape[1:], -jnp.inf, F32)

    def score_tick(slot, kc):
        k = kk_ref[keys(kc), :]
        for c in range(2):
            s = lax.dot_general(k, qm_ref[slot, c], _NT, preferred_element_type=F32)
            s_ref[slot, c, keys(kc), :] = s
            part = jnp.max(s.reshape(DIFF_TK // SUBLANES, SUBLANES, DIFF_TQ), axis=0)
            m_ref[slot, c] = jnp.maximum(m_ref[slot, c], part)

    def final_max(slot):
        return [jnp.broadcast_to(jnp.max(m_ref[slot, c], axis=0, keepdims=True), (SUBLANES, DIFF_TQ))
                for c in range(2)]

    def pv_tick(slot, kc, mx):
        for c in range(2):
            s = s_ref[slot, c, keys(kc), :].reshape(DIFF_TK // SUBLANES, SUBLANES, DIFF_TQ)
            p = jnp.exp2(s - mx[c]).reshape(DIFF_TK, DIFF_TQ).astype(BF16)
            acc_ref[c] += jnp.dot(vte_ref[kc], p, preferred_element_type=F32)

    def finish(i):
        l0 = acc_ref[0, LANES:LANES + 1, :]
        l1 = acc_ref[1, LANES:LANES + 1, :]
        ot = acc_ref[0, 0:LANES, :] * (1.0 / l0) - acc_ref[1, 0:LANES, :] * (lam / l1)
        ms = jnp.mean(ot * ot, axis=0, keepdims=True)
        on = ot * lax.rsqrt(ms + NORM_EPS) * (gcol_ref[...] * (1.0 - LAM_INIT))
        o_ref[0, rows(i), :] = on.T.astype(BF16)
        acc_ref[...] = jnp.zeros_like(acc_ref)

    def block(i, slot, last=False):
        mx = final_max(slot)
        if not last:
            prep(i + 1, 1 - slot)

        def tick(kc, carry):
            pv_tick(slot, kc, mx)
            if not last:
                score_tick(1 - slot, kc)
            return carry

        lax.fori_loop(0, n_tick, tick, 0, unroll=True)
        finish(i)

    prep(0, 0)
    lax.fori_loop(0, n_tick, lambda kc, carry: (score_tick(0, kc), carry)[1], 0, unroll=True)

    def pair(t, carry):
        block(2 * t, 0)
        block(2 * t + 1, 1)
        return carry

    lax.fori_loop(0, n_blk // 2 - 1, pair, 0)
    block(n_blk - 2, 0)
    block(n_blk - 1, 1, last=True)


def _diff_attn(aq, ak, av, akc, avc, lamv, gout):
    bsz, s, _ = aq.shape
    c = akc.shape[1]
    assert s % (2 * DIFF_TQ) == 0 and c % DIFF_TK == 0 and s % DIFF_TK == 0
    return pl.pallas_call(
        _diff_kernel,
        grid=(bsz, DIFF_HEADS),
        in_specs=[pl.BlockSpec((1, s, LANES), lambda b, h: (b, 0, h)),
                  pl.BlockSpec((1, c, LANES), lambda b, h: (b, 0, h)),
                  pl.BlockSpec((1, s, LANES), lambda b, h: (b, 0, h)),
                  pl.BlockSpec((1, c, LANES), lambda b, h: (b, 0, h)),
                  pl.BlockSpec((1, s, LANES), lambda b, h: (b, 0, h)),
                  pl.BlockSpec(lamv.shape, lambda b, h: (0, 0)),
                  pl.BlockSpec((LANES, 1), lambda b, h: (0, 0))],
        out_specs=pl.BlockSpec((1, s, LANES), lambda b, h: (b, 0, h)),
        out_shape=jax.ShapeDtypeStruct(aq.shape, BF16),
        scratch_shapes=[pltpu.VMEM((c + s, LANES), BF16),
                        pltpu.VMEM(((c + s) // DIFF_TK, V_ROWS, DIFF_TK), BF16),
                        pltpu.VMEM((2, 2, DIFF_TQ, LANES), BF16),
                        pltpu.VMEM((2, 2, c + s, DIFF_TQ), F32),
                        pltpu.VMEM((2, 2, SUBLANES, DIFF_TQ), F32),
                        pltpu.VMEM((2, V_ROWS, DIFF_TQ), F32)],
        compiler_params=_cparams(("parallel", "parallel")),
        name="diff_attn",
    )(aq, akc, ak, avc, av, lamv, gout.reshape(LANES, 1))


NA_QROWS = 4
NA_KROWS = 12


def _na_window_start(rb, rows):
    return jnp.clip(rb * NA_QROWS - NA_KH // 2, 0, rows - NA_KROWS)


def _na_variants(rows):
    n_rb = rows // NA_QROWS
    out = []
    for rb in (0, 1, n_rb - 1):
        ws = min(max(rb * NA_QROWS - NA_KH // 2, 0), rows - NA_KROWS)
        out.append((rb * NA_QROWS, ws))
    return out


def _na_build_bias(rpb_ref, bias_ref, rows):
    shape = (GRID_W, LANES)
    qc = lax.broadcasted_iota(jnp.int32, shape, 0)
    lane = lax.broadcasted_iota(jnp.int32, shape, 1)
    kc = lane & (GRID_W - 1)
    cs = jnp.clip(qc - NA_KW // 2, 0, GRID_W - NA_KW)
    col_ok = (kc >= cs) & (kc < cs + NA_KW)
    side_ok = {(True, True): col_ok, (True, False): col_ok & (lane < GRID_W),
               (False, True): col_ok & (lane >= GRID_W)}
    neg = jnp.full(shape, NEG_INF, F32)
    for hh in range(2):
        toeplitz = {}
        for v, (r0, ws) in enumerate(_na_variants(rows)):
            for j in range(NA_QROWS):
                r = r0 + j
                rs = min(max(r - NA_KH // 2, 0), rows - NA_KH)
                for ip in range(NA_KROWS // 2):
                    kr = ws + 2 * ip
                    ok = (rs <= kr < rs + NA_KH, rs <= kr + 1 < rs + NA_KH)
                    tile = neg
                    if ok != (False, False):
                        e = kr - r + NA_KH
                        if e not in toeplitz:
                            vec = jnp.broadcast_to(rpb_ref[0, hh, e:e + 1, :], shape)
                            toeplitz[e] = pltpu.roll(vec, LANES - (NA_KW - 1), 1, stride=1, stride_axis=0)
                        tile = jnp.where(side_ok[ok], toeplitz[e], neg)
                    bias_ref[hh, v, j * GRID_W:(j + 1) * GRID_W, ip * LANES:(ip + 1) * LANES] = tile


def _na_kernel(q_ref, kc_ref, k_ref, vc_ref, v_ref, rpb_ref, o_ref, bias_ref, *, rows):
    rb = pl.program_id(2)
    n_rb = rows // NA_QROWS

    @pl.when((pl.program_id(1) == 0) & (rb == 0))
    def _():
        _na_build_bias(rpb_ref, bias_ref, rows)

    variant = jnp.where(rb == 0, 0, jnp.where(rb == n_rb - 1, 2, 1))
    start = pl.multiple_of(_na_window_start(rb, rows) * GRID_W, GRID_W)
    q = q_ref[0]
    kw = k_ref[0, pl.ds(start, NA_KROWS * GRID_W), :]
    vw = v_ref[0, pl.ds(start, NA_KROWS * GRID_W), :]
    kc = kc_ref[0]
    vc = vc_ref[0]
    lane = lax.broadcasted_iota(jnp.int32, q.shape, 1)
    out = None
    for hh, qm in enumerate(_split_halves(q)):
        sel = (lane < HEAD_DIM) if hh == 0 else (lane >= HEAD_DIM)
        sw = lax.dot_general(qm, kw, _NT, preferred_element_type=F32) + bias_ref[hh, variant]
        sc = lax.dot_general(qm, kc, _NT, preferred_element_type=F32)
        m = jnp.maximum(jnp.max(sw, axis=-1, keepdims=True), jnp.max(sc, axis=-1, keepdims=True))
        pw = jnp.exp(sw - m)
        pc = jnp.exp(sc - m)
        inv = 1.0 / (jnp.sum(pw, axis=-1, keepdims=True) + jnp.sum(pc, axis=-1, keepdims=True))
        o = (jnp.dot((pw * inv).astype(BF16), vw, preferred_element_type=F32)
             + jnp.dot((pc * inv).astype(BF16), vc, preferred_element_type=F32))
        out = jnp.where(sel, o, 0.0) if out is None else jnp.where(sel, o, out)
    o_ref[0] = out.astype(BF16)


def _na_attn(nq, nk, nv, nkc, nvc, rpb):
    bsz, s, _ = nq.shape
    c = nkc.shape[1]
    rows = s // GRID_W
    assert rows // NA_QROWS >= 3 and rows >= NA_KROWS
    tq = NA_QROWS * GRID_W
    n_pairs = NA_HEADS // 2
    n_r, n_c = rpb.shape[1:]
    padded = jnp.pad(rpb, ((0, 0), (1, 1), (0, GRID_W - n_c)))
    rpb2 = jnp.concatenate([padded[:, :-1], padded[:, 1:]], axis=-1).reshape(n_pairs, 2, n_r + 1, LANES)
    return pl.pallas_call(
        functools.partial(_na_kernel, rows=rows),
        grid=(n_pairs, bsz, rows // NA_QROWS),
        in_specs=[pl.BlockSpec((1, tq, LANES), lambda p, b, i: (b, i, p)),
                  pl.BlockSpec((1, c, LANES), lambda p, b, i: (b, 0, p)),
                  pl.BlockSpec((1, s, LANES), lambda p, b, i: (b, 0, p)),
                  pl.BlockSpec((1, c, LANES), lambda p, b, i: (b, 0, p)),
                  pl.BlockSpec((1, s, LANES), lambda p, b, i: (b, 0, p)),
                  pl.BlockSpec((1,) + rpb2.shape[1:], lambda p, b, i: (p, 0, 0, 0))],
        out_specs=pl.BlockSpec((1, tq, LANES), lambda p, b, i: (b, i, p)),
        out_shape=jax.ShapeDtypeStruct(nq.shape, BF16),
        scratch_shapes=[pltpu.VMEM((2, 3, tq, NA_KROWS * GRID_W), F32)],
        compiler_params=_cparams(("arbitrary", "arbitrary", "arbitrary")),
        name="na_attn",
    )(nq, nkc, nk, nvc, nv, rpb2)


def _out_proj_kernel(h_ref, gate_ref, oa_ref, ob_ref, wa_ref, wb_ref, o_ref):
    acc = (jnp.dot(oa_ref[0], wa_ref[...], preferred_element_type=F32)
           + jnp.dot(ob_ref[0], wb_ref[...], preferred_element_type=F32))
    o_ref[0] = h_ref[0] + gate_ref[0, 0] * acc


def _out_proj(h, mod4, oa, ob, w_out, *, tm):
    bsz, t, d = h.shape
    half = w_out.shape[0] // 2
    return pl.pallas_call(
        _out_proj_kernel,
        grid=(bsz, t // tm),
        in_specs=[pl.BlockSpec((1, tm, d), lambda b, i: (b, i, 0)),
                  _mod_spec(5, False),
                  pl.BlockSpec((1, tm, half), lambda b, i: (b, i, 0)),
                  pl.BlockSpec((1, tm, half), lambda b, i: (b, i, 0)),
                  pl.BlockSpec((half, d), lambda b, i: (0, 0)),
                  pl.BlockSpec((half, d), lambda b, i: (1, 0))],
        out_specs=pl.BlockSpec((1, tm, d), lambda b, i: (b, i, 0)),
        out_shape=jax.ShapeDtypeStruct(h.shape, F32),
        compiler_params=_cparams(("parallel", "parallel")),
        name="out_proj",
    )(h, mod4, oa, ob, w_out, w_out)


def _rope_tables(n_tokens):
    t = jnp.arange(n_tokens, dtype=jnp.int32)
    row = (t // GRID_W).astype(F32)
    col = (t % GRID_W).astype(F32)
    n_freq = HEAD_DIM // 4
    inv_freq = ROPE_BASE ** (-jnp.arange(n_freq, dtype=F32) / n_freq)
    ar = row[:, None] * inv_freq
    ac = col[:, None] * inv_freq
    cos = jnp.concatenate([jnp.cos(ar), jnp.cos(ar), jnp.cos(ac), jnp.cos(ac)], axis=-1)
    sin = jnp.concatenate([-jnp.sin(ar), jnp.sin(ar), -jnp.sin(ac), jnp.sin(ac)], axis=-1)
    reps = LANES // HEAD_DIM
    return jnp.tile(cos, (1, reps)), jnp.tile(sin, (1, reps))


def kernel(x, c, ctx, c_ctx, w_ada, b_ada, norm1, norm2, norm3, ffn1_w_gu, ffn1_w_down, w_in,
           diff_q_norm, diff_k_norm, lam_q1, lam_k1, lam_q2, lam_k2, diff_out_norm,
           na_q_norm, na_k_norm, na_rpb, w_out, ffn2_w_gu, ffn2_w_down):
    bsz, s, d = x.shape
    assert w_ada.shape[0] == 1 and d == D_MODEL and s % (NA_QROWS * GRID_W) == 0
    rows = s // GRID_W

    cond = jnp.zeros((16, d), F32).at[:bsz].set(c).at[bsz].set(c_ctx)
    mod4 = _adaln(cond, w_ada[0], b_ada).reshape(16, N_MOD, 1, d)

    ffn1_w = _ffn_weights(ffn1_w_gu[0], ffn1_w_down[0])
    h = _ffn(x, mod4, 0, False, norm1, ffn1_w, tm=1024)
    n_ctx = ctx.shape[1]
    hc = _ffn(ctx.reshape(1, bsz * n_ctx, d), mod4, 0, True, norm1, ffn1_w, tm=1024)

    w_in_b = w_in[0].astype(BF16)
    seg = np.arange(GROUP_W) // HEAD_DIM
    bd = jnp.asarray(seg[:, None] == seg[None, :], BF16)
    cos, sin = _rope_tables(s)
    reps = GROUP_W // HEAD_DIM
    gains = [jnp.tile(g, (1, reps)) for g in (diff_q_norm, diff_k_norm, na_q_norm, na_k_norm)]
    qs = HEAD_DIM ** -0.5
    lat_groups = ((0, 0, True, qs * LOG2E), (1, 1, True, 1.0), (2, None, False, 1.0),
                  (3, 2, False, qs), (4, 3, False, 1.0), (5, None, False, 1.0))
    aq, ak, av, nq, nk, nv = _in_proj(h, mod4, False, norm2, w_in_b, bd, cos, sin, gains,
                                      lat_groups, tm=512)
    ctx_groups = ((1, 1, False, 1.0), (2, None, False, 1.0), (4, 3, False, 1.0), (5, None, False, 1.0))
    ctx_kv = _in_proj(hc, mod4, True, norm2, w_in_b, bd, cos, sin, gains, ctx_groups, tm=512)
    akc, avc, nkc, nvc = (a.reshape(bsz, n_ctx, GROUP_W) for a in ctx_kv)

    lamv = jnp.concatenate([lam_q1, lam_k1, lam_q2, lam_k2], axis=0)
    oa = _diff_attn(aq, ak, av, akc, avc, lamv, diff_out_norm)

    ob = _na_attn(nq, nk, nv, nkc, nvc, na_rpb[0])

    h = _out_proj(h, mod4, oa, ob, w_out[0].astype(BF16), tm=1024)
    return _ffn(h, mod4, 6, False, norm3, _ffn_weights(ffn2_w_gu[0], ffn2_w_down[0]), tm=1024)
```

```python
import functools
import math

import numpy as np
import jax
import jax.numpy as jnp
from jax import lax
from jax.experimental import pallas as pl
from jax.experimental.pallas import tpu as pltpu

D_MODEL = 1024
GRID_W = 64
HEAD_DIM = 64
DIFF_HEADS = 4
NA_HEADS = 8
GROUP_W = 512
D_FF = 2816
N_MOD = 9
NA_KH = 8
NA_KW = 16
ROPE_BASE = 10000.0
NORM_EPS = 1e-6
NEG_INF = -1e30
LAM_INIT = 0.8 - 0.6 * math.exp(-0.3 * 0)
LOG2E = math.log2(math.e)

LANES = 128
VMEM_LIMIT = 56 * 1024 * 1024

F32 = jnp.float32
BF16 = jnp.bfloat16

_NT = (((1,), (1,)), ((), ()))


def _cparams(sem):
    return pltpu.CompilerParams(dimension_semantics=sem, vmem_limit_bytes=VMEM_LIMIT)


def _adaln_kernel(cond_ref, w_ref, b_ref, o_ref):
    c = cond_ref[...]
    a = (c * jax.nn.sigmoid(c)).astype(BF16)
    o_ref[...] = jnp.dot(a, w_ref[...].astype(BF16), preferred_element_type=F32) + b_ref[...]


def _adaln(cond, w_ada, b_ada):
    rows, d = cond.shape
    n = w_ada.shape[1]
    tn = 1024
    return pl.pallas_call(
        _adaln_kernel,
        grid=(n // tn,),
        in_specs=[pl.BlockSpec((rows, d), lambda j: (0, 0)),
                  pl.BlockSpec((d, tn), lambda j: (0, j)),
                  pl.BlockSpec((1, tn), lambda j: (0, j))],
        out_specs=pl.BlockSpec((rows, tn), lambda j: (0, j)),
        out_shape=jax.ShapeDtypeStruct((rows, n), F32),
        compiler_params=_cparams(("parallel",)),
        name="adaln",
    )(cond, w_ada, b_ada)


def _modulated_norm(x, g, shift, scale):
    ms = jnp.mean(x * x, axis=-1, keepdims=True)
    xn = x * lax.rsqrt(ms + NORM_EPS) * g
    return xn * (1.0 + scale) + shift


def _mod_spec(k, ctx):
    if ctx:
        return pl.BlockSpec((1, 1, 1, D_MODEL), lambda b, *_: (8, k, 0, 0))
    return pl.BlockSpec((1, 1, 1, D_MODEL), lambda b, *_: (b, k, 0, 0))


def _ffn_kernel(h_ref, shift_ref, scale_ref, gate_ref, g_ref, wg_ref, wu_ref, wd_ref, o_ref,
                xm_ref, acc_ref):
    nj = wd_ref.shape[0]
    xm_ref[...] = _modulated_norm(h_ref[0], g_ref[...], shift_ref[0, 0], scale_ref[0, 0]).astype(BF16)

    def act(j):
        xm = xm_ref[...]
        g = jnp.dot(xm, wg_ref[j], preferred_element_type=F32)
        u = jnp.dot(xm, wu_ref[j], preferred_element_type=F32)
        return (g * jax.nn.sigmoid(g) * u).astype(BF16)

    def step(j, a, first=False):
        down = jnp.dot(a, wd_ref[j], preferred_element_type=F32)
        if first:
            acc_ref[...] = down
        else:
            acc_ref[...] += down
        return act(j + 1)

    a = step(0, act(0), first=True)
    a = lax.fori_loop(1, nj - 1, step, a)
    acc = acc_ref[...] + jnp.dot(a, wd_ref[nj - 1], preferred_element_type=F32)
    o_ref[0] = h_ref[0] + (0.5 * gate_ref[0, 0]) * acc


def _ffn_weights(w_gu, w_down, tf=256):
    d, f2 = w_gu.shape
    f = f2 // 2
    w = w_gu.astype(BF16).reshape(d, 2, f // tf, tf)
    return (w[:, 0].transpose(1, 0, 2), w[:, 1].transpose(1, 0, 2),
            w_down.astype(BF16).reshape(f // tf, tf, d))


def _resident(shape):
    return pl.BlockSpec(shape, lambda *_: (0,) * len(shape), pipeline_mode=pl.Buffered(1))


def _ffn(h, mod4, k0, ctx, g, weights, *, tm):
    bsz, t, d = h.shape
    wg, wu, wd = weights
    return pl.pallas_call(
        _ffn_kernel,
        grid=(bsz, t // tm),
        in_specs=[pl.BlockSpec((1, tm, d), lambda b, i: (b, i, 0)),
                  _mod_spec(k0, ctx), _mod_spec(k0 + 1, ctx), _mod_spec(k0 + 2, ctx),
                  pl.BlockSpec((1, d), lambda b, i: (0, 0)),
                  _resident(wg.shape), _resident(wu.shape), _resident(wd.shape)],
        out_specs=pl.BlockSpec((1, tm, d), lambda b, i: (b, i, 0)),
        out_shape=jax.ShapeDtypeStruct(h.shape, F32),
        scratch_shapes=[pltpu.VMEM((tm, d), BF16), pltpu.VMEM((tm, d), F32)],
        compiler_params=_cparams(("parallel", "parallel")),
        name="ffn_ctx" if ctx else "ffn",
    )(h, mod4, mod4, mod4, g, wg, wu, wd)


def _head_rms_scale(y, bd_ref):
    y2 = y * y
    hi = y2.astype(BF16)
    lo = (y2 - hi.astype(F32)).astype(BF16)
    bd = bd_ref[...]
    ss = jnp.dot(hi, bd, preferred_element_type=F32) + jnp.dot(lo, bd, preferred_element_type=F32)
    return lax.rsqrt(ss * (1.0 / HEAD_DIM) + NORM_EPS)


def _rope(z, cos, sin):
    lane = lax.broadcasted_iota(jnp.int32, z.shape, 1)
    up = pltpu.roll(z, LANES - 16, 1)
    dn = pltpu.roll(z, 16, 1)
    return z * cos + jnp.where((lane & 16) == 0, up, dn) * sin


def _in_proj_kernel(*refs, groups, n_gain):
    h_ref, shift_ref, scale_ref, g_ref, w_ref, bd_ref, cos_ref, sin_ref = refs[:8]
    gain_refs = refs[8:8 + n_gain]
    out_refs = refs[8 + n_gain:]
    xm = _modulated_norm(h_ref[0], g_ref[...], shift_ref[0, 0], scale_ref[0, 0]).astype(BF16)
    for (col, gain_idx, rope, qscale), o_ref in zip(groups, out_refs):
        y = jnp.dot(xm, w_ref[:, col * GROUP_W:(col + 1) * GROUP_W], preferred_element_type=F32)
        if gain_idx is not None:
            gain = gain_refs[gain_idx][...]
            if qscale != 1.0:
                gain = gain * qscale
            y = y * _head_rms_scale(y, bd_ref) * gain
        if rope:
            cos = cos_ref[...]
            sin = sin_ref[...]
            for c in range(GROUP_W // LANES):
                sl = slice(c * LANES, (c + 1) * LANES)
                o_ref[0, :, sl] = _rope(y[:, sl], cos, sin).astype(BF16)
        else:
            o_ref[0] = y.astype(BF16)


def _in_proj(h, mod4, ctx, g, w_in, bd, cos, sin, gains, groups, *, tm):
    bsz, t, d = h.shape
    n_out = len(groups)
    in_specs = [pl.BlockSpec((1, tm, d), lambda b, i: (b, i, 0)),
                _mod_spec(3, ctx), _mod_spec(4, ctx),
                pl.BlockSpec((1, d), lambda b, i: (0, 0)),
                pl.BlockSpec(w_in.shape, lambda b, i: (0, 0)),
                pl.BlockSpec(bd.shape, lambda b, i: (0, 0)),
                pl.BlockSpec((tm, LANES), lambda b, i: (i, 0)),
                pl.BlockSpec((tm, LANES), lambda b, i: (i, 0))]
    in_specs += [pl.BlockSpec((1, GROUP_W), lambda b, i: (0, 0)) for _ in gains]
    return pl.pallas_call(
        functools.partial(_in_proj_kernel, groups=groups, n_gain=len(gains)),
        grid=(bsz, t // tm),
        in_specs=in_specs,
        out_specs=[pl.BlockSpec((1, tm, GROUP_W), lambda b, i: (b, i, 0)) for _ in range(n_out)],
        out_shape=[jax.ShapeDtypeStruct((bsz, t, GROUP_W), BF16) for _ in range(n_out)],
        compiler_params=_cparams(("parallel", "parallel")),
        name="in_proj_ctx" if ctx else "in_proj",
    )(h, mod4, mod4, g, w_in, bd, cos, sin, *gains)


def _split_halves(q):
    lane = lax.broadcasted_iota(jnp.int32, q.shape, 1)
    qf = q.astype(F32)
    lo = jnp.where(lane < HEAD_DIM, qf, 0.0).astype(BF16)
    hi = jnp.where(lane >= HEAD_DIM, qf, 0.0).astype(BF16)
    return lo, hi


DIFF_TQ = 256
DIFF_TK = 256
SUBLANES = 8
V_ROWS = LANES + 16


def _diff_kernel(q_ref, kc_ref, kl_ref, vc_ref, vl_ref, lamv_ref, gcol_ref, o_ref,
                 kk_ref, vte_ref, qm_ref, s_ref, m_ref, acc_ref):
    n_ctx = kc_ref.shape[1]
    n_tick = kk_ref.shape[0] // DIFF_TK
    n_blk = q_ref.shape[1] // DIFF_TQ
    kk_ref[0:n_ctx] = kc_ref[0]
    kk_ref[n_ctx:] = kl_ref[0]
    one_row = lax.broadcasted_iota(jnp.int32, (V_ROWS - LANES, DIFF_TK), 0) == 0
    for kc in range(n_tick):
        lo = kc * DIFF_TK
        v = vc_ref[0, lo:lo + DIFF_TK] if lo < n_ctx else vl_ref[0, lo - n_ctx:lo - n_ctx + DIFF_TK]
        vte_ref[kc, 0:LANES, :] = v.astype(F32).T.astype(BF16)
        vte_ref[kc, LANES:, :] = jnp.where(one_row, 1.0, 0.0).astype(BF16)
    lv = lamv_ref[...]
    lam = (jnp.exp(jnp.sum(lv[0:1] * lv[1:2], axis=-1, keepdims=True))
           - jnp.exp(jnp.sum(lv[2:3] * lv[3:4], axis=-1, keepdims=True)) + LAM_INIT)
    acc_ref[...] = jnp.zeros_like(acc_ref)

    def rows(i):
        return pl.ds(pl.multiple_of(i * DIFF_TQ, DIFF_TQ), DIFF_TQ)

    def keys(kc):
        return pl.ds(pl.multiple_of(kc * DIFF_TK, DIFF_TK), DIFF_TK)

    def prep(i, slot):
        lo, hi = _split_halves(q_ref[0, rows(i), :])
        qm_ref[slot, 0] = lo
        qm_ref[slot, 1] = hi
        m_ref[slot] = jnp.full(m_ref.shape[1:], -jnp.inf, F32)

    def score_tick(slot, kc):
        k = kk_ref[keys(kc), :]
        for c in range(2):
            s = lax.dot_general(k, qm_ref[slot, c], _NT, preferred_element_type=F32)
            s_ref[slot, c, keys(kc), :] = s
            part = jnp.max(s.reshape(DIFF_TK // SUBLANES, SUBLANES, DIFF_TQ), axis=0)
            m_ref[slot, c] = jnp.maximum(m_ref[slot, c], part)

    def final_max(slot):
        return [jnp.broadcast_to(jnp.max(m_ref[slot, c], axis=0, keepdims=True), (SUBLANES, DIFF_TQ))
                for c in range(2)]

    def pv_tick(slot, kc, mx):
        for c in range(2):
            s = s_ref[slot, c, keys(kc), :].reshape(DIFF_TK // SUBLANES, SUBLANES, DIFF_TQ)
            p = jnp.exp2(s - mx[c]).reshape(DIFF_TK, DIFF_TQ).astype(BF16)
            acc_ref[c] += jnp.dot(vte_ref[kc], p, preferred_element_type=F32)

    def finish(i):
        l0 = acc_ref[0, LANES:LANES + 1, :]
        l1 = acc_ref[1, LANES:LANES + 1, :]
        ot = acc_ref[0, 0:LANES, :] * (1.0 / l0) - acc_ref[1, 0:LANES, :] * (lam / l1)
        ms = jnp.mean(ot * ot, axis=0, keepdims=True)
        on = ot * lax.rsqrt(ms + NORM_EPS) * (gcol_ref[...] * (1.0 - LAM_INIT))
        o_ref[0, rows(i), :] = on.T.astype(BF16)
        acc_ref[...] = jnp.zeros_like(acc_ref)

    def block(i, slot, last=False):
        mx = final_max(slot)
        if not last:
            prep(i + 1, 1 - slot)

        def tick(kc, carry):
            pv_tick(slot, kc, mx)
            if not last:
                score_tick(1 - slot, kc)
            return carry

        lax.fori_loop(0, n_tick, tick, 0, unroll=True)
        finish(i)

    prep(0, 0)
    lax.fori_loop(0, n_tick, lambda kc, carry: (score_tick(0, kc), carry)[1], 0, unroll=True)

    def pair(t, carry):
        block(2 * t, 0)
        block(2 * t + 1, 1)
        return carry

    lax.fori_loop(0, n_blk // 2 - 1, pair, 0)
    block(n_blk - 2, 0)
    block(n_blk - 1, 1, last=True)


def _diff_attn(aq, ak, av, akc, avc, lamv, gout):
    bsz, s, _ = aq.shape
    c = akc.shape[1]
    assert s % (2 * DIFF_TQ) == 0 and c % DIFF_TK == 0 and s % DIFF_TK == 0
    return pl.pallas_call(
        _diff_kernel,
        grid=(bsz, DIFF_HEADS),
        in_specs=[pl.BlockSpec((1, s, LANES), lambda b, h: (b, 0, h)),
                  pl.BlockSpec((1, c, LANES), lambda b, h: (b, 0, h)),
                  pl.BlockSpec((1, s, LANES), lambda b, h: (b, 0, h)),
                  pl.BlockSpec((1, c, LANES), lambda b, h: (b, 0, h)),
                  pl.BlockSpec((1, s, LANES), lambda b, h: (b, 0, h)),
                  pl.BlockSpec(lamv.shape, lambda b, h: (0, 0)),
                  pl.BlockSpec((LANES, 1), lambda b, h: (0, 0))],
        out_specs=pl.BlockSpec((1, s, LANES), lambda b, h: (b, 0, h)),
        out_shape=jax.ShapeDtypeStruct(aq.shape, BF16),
        scratch_shapes=[pltpu.VMEM((c + s, LANES), BF16),
                        pltpu.VMEM(((c + s) // DIFF_TK, V_ROWS, DIFF_TK), BF16),
                        pltpu.VMEM((2, 2, DIFF_TQ, LANES), BF16),
                        pltpu.VMEM((2, 2, c + s, DIFF_TQ), F32),
                        pltpu.VMEM((2, 2, SUBLANES, DIFF_TQ), F32),
                        pltpu.VMEM((2, V_ROWS, DIFF_TQ), F32)],
        compiler_params=_cparams(("parallel", "parallel")),
        name="diff_attn",
    )(aq, akc, ak, avc, av, lamv, gout.reshape(LANES, 1))


NA_QROWS = 4
NA_KROWS = 12
NA_TQ = NA_QROWS * GRID_W
NA_TK = 4 * GRID_W
NA_WIN_TICKS = NA_KROWS * GRID_W // NA_TK
NA_V_ROWS = HEAD_DIM + 16


def _na_variants(rows):
    n_rb = rows // NA_QROWS
    out = []
    for rb in (0, 1, n_rb - 1):
        ws = min(max(rb * NA_QROWS - NA_KH // 2, 0), rows - NA_KROWS)
        out.append((rb * NA_QROWS, ws))
    return out


def _na_build_bias(rpb_ref, bias_ref, rows):
    shape = (GRID_W, LANES)
    kc = lax.broadcasted_iota(jnp.int32, shape, 0)
    lane = lax.broadcasted_iota(jnp.int32, shape, 1)
    qc = lane & (GRID_W - 1)
    cs = jnp.clip(qc - NA_KW // 2, 0, GRID_W - NA_KW)
    col_ok = (kc >= cs) & (kc < cs + NA_KW)
    side_ok = {(True, True): col_ok, (True, False): col_ok & (lane < GRID_W),
               (False, True): col_ok & (lane >= GRID_W)}
    neg = jnp.full(shape, NEG_INF, F32)
    for hh in range(2):
        toeplitz = {}
        for v, (r0, ws) in enumerate(_na_variants(rows)):
            for i in range(NA_KROWS):
                kr = ws + i
                for jp in range(NA_QROWS // 2):
                    ok = []
                    for r in (r0 + 2 * jp, r0 + 2 * jp + 1):
                        rs = min(max(r - NA_KH // 2, 0), rows - NA_KH)
                        ok.append(rs <= kr < rs + NA_KH)
                    tile = neg
                    if any(ok):
                        e = kr - (r0 + 2 * jp) + NA_KH - 1
                        if e not in toeplitz:
                            vec = jnp.broadcast_to(rpb_ref[0, hh, e:e + 1, :] * LOG2E, shape)
                            toeplitz[e] = pltpu.roll(vec, LANES - (NA_KW - 1), 1, stride=1, stride_axis=0)
                        tile = jnp.where(side_ok[tuple(ok)], toeplitz[e], neg)
                    bias_ref[hh, v, i * GRID_W:(i + 1) * GRID_W, jp * LANES:(jp + 1) * LANES] = tile


def _na_kernel(q_ref, kc_ref, k_ref, vc_ref, v_ref, rpb_ref, o_ref,
               bias_ref, vte_ref, qm_ref, s_ref, m_ref, acc_ref, *, rows):
    n_blk = rows // NA_QROWS
    n_lat = v_ref.shape[1] // NA_TK
    assert kc_ref.shape[1] == NA_TK
    n_tick = NA_WIN_TICKS + 1

    @pl.when(pl.program_id(1) == 0)
    def _():
        _na_build_bias(rpb_ref, bias_ref, rows)

    one_row = lax.broadcasted_iota(jnp.int32, (NA_V_ROWS - HEAD_DIM, NA_TK), 0) == 0
    for ch in range(n_lat + 1):
        v = v_ref[0, ch * NA_TK:(ch + 1) * NA_TK] if ch < n_lat else vc_ref[0]
        vt = v.astype(F32).T
        for hh in range(2):
            vte_ref[hh, ch, 0:HEAD_DIM, :] = vt[hh * HEAD_DIM:(hh + 1) * HEAD_DIM].astype(BF16)
            vte_ref[hh, ch, HEAD_DIM:, :] = jnp.where(one_row, 1.0, 0.0).astype(BF16)
    acc_ref[...] = jnp.zeros_like(acc_ref)

    def rows_of(i):
        return pl.ds(pl.multiple_of(i * NA_TQ, NA_TQ), NA_TQ)

    def window(i):
        return (jnp.clip(i - 1, 0, n_lat - NA_WIN_TICKS),
                jnp.where(i == 0, 0, jnp.where(i == n_blk - 1, 2, 1)))

    def prep(i, slot):
        lo, hi = _split_halves(q_ref[0, rows_of(i), :])
        qm_ref[slot, 0] = lo
        qm_ref[slot, 1] = hi
        m_ref[slot] = jnp.full(m_ref.shape[1:], -jnp.inf, F32)

    def score_tick(slot, i, t):
        ch0, variant = window(i)
        if t < NA_WIN_TICKS:
            k = k_ref[0, pl.ds(pl.multiple_of((ch0 + t) * NA_TK, NA_TK), NA_TK), :]
        else:
            k = kc_ref[0]
        for hh in range(2):
            s = lax.dot_general(k, qm_ref[slot, hh], _NT, preferred_element_type=F32)
            if t < NA_WIN_TICKS:
                s = s + bias_ref[hh, variant, t * NA_TK:(t + 1) * NA_TK, :]
            s_ref[slot, hh, t * NA_TK:(t + 1) * NA_TK, :] = s
            part = jnp.max(s.reshape(NA_TK // SUBLANES, SUBLANES, NA_TQ), axis=0)
            m_ref[slot, hh] = jnp.maximum(m_ref[slot, hh], part)

    def final_max(slot):
        return [jnp.broadcast_to(jnp.max(m_ref[slot, hh], axis=0, keepdims=True), (SUBLANES, NA_TQ))
                for hh in range(2)]

    def pv_tick(slot, i, t, mx):
        ch = window(i)[0] + t if t < NA_WIN_TICKS else n_lat
        for hh in range(2):
            s = s_ref[slot, hh, t * NA_TK:(t + 1) * NA_TK, :].reshape(NA_TK // SUBLANES, SUBLANES, NA_TQ)
            p = jnp.exp2(s - mx[hh]).reshape(NA_TK, NA_TQ).astype(BF16)
            acc_ref[hh] += jnp.dot(vte_ref[hh, ch], p, preferred_element_type=F32)

    def finish(i):
        ot = jnp.concatenate(
            [acc_ref[hh, 0:HEAD_DIM, :] * (1.0 / acc_ref[hh, HEAD_DIM:HEAD_DIM + 1, :]) for hh in range(2)],
            axis=0)
        o_ref[0, rows_of(i), :] = ot.T.astype(BF16)
        acc_ref[...] = jnp.zeros_like(acc_ref)

    def block(i, slot, last=False):
        mx = final_max(slot)
        if not last:
            prep(i + 1, 1 - slot)
        for t in range(n_tick):
            pv_tick(slot, i, t, mx)
            if not last:
                score_tick(1 - slot, i + 1, t)
        finish(i)

    prep(0, 0)
    for t in range(n_tick):
        score_tick(0, 0, t)

    def pair(u, carry):
        block(2 * u, 0)
        block(2 * u + 1, 1)
        return carry

    lax.fori_loop(0, n_blk // 2 - 1, pair, 0)
    block(n_blk - 2, 0)
    block(n_blk - 1, 1, last=True)


def _na_attn(nq, nk, nv, nkc, nvc, rpb):
    bsz, s, _ = nq.shape
    c = nkc.shape[1]
    rows = s // GRID_W
    n_blk = rows // NA_QROWS
    assert n_blk >= 4 and n_blk % 2 == 0 and NA_QROWS * GRID_W == NA_TK and c == NA_TK
    n_pairs = NA_HEADS // 2
    n_lat = s // NA_TK
    n_r, n_c = rpb.shape[1:]
    padded = jnp.pad(rpb[:, :, ::-1], ((0, 0), (1, 1), (0, GRID_W - n_c)))
    rpb2 = jnp.concatenate([padded[:, 1:], padded[:, :-1]], axis=-1).reshape(n_pairs, 2, n_r + 1, LANES)
    return pl.pallas_call(
        functools.partial(_na_kernel, rows=rows),
        grid=(n_pairs, bsz),
        in_specs=[pl.BlockSpec((1, s, LANES), lambda p, b: (b, 0, p)),
                  pl.BlockSpec((1, c, LANES), lambda p, b: (b, 0, p)),
                  pl.BlockSpec((1, s, LANES), lambda p, b: (b, 0, p)),
                  pl.BlockSpec((1, c, LANES), lambda p, b: (b, 0, p)),
                  pl.BlockSpec((1, s, LANES), lambda p, b: (b, 0, p)),
                  pl.BlockSpec((1,) + rpb2.shape[1:], lambda p, b: (p, 0, 0, 0))],
        out_specs=pl.BlockSpec((1, s, LANES), lambda p, b: (b, 0, p)),
        out_shape=jax.ShapeDtypeStruct(nq.shape, BF16),
        scratch_shapes=[pltpu.VMEM((2, 3, NA_KROWS * GRID_W, NA_TQ), F32),
                        pltpu.VMEM((2, n_lat + 1, NA_V_ROWS, NA_TK), BF16),
                        pltpu.VMEM((2, 2, NA_TQ, LANES), BF16),
                        pltpu.VMEM((2, 2, (NA_WIN_TICKS + 1) * NA_TK, NA_TQ), F32),
                        pltpu.VMEM((2, 2, SUBLANES, NA_TQ), F32),
                        pltpu.VMEM((2, NA_V_ROWS, NA_TQ), F32)],
        compiler_params=_cparams(("arbitrary", "arbitrary")),
        name="na_attn",
    )(nq, nkc, nk, nvc, nv, rpb2)


def _out_proj_kernel(h_ref, gate_ref, oa_ref, ob_ref, wa_ref, wb_ref, o_ref):
    acc = (jnp.dot(oa_ref[0], wa_ref[...], preferred_element_type=F32)
           + jnp.dot(ob_ref[0], wb_ref[...], preferred_element_type=F32))
    o_ref[0] = h_ref[0] + gate_ref[0, 0] * acc


def _out_proj(h, mod4, oa, ob, w_out, *, tm):
    bsz, t, d = h.shape
    half = w_out.shape[0] // 2
    return pl.pallas_call(
        _out_proj_kernel,
        grid=(bsz, t // tm),
        in_specs=[pl.BlockSpec((1, tm, d), lambda b, i: (b, i, 0)),
                  _mod_spec(5, False),
                  pl.BlockSpec((1, tm, half), lambda b, i: (b, i, 0)),
                  pl.BlockSpec((1, tm, half), lambda b, i: (b, i, 0)),
                  pl.BlockSpec((half, d), lambda b, i: (0, 0)),
                  pl.BlockSpec((half, d), lambda b, i: (1, 0))],
        out_specs=pl.BlockSpec((1, tm, d), lambda b, i: (b, i, 0)),
        out_shape=jax.ShapeDtypeStruct(h.shape, F32),
        compiler_params=_cparams(("parallel", "parallel")),
        name="out_proj",
    )(h, mod4, oa, ob, w_out, w_out)


def _rope_tables(n_tokens):
    t = jnp.arange(n_tokens, dtype=jnp.int32)
    row = (t // GRID_W).astype(F32)
    col = (t % GRID_W).astype(F32)
    n_freq = HEAD_DIM // 4
    inv_freq = ROPE_BASE ** (-jnp.arange(n_freq, dtype=F32) / n_freq)
    ar = row[:, None] * inv_freq
    ac = col[:, None] * inv_freq
    cos = jnp.concatenate([jnp.cos(ar), jnp.cos(ar), jnp.cos(ac), jnp.cos(ac)], axis=-1)
    sin = jnp.concatenate([-jnp.sin(ar), jnp.sin(ar), -jnp.sin(ac), jnp.sin(ac)], axis=-1)
    reps = LANES // HEAD_DIM
    return jnp.tile(cos, (1, reps)), jnp.tile(sin, (1, reps))


def kernel(x, c, ctx, c_ctx, w_ada, b_ada, norm1, norm2, norm3, ffn1_w_gu, ffn1_w_down, w_in,
           diff_q_norm, diff_k_norm, lam_q1, lam_k1, lam_q2, lam_k2, diff_out_norm,
           na_q_norm, na_k_norm, na_rpb, w_out, ffn2_w_gu, ffn2_w_down):
    bsz, s, d = x.shape
    assert w_ada.shape[0] == 1 and d == D_MODEL and s % (NA_QROWS * GRID_W) == 0
    rows = s // GRID_W

    cond = jnp.zeros((16, d), F32).at[:bsz].set(c).at[bsz].set(c_ctx)
    mod4 = _adaln(cond, w_ada[0], b_ada).reshape(16, N_MOD, 1, d)

    ffn1_w = _ffn_weights(ffn1_w_gu[0], ffn1_w_down[0])
    h = _ffn(x, mod4, 0, False, norm1, ffn1_w, tm=1024)
    n_ctx = ctx.shape[1]
    hc = _ffn(ctx.reshape(1, bsz * n_ctx, d), mod4, 0, True, norm1, ffn1_w, tm=1024)

    w_in_b = w_in[0].astype(BF16)
    seg = np.arange(GROUP_W) // HEAD_DIM
    bd = jnp.asarray(seg[:, None] == seg[None, :], BF16)
    cos, sin = _rope_tables(s)
    reps = GROUP_W // HEAD_DIM
    gains = [jnp.tile(g, (1, reps)) for g in (diff_q_norm, diff_k_norm, na_q_norm, na_k_norm)]
    qs = HEAD_DIM ** -0.5
    lat_groups = ((0, 0, True, qs * LOG2E), (1, 1, True, 1.0), (2, None, False, 1.0),
                  (3, 2, False, qs * LOG2E), (4, 3, False, 1.0), (5, None, False, 1.0))
    aq, ak, av, nq, nk, nv = _in_proj(h, mod4, False, norm2, w_in_b, bd, cos, sin, gains,
                                      lat_groups, tm=512)
    ctx_groups = ((1, 1, False, 1.0), (2, None, False, 1.0), (4, 3, False, 1.0), (5, None, False, 1.0))
    ctx_kv = _in_proj(hc, mod4, True, norm2, w_in_b, bd, cos, sin, gains, ctx_groups, tm=512)
    akc, avc, nkc, nvc = (a.reshape(bsz, n_ctx, GROUP_W) for a in ctx_kv)

    lamv = jnp.concatenate([lam_q1, lam_k1, lam_q2, lam_k2], axis=0)
    oa = _diff_attn(aq, ak, av, akc, avc, lamv, diff_out_norm)

    ob = _na_attn(nq, nk, nv, nkc, nvc, na_rpb[0])

    h = _out_proj(h, mod4, oa, ob, w_out[0].astype(BF16), tm=1024)
    return _ffn(h, mod4, 6, False, norm3, _ffn_weights(ffn2_w_gu[0], ffn2_w_down[0]), tm=1024)
```

```python
import functools
import math

import numpy as np
import jax
import jax.numpy as jnp
from jax import lax
from jax.experimental import pallas as pl
from jax.experimental.pallas import tpu as pltpu

D_MODEL = 1024
GRID_W = 64
HEAD_DIM = 64
DIFF_HEADS = 4
NA_HEADS = 8
GROUP_W = 512
D_FF = 2816
N_MOD = 9
NA_KH = 8
NA_KW = 16
ROPE_BASE = 10000.0
NORM_EPS = 1e-6
NEG_INF = -1e30
LAM_INIT = 0.8 - 0.6 * math.exp(-0.3 * 0)
LOG2E = math.log2(math.e)

LANES = 128
MXU_W = 256
VMEM_LIMIT = 56 * 1024 * 1024

F32 = jnp.float32
BF16 = jnp.bfloat16

_NT = (((1,), (1,)), ((), ()))


def _cparams(sem):
    return pltpu.CompilerParams(dimension_semantics=sem, vmem_limit_bytes=VMEM_LIMIT)


def _adaln_kernel(cond_ref, w_ref, b_ref, o_ref):
    c = cond_ref[...]
    a = (c * jax.nn.sigmoid(c)).astype(BF16)
    o_ref[...] = jnp.dot(a, w_ref[...].astype(BF16), preferred_element_type=F32) + b_ref[...]


def _adaln(cond, w_ada, b_ada):
    rows, d = cond.shape
    n = w_ada.shape[1]
    tn = 1024
    return pl.pallas_call(
        _adaln_kernel,
        grid=(n // tn,),
        in_specs=[pl.BlockSpec((rows, d), lambda j: (0, 0)),
                  pl.BlockSpec((d, tn), lambda j: (0, j)),
                  pl.BlockSpec((1, tn), lambda j: (0, j))],
        out_specs=pl.BlockSpec((rows, tn), lambda j: (0, j)),
        out_shape=jax.ShapeDtypeStruct((rows, n), F32),
        compiler_params=_cparams(("parallel",)),
        name="adaln",
    )(cond, w_ada, b_ada)


def _modulated_norm(x, g, shift, scale):
    ms = jnp.mean(x * x, axis=-1, keepdims=True)
    xn = x * lax.rsqrt(ms + NORM_EPS) * g
    return xn * (1.0 + scale) + shift


def _mod_spec(k, ctx):
    if ctx:
        return pl.BlockSpec((1, 1, 1, D_MODEL), lambda b, *_: (8, k, 0, 0))
    return pl.BlockSpec((1, 1, 1, D_MODEL), lambda b, *_: (b, k, 0, 0))


FFN_TF = 256


def _ffn_kernel(h_ref, shift_ref, scale_ref, gate_ref, g_ref, wgu_ref, wd_ref, o_ref, xm_ref, acc_ref):
    f = wd_ref.shape[0]
    nj = f // FFN_TF
    xm_ref[...] = _modulated_norm(h_ref[0], g_ref[...], shift_ref[0, 0], scale_ref[0, 0]).astype(BF16)

    def act(j):
        xm = xm_ref[...]
        g = jnp.dot(xm, wgu_ref[:, j * FFN_TF:(j + 1) * FFN_TF], preferred_element_type=F32)
        u = jnp.dot(xm, wgu_ref[:, f + j * FFN_TF:f + (j + 1) * FFN_TF], preferred_element_type=F32)
        return (g * jax.nn.sigmoid(g) * u).astype(BF16)

    a = act(0)
    for j in range(nj):
        down = jnp.dot(a, wd_ref[j * FFN_TF:(j + 1) * FFN_TF, :], preferred_element_type=F32)
        if j + 1 < nj:
            a = act(j + 1)
        if j == 0:
            acc_ref[...] = down
        elif j + 1 < nj:
            acc_ref[...] += down
        else:
            o_ref[0] = h_ref[0] + (0.5 * gate_ref[0, 0]) * (acc_ref[...] + down)


def _ffn_weights(w_gu, w_down):
    return w_gu.astype(BF16), w_down.astype(BF16)


def _resident(shape):
    return pl.BlockSpec(shape, lambda *_: (0,) * len(shape), pipeline_mode=pl.Buffered(1))


def _ffn(h, mod4, k0, ctx, g, weights, *, tm):
    bsz, t, d = h.shape
    wgu, wd = weights
    assert wd.shape[0] % FFN_TF == 0
    return pl.pallas_call(
        _ffn_kernel,
        grid=(bsz, t // tm),
        in_specs=[pl.BlockSpec((1, tm, d), lambda b, i: (b, i, 0)),
                  _mod_spec(k0, ctx), _mod_spec(k0 + 1, ctx), _mod_spec(k0 + 2, ctx),
                  pl.BlockSpec((1, d), lambda b, i: (0, 0)),
                  _resident(wgu.shape), _resident(wd.shape)],
        out_specs=pl.BlockSpec((1, tm, d), lambda b, i: (b, i, 0)),
        out_shape=jax.ShapeDtypeStruct(h.shape, F32),
        scratch_shapes=[pltpu.VMEM((tm, d), BF16), pltpu.VMEM((tm, d), F32)],
        compiler_params=_cparams(("parallel", "parallel")),
        name="ffn_ctx" if ctx else "ffn",
    )(h, mod4, mod4, mod4, g, wgu, wd)


def _head_rms_scale(y, bd_ref):
    y2 = y * y
    hi = y2.astype(BF16)
    lo = (y2 - hi.astype(F32)).astype(BF16)
    bd = bd_ref[...]
    w = bd.shape[0]
    ss = jnp.concatenate(
        [jnp.dot(hi[:, c:c + w], bd, preferred_element_type=F32)
         + jnp.dot(lo[:, c:c + w], bd, preferred_element_type=F32) for c in range(0, y.shape[1], w)],
        axis=-1)
    return lax.rsqrt(ss * (1.0 / HEAD_DIM) + NORM_EPS)


def _rope(z, cos, sin):
    lane = lax.broadcasted_iota(jnp.int32, z.shape, 1)
    up = pltpu.roll(z, LANES - 16, 1)
    dn = pltpu.roll(z, 16, 1)
    return z * cos + jnp.where((lane & 16) == 0, up, dn) * sin


def _in_proj_kernel(*refs, groups, n_gain):
    h_ref, shift_ref, scale_ref, g_ref, w_ref, bd_ref, cos_ref, sin_ref = refs[:8]
    gain_refs = refs[8:8 + n_gain]
    out_refs = refs[8 + n_gain:]
    xm = _modulated_norm(h_ref[0], g_ref[...], shift_ref[0, 0], scale_ref[0, 0]).astype(BF16)
    for (col, gain_idx, rope, qscale), o_ref in zip(groups, out_refs):
        y = jnp.dot(xm, w_ref[:, col * GROUP_W:(col + 1) * GROUP_W], preferred_element_type=F32)
        if gain_idx is not None:
            gain = gain_refs[gain_idx][...]
            if qscale != 1.0:
                gain = gain * qscale
            y = y * _head_rms_scale(y, bd_ref) * gain
        if rope:
            cos = cos_ref[...]
            sin = sin_ref[...]
            for c in range(GROUP_W // LANES):
                sl = slice(c * LANES, (c + 1) * LANES)
                o_ref[0, :, sl] = _rope(y[:, sl], cos, sin).astype(BF16)
        else:
            o_ref[0] = y.astype(BF16)


def _in_proj(h, mod4, ctx, g, w_in, bd, cos, sin, gains, groups, *, tm):
    bsz, t, d = h.shape
    n_out = len(groups)
    in_specs = [pl.BlockSpec((1, tm, d), lambda b, i: (b, i, 0)),
                _mod_spec(3, ctx), _mod_spec(4, ctx),
                pl.BlockSpec((1, d), lambda b, i: (0, 0)),
                pl.BlockSpec(w_in.shape, lambda b, i: (0, 0)),
                pl.BlockSpec(bd.shape, lambda b, i: (0, 0)),
                pl.BlockSpec((tm, LANES), lambda b, i: (i, 0)),
                pl.BlockSpec((tm, LANES), lambda b, i: (i, 0))]
    in_specs += [pl.BlockSpec((1, GROUP_W), lambda b, i: (0, 0)) for _ in gains]
    return pl.pallas_call(
        functools.partial(_in_proj_kernel, groups=groups, n_gain=len(gains)),
        grid=(bsz, t // tm),
        in_specs=in_specs,
        out_specs=[pl.BlockSpec((1, tm, GROUP_W), lambda b, i: (b, i, 0)) for _ in range(n_out)],
        out_shape=[jax.ShapeDtypeStruct((bsz, t, GROUP_W), BF16) for _ in range(n_out)],
        compiler_params=_cparams(("parallel", "parallel")),
        name="in_proj_ctx" if ctx else "in_proj",
    )(h, mod4, mod4, g, w_in, bd, cos, sin, *gains)


def _split_halves(q):
    lane = lax.broadcasted_iota(jnp.int32, q.shape, 1)
    qf = q.astype(F32)
    lo = jnp.where(lane < HEAD_DIM, qf, 0.0).astype(BF16)
    hi = jnp.where(lane >= HEAD_DIM, qf, 0.0).astype(BF16)
    return lo, hi


DIFF_TQ = 256
DIFF_TK = 256
SUBLANES = 8
V_ROWS = LANES + 16


def _diff_kernel(q_ref, kc_ref, kl_ref, vc_ref, vl_ref, lamv_ref, gcol_ref, o_ref,
                 kk_ref, vte_ref, qm_ref, s_ref, m_ref, acc_ref):
    n_ctx = kc_ref.shape[1]
    n_tick = kk_ref.shape[0] // DIFF_TK
    n_blk = q_ref.shape[1] // DIFF_TQ
    kk_ref[0:n_ctx] = kc_ref[0]
    kk_ref[n_ctx:] = kl_ref[0]
    one_row = lax.broadcasted_iota(jnp.int32, (V_ROWS - LANES, DIFF_TK), 0) == 0
    for kc in range(n_tick):
        lo = kc * DIFF_TK
        v = vc_ref[0, lo:lo + DIFF_TK] if lo < n_ctx else vl_ref[0, lo - n_ctx:lo - n_ctx + DIFF_TK]
        vte_ref[kc, 0:LANES, :] = v.astype(F32).T.astype(BF16)
        vte_ref[kc, LANES:, :] = jnp.where(one_row, 1.0, 0.0).astype(BF16)
    lv = lamv_ref[...]
    lam = (jnp.exp(jnp.sum(lv[0:1] * lv[1:2], axis=-1, keepdims=True))
           - jnp.exp(jnp.sum(lv[2:3] * lv[3:4], axis=-1, keepdims=True)) + LAM_INIT)
    acc_ref[...] = jnp.zeros_like(acc_ref)

    def rows(i):
        return pl.ds(pl.multiple_of(i * DIFF_TQ, DIFF_TQ), DIFF_TQ)

    def keys(kc):
        return pl.ds(pl.multiple_of(kc * DIFF_TK, DIFF_TK), DIFF_TK)

    def prep(i, slot):
        lo, hi = _split_halves(q_ref[0, rows(i), :])
        qm_ref[slot, 0] = lo
        qm_ref[slot, 1] = hi
        m_ref[slot] = jnp.full(m_ref.shape[1:], -jnp.inf, F32)

    def score_tick(slot, kc):
        k = kk_ref[keys(kc), :]
        for c in range(2):
            s = lax.dot_general(k, qm_ref[slot, c], _NT, preferred_element_type=F32)
            s_ref[slot, c, keys(kc), :] = s
            part = jnp.max(s.reshape(DIFF_TK // SUBLANES, SUBLANES, DIFF_TQ), axis=0)
            m_ref[slot, c] = jnp.maximum(m_ref[slot, c], part)

    def final_max(slot):
        return [jnp.broadcast_to(jnp.max(m_ref[slot, c], axis=0, keepdims=True), (SUBLANES, DIFF_TQ))
                for c in range(2)]

    def pv_tick(slot, kc, mx):
        for c in range(2):
            s = s_ref[slot, c, keys(kc), :].reshape(DIFF_TK // SUBLANES, SUBLANES, DIFF_TQ)
            p = jnp.exp2(s - mx[c]).reshape(DIFF_TK, DIFF_TQ).astype(BF16)
            acc_ref[c] += jnp.dot(vte_ref[kc], p, preferred_element_type=F32)

    def finish(i):
        l0 = acc_ref[0, LANES:LANES + 1, :]
        l1 = acc_ref[1, LANES:LANES + 1, :]
        ot = acc_ref[0, 0:LANES, :] * (1.0 / l0) - acc_ref[1, 0:LANES, :] * (lam / l1)
        ms = jnp.mean(ot * ot, axis=0, keepdims=True)
        on = ot * lax.rsqrt(ms + NORM_EPS) * (gcol_ref[...] * (1.0 - LAM_INIT))
        o_ref[0, rows(i), :] = on.T.astype(BF16)
        acc_ref[...] = jnp.zeros_like(acc_ref)

    def block(i, slot, last=False):
        mx = final_max(slot)
        if not last:
            prep(i + 1, 1 - slot)

        def tick(kc, carry):
            pv_tick(slot, kc, mx)
            if not last:
                score_tick(1 - slot, kc)
            return carry

        lax.fori_loop(0, n_tick, tick, 0, unroll=True)
        finish(i)

    prep(0, 0)
    lax.fori_loop(0, n_tick, lambda kc, carry: (score_tick(0, kc), carry)[1], 0, unroll=True)

    def pair(t, carry):
        block(2 * t, 0)
        block(2 * t + 1, 1)
        return carry

    lax.fori_loop(0, n_blk // 2 - 1, pair, 0)
    block(n_blk - 2, 0)
    block(n_blk - 1, 1, last=True)


def _diff_attn(aq, ak, av, akc, avc, lamv, gout):
    bsz, s, _ = aq.shape
    c = akc.shape[1]
    assert s % (2 * DIFF_TQ) == 0 and c % DIFF_TK == 0 and s % DIFF_TK == 0
    return pl.pallas_call(
        _diff_kernel,
        grid=(bsz, DIFF_HEADS),
        in_specs=[pl.BlockSpec((1, s, LANES), lambda b, h: (b, 0, h)),
                  pl.BlockSpec((1, c, LANES), lambda b, h: (b, 0, h)),
                  pl.BlockSpec((1, s, LANES), lambda b, h: (b, 0, h)),
                  pl.BlockSpec((1, c, LANES), lambda b, h: (b, 0, h)),
                  pl.BlockSpec((1, s, LANES), lambda b, h: (b, 0, h)),
                  pl.BlockSpec(lamv.shape, lambda b, h: (0, 0)),
                  pl.BlockSpec((LANES, 1), lambda b, h: (0, 0))],
        out_specs=pl.BlockSpec((1, s, LANES), lambda b, h: (b, 0, h)),
        out_shape=jax.ShapeDtypeStruct(aq.shape, BF16),
        scratch_shapes=[pltpu.VMEM((c + s, LANES), BF16),
                        pltpu.VMEM(((c + s) // DIFF_TK, V_ROWS, DIFF_TK), BF16),
                        pltpu.VMEM((2, 2, DIFF_TQ, LANES), BF16),
                        pltpu.VMEM((2, 2, c + s, DIFF_TQ), F32),
                        pltpu.VMEM((2, 2, SUBLANES, DIFF_TQ), F32),
                        pltpu.VMEM((2, V_ROWS, DIFF_TQ), F32)],
        compiler_params=_cparams(("parallel", "parallel")),
        name="diff_attn",
    )(aq, akc, ak, avc, av, lamv, gout.reshape(LANES, 1))


NA_QROWS = 4
NA_KROWS = 12
NA_TQ = NA_QROWS * GRID_W
NA_TK = 4 * GRID_W
NA_WIN_TICKS = NA_KROWS * GRID_W // NA_TK
NA_V_ROWS = HEAD_DIM + 16


def _na_variants(rows):
    n_rb = rows // NA_QROWS
    out = []
    for rb in (0, 1, n_rb - 1):
        ws = min(max(rb * NA_QROWS - NA_KH // 2, 0), rows - NA_KROWS)
        out.append((rb * NA_QROWS, ws))
    return out


def _na_build_bias(rpb_ref, bias_ref, rows):
    shape = (GRID_W, LANES)
    kc = lax.broadcasted_iota(jnp.int32, shape, 0)
    lane = lax.broadcasted_iota(jnp.int32, shape, 1)
    qc = lane & (GRID_W - 1)
    cs = jnp.clip(qc - NA_KW // 2, 0, GRID_W - NA_KW)
    col_ok = (kc >= cs) & (kc < cs + NA_KW)
    side_ok = {(True, True): col_ok, (True, False): col_ok & (lane < GRID_W),
               (False, True): col_ok & (lane >= GRID_W)}
    neg = jnp.full(shape, NEG_INF, F32)
    for hh in range(2):
        toeplitz = {}
        for v, (r0, ws) in enumerate(_na_variants(rows)):
            for i in range(NA_KROWS):
                kr = ws + i
                for jp in range(NA_QROWS // 2):
                    ok = []
                    for r in (r0 + 2 * jp, r0 + 2 * jp + 1):
                        rs = min(max(r - NA_KH // 2, 0), rows - NA_KH)
                        ok.append(rs <= kr < rs + NA_KH)
                    tile = neg
                    if any(ok):
                        e = kr - (r0 + 2 * jp) + NA_KH - 1
                        if e not in toeplitz:
                            vec = jnp.broadcast_to(rpb_ref[0, hh, e:e + 1, :] * LOG2E, shape)
                            toeplitz[e] = pltpu.roll(vec, LANES - (NA_KW - 1), 1, stride=1, stride_axis=0)
                        tile = jnp.where(side_ok[tuple(ok)], toeplitz[e], neg)
                    bias_ref[hh, v, i * GRID_W:(i + 1) * GRID_W, jp * LANES:(jp + 1) * LANES] = tile


def _na_kernel(q_ref, kc_ref, k_ref, vc_ref, v_ref, rpb_ref, o_ref,
               bias_ref, vte_ref, qm_ref, s_ref, m_ref, acc_ref, *, rows):
    n_blk = rows // NA_QROWS
    n_lat = v_ref.shape[1] // NA_TK
    assert kc_ref.shape[1] == NA_TK
    n_tick = NA_WIN_TICKS + 1

    @pl.when(pl.program_id(1) == 0)
    def _():
        _na_build_bias(rpb_ref, bias_ref, rows)

    one_row = lax.broadcasted_iota(jnp.int32, (NA_V_ROWS - HEAD_DIM, NA_TK), 0) == 0
    for ch in range(n_lat + 1):
        v = v_ref[0, ch * NA_TK:(ch + 1) * NA_TK] if ch < n_lat else vc_ref[0]
        vt = v.astype(F32).T
        for hh in range(2):
            vte_ref[hh, ch, 0:HEAD_DIM, :] = vt[hh * HEAD_DIM:(hh + 1) * HEAD_DIM].astype(BF16)
            vte_ref[hh, ch, HEAD_DIM:, :] = jnp.where(one_row, 1.0, 0.0).astype(BF16)
    acc_ref[...] = jnp.zeros_like(acc_ref)

    def rows_of(i):
        return pl.ds(pl.multiple_of(i * NA_TQ, NA_TQ), NA_TQ)

    def window(i):
        return (jnp.clip(i - 1, 0, n_lat - NA_WIN_TICKS),
                jnp.where(i == 0, 0, jnp.where(i == n_blk - 1, 2, 1)))

    def prep(i, slot):
        lo, hi = _split_halves(q_ref[0, rows_of(i), :])
        qm_ref[slot, 0] = lo
        qm_ref[slot, 1] = hi
        m_ref[slot] = jnp.full(m_ref.shape[1:], -jnp.inf, F32)

    def score_tick(slot, i, t):
        ch0, variant = window(i)
        if t < NA_WIN_TICKS:
            k = k_ref[0, pl.ds(pl.multiple_of((ch0 + t) * NA_TK, NA_TK), NA_TK), :]
        else:
            k = kc_ref[0]
        for hh in range(2):
            s = lax.dot_general(k, qm_ref[slot, hh], _NT, preferred_element_type=F32)
            if t < NA_WIN_TICKS:
                s = s + bias_ref[hh, variant, t * NA_TK:(t + 1) * NA_TK, :]
            s_ref[slot, hh, t * NA_TK:(t + 1) * NA_TK, :] = s
            part = jnp.max(s.reshape(NA_TK // SUBLANES, SUBLANES, NA_TQ), axis=0)
            m_ref[slot, hh] = jnp.maximum(m_ref[slot, hh], part)

    def final_max(slot):
        return [jnp.broadcast_to(jnp.max(m_ref[slot, hh], axis=0, keepdims=True), (SUBLANES, NA_TQ))
                for hh in range(2)]

    def pv_tick(slot, i, t, mx):
        ch = window(i)[0] + t if t < NA_WIN_TICKS else n_lat
        for hh in range(2):
            s = s_ref[slot, hh, t * NA_TK:(t + 1) * NA_TK, :].reshape(NA_TK // SUBLANES, SUBLANES, NA_TQ)
            p = jnp.exp2(s - mx[hh]).reshape(NA_TK, NA_TQ).astype(BF16)
            acc_ref[hh] += jnp.dot(vte_ref[hh, ch], p, preferred_element_type=F32)

    def finish(i):
        ot = jnp.concatenate(
            [acc_ref[hh, 0:HEAD_DIM, :] * (1.0 / acc_ref[hh, HEAD_DIM:HEAD_DIM + 1, :]) for hh in range(2)],
            axis=0)
        o_ref[0, rows_of(i), :] = ot.T.astype(BF16)
        acc_ref[...] = jnp.zeros_like(acc_ref)

    def block(i, slot, last=False):
        mx = final_max(slot)
        if not last:
            prep(i + 1, 1 - slot)
        for t in range(n_tick):
            pv_tick(slot, i, t, mx)
            if not last:
                score_tick(1 - slot, i + 1, t)
        finish(i)

    prep(0, 0)
    for t in range(n_tick):
        score_tick(0, 0, t)

    def pair(u, carry):
        block(2 * u, 0)
        block(2 * u + 1, 1)
        return carry

    lax.fori_loop(0, n_blk // 2 - 1, pair, 0)
    block(n_blk - 2, 0)
    block(n_blk - 1, 1, last=True)


def _na_attn(nq, nk, nv, nkc, nvc, rpb):
    bsz, s, _ = nq.shape
    c = nkc.shape[1]
    rows = s // GRID_W
    n_blk = rows // NA_QROWS
    assert n_blk >= 4 and n_blk % 2 == 0 and NA_QROWS * GRID_W == NA_TK and c == NA_TK
    n_pairs = NA_HEADS // 2
    n_lat = s // NA_TK
    n_r, n_c = rpb.shape[1:]
    padded = jnp.pad(rpb[:, :, ::-1], ((0, 0), (1, 1), (0, GRID_W - n_c)))
    rpb2 = jnp.concatenate([padded[:, 1:], padded[:, :-1]], axis=-1).reshape(n_pairs, 2, n_r + 1, LANES)
    return pl.pallas_call(
        functools.partial(_na_kernel, rows=rows),
        grid=(n_pairs, bsz),
        in_specs=[pl.BlockSpec((1, s, LANES), lambda p, b: (b, 0, p)),
                  pl.BlockSpec((1, c, LANES), lambda p, b: (b, 0, p)),
                  pl.BlockSpec((1, s, LANES), lambda p, b: (b, 0, p)),
                  pl.BlockSpec((1, c, LANES), lambda p, b: (b, 0, p)),
                  pl.BlockSpec((1, s, LANES), lambda p, b: (b, 0, p)),
                  pl.BlockSpec((1,) + rpb2.shape[1:], lambda p, b: (p, 0, 0, 0))],
        out_specs=pl.BlockSpec((1, s, LANES), lambda p, b: (b, 0, p)),
        out_shape=jax.ShapeDtypeStruct(nq.shape, BF16),
        scratch_shapes=[pltpu.VMEM((2, 3, NA_KROWS * GRID_W, NA_TQ), F32),
                        pltpu.VMEM((2, n_lat + 1, NA_V_ROWS, NA_TK), BF16),
                        pltpu.VMEM((2, 2, NA_TQ, LANES), BF16),
                        pltpu.VMEM((2, 2, (NA_WIN_TICKS + 1) * NA_TK, NA_TQ), F32),
                        pltpu.VMEM((2, 2, SUBLANES, NA_TQ), F32),
                        pltpu.VMEM((2, NA_V_ROWS, NA_TQ), F32)],
        compiler_params=_cparams(("arbitrary", "arbitrary")),
        name="na_attn",
    )(nq, nkc, nk, nvc, nv, rpb2)


def _out_proj_kernel(h_ref, gate_ref, oa_ref, ob_ref, wa_ref, wb_ref, o_ref):
    acc = (jnp.dot(oa_ref[0], wa_ref[...], preferred_element_type=F32)
           + jnp.dot(ob_ref[0], wb_ref[...], preferred_element_type=F32))
    o_ref[0] = h_ref[0] + gate_ref[0, 0] * acc


def _out_proj(h, mod4, oa, ob, w_out, *, tm):
    bsz, t, d = h.shape
    half = w_out.shape[0] // 2
    return pl.pallas_call(
        _out_proj_kernel,
        grid=(bsz, t // tm),
        in_specs=[pl.BlockSpec((1, tm, d), lambda b, i: (b, i, 0)),
                  _mod_spec(5, False),
                  pl.BlockSpec((1, tm, half), lambda b, i: (b, i, 0)),
                  pl.BlockSpec((1, tm, half), lambda b, i: (b, i, 0)),
                  pl.BlockSpec((half, d), lambda b, i: (0, 0)),
                  pl.BlockSpec((half, d), lambda b, i: (1, 0))],
        out_specs=pl.BlockSpec((1, tm, d), lambda b, i: (b, i, 0)),
        out_shape=jax.ShapeDtypeStruct(h.shape, F32),
        compiler_params=_cparams(("parallel", "parallel")),
        name="out_proj",
    )(h, mod4, oa, ob, w_out, w_out)


def _rope_tables(n_tokens):
    t = jnp.arange(n_tokens, dtype=jnp.int32)
    row = (t // GRID_W).astype(F32)
    col = (t % GRID_W).astype(F32)
    n_freq = HEAD_DIM // 4
    inv_freq = ROPE_BASE ** (-jnp.arange(n_freq, dtype=F32) / n_freq)
    ar = row[:, None] * inv_freq
    ac = col[:, None] * inv_freq
    cos = jnp.concatenate([jnp.cos(ar), jnp.cos(ar), jnp.cos(ac), jnp.cos(ac)], axis=-1)
    sin = jnp.concatenate([-jnp.sin(ar), jnp.sin(ar), -jnp.sin(ac), jnp.sin(ac)], axis=-1)
    reps = LANES // HEAD_DIM
    return jnp.tile(cos, (1, reps)), jnp.tile(sin, (1, reps))


def kernel(x, c, ctx, c_ctx, w_ada, b_ada, norm1, norm2, norm3, ffn1_w_gu, ffn1_w_down, w_in,
           diff_q_norm, diff_k_norm, lam_q1, lam_k1, lam_q2, lam_k2, diff_out_norm,
           na_q_norm, na_k_norm, na_rpb, w_out, ffn2_w_gu, ffn2_w_down):
    bsz, s, d = x.shape
    assert w_ada.shape[0] == 1 and d == D_MODEL and s % (NA_QROWS * GRID_W) == 0
    rows = s // GRID_W

    cond = jnp.zeros((16, d), F32).at[:bsz].set(c).at[bsz].set(c_ctx)
    mod4 = _adaln(cond, w_ada[0], b_ada).reshape(16, N_MOD, 1, d)

    ffn1_w = _ffn_weights(ffn1_w_gu[0], ffn1_w_down[0])
    h = _ffn(x, mod4, 0, False, norm1, ffn1_w, tm=1024)
    n_ctx = ctx.shape[1]
    hc = _ffn(ctx.reshape(1, bsz * n_ctx, d), mod4, 0, True, norm1, ffn1_w, tm=1024)

    w_in_b = w_in[0].astype(BF16)
    seg = np.arange(MXU_W) // HEAD_DIM
    bd = jnp.asarray(seg[:, None] == seg[None, :], BF16)
    cos, sin = _rope_tables(s)
    reps = GROUP_W // HEAD_DIM
    gains = [jnp.tile(g, (1, reps)) for g in (diff_q_norm, diff_k_norm, na_q_norm, na_k_norm)]
    qs = HEAD_DIM ** -0.5
    lat_groups = ((0, 0, True, qs * LOG2E), (1, 1, True, 1.0), (2, None, False, 1.0),
                  (3, 2, False, qs * LOG2E), (4, 3, False, 1.0), (5, None, False, 1.0))
    aq, ak, av, nq, nk, nv = _in_proj(h, mod4, False, norm2, w_in_b, bd, cos, sin, gains,
                                      lat_groups, tm=512)
    ctx_groups = ((1, 1, False, 1.0), (2, None, False, 1.0), (4, 3, False, 1.0), (5, None, False, 1.0))
    ctx_kv = _in_proj(hc, mod4, True, norm2, w_in_b, bd, cos, sin, gains, ctx_groups, tm=512)
    akc, avc, nkc, nvc = (a.reshape(bsz, n_ctx, GROUP_W) for a in ctx_kv)

    lamv = jnp.concatenate([lam_q1, lam_k1, lam_q2, lam_k2], axis=0)
    oa = _diff_attn(aq, ak, av, akc, avc, lamv, diff_out_norm)

    ob = _na_attn(nq, nk, nv, nkc, nvc, na_rpb[0])

    h = _out_proj(h, mod4, oa, ob, w_out[0].astype(BF16), tm=1024)
    return _ffn(h, mod4, 6, False, norm3, _ffn_weights(ffn2_w_gu[0], ffn2_w_down[0]), tm=1024)
```

```python
import functools
import math

import numpy as np
import jax
import jax.numpy as jnp
from jax import lax
from jax.experimental import pallas as pl
from jax.experimental.pallas import tpu as pltpu

D_MODEL = 1024
GRID_W = 64
HEAD_DIM = 64
DIFF_HEADS = 4
NA_HEADS = 8
GROUP_W = 512
D_FF = 2816
N_MOD = 9
NA_KH = 8
NA_KW = 16
ROPE_BASE = 10000.0
NORM_EPS = 1e-6
NEG_INF = -1e30
LAM_INIT = 0.8 - 0.6 * math.exp(-0.3 * 0)
LOG2E = math.log2(math.e)

LANES = 128
MXU_W = 256
VMEM_LIMIT = 56 * 1024 * 1024

F32 = jnp.float32
BF16 = jnp.bfloat16

_NT = (((1,), (1,)), ((), ()))


def _cparams(sem):
    return pltpu.CompilerParams(dimension_semantics=sem, vmem_limit_bytes=VMEM_LIMIT)


ADALN_TK = 128


def _adaln_kernel(cond_ref, w_ref, b_ref, o_ref):
    k = pl.program_id(0)
    c = cond_ref[...]
    a = (c * jax.nn.sigmoid(c)).astype(BF16)
    part = jnp.dot(a, w_ref[...].astype(BF16), preferred_element_type=F32)

    @pl.when(k == 0)
    def _():
        o_ref[...] = part + b_ref[...]

    @pl.when(k > 0)
    def _():
        o_ref[...] += part


def _adaln(cond, w_ada, b_ada):
    rows, d = cond.shape
    n = w_ada.shape[1]
    return pl.pallas_call(
        _adaln_kernel,
        grid=(d // ADALN_TK,),
        in_specs=[pl.BlockSpec((rows, ADALN_TK), lambda k: (0, k)),
                  pl.BlockSpec((ADALN_TK, n), lambda k: (k, 0)),
                  pl.BlockSpec((1, n), lambda k: (0, 0))],
        out_specs=pl.BlockSpec((rows, n), lambda k: (0, 0)),
        out_shape=jax.ShapeDtypeStruct((rows, n), F32),
        compiler_params=_cparams(("arbitrary",)),
        name="adaln",
    )(cond, w_ada, b_ada)


def _modulated_norm(x, g, shift, scale):
    ms = jnp.mean(x * x, axis=-1, keepdims=True)
    xn = x * lax.rsqrt(ms + NORM_EPS) * g
    return xn * (1.0 + scale) + shift


def _mod_spec(k, ctx):
    if ctx:
        return pl.BlockSpec((1, 1, 1, D_MODEL), lambda b, *_: (8, k, 0, 0))
    return pl.BlockSpec((1, 1, 1, D_MODEL), lambda b, *_: (b, k, 0, 0))


FFN_TF = 256


def _ffn_kernel(*refs, with_attn):
    if with_attn:
        x_ref, agate_ref, oa_ref, ob_ref, wo_ref = refs[:5]
        shift_ref, scale_ref, gate_ref, g_ref, wgu_ref, wd_ref, o_ref, xm_ref, acc_ref, h_ref = refs[5:]
        half = oa_ref.shape[-1]
        attn = (jnp.dot(oa_ref[0], wo_ref[0:half, :], preferred_element_type=F32)
                + jnp.dot(ob_ref[0], wo_ref[half:, :], preferred_element_type=F32))
        h_ref[0] = x_ref[0] + agate_ref[0, 0] * attn
    else:
        h_ref, shift_ref, scale_ref, gate_ref, g_ref, wgu_ref, wd_ref, o_ref, xm_ref, acc_ref = refs
    f = wd_ref.shape[0]
    nj = f // FFN_TF
    xm_ref[...] = _modulated_norm(h_ref[0], g_ref[...], shift_ref[0, 0], scale_ref[0, 0]).astype(BF16)

    def act(j):
        xm = xm_ref[...]
        g = jnp.dot(xm, wgu_ref[:, j * FFN_TF:(j + 1) * FFN_TF], preferred_element_type=F32)
        u = jnp.dot(xm, wgu_ref[:, f + j * FFN_TF:f + (j + 1) * FFN_TF], preferred_element_type=F32)
        return (g * jax.nn.sigmoid(g) * u).astype(BF16)

    a = act(0)
    for j in range(nj):
        down = jnp.dot(a, wd_ref[j * FFN_TF:(j + 1) * FFN_TF, :], preferred_element_type=F32)
        if j + 1 < nj:
            a = act(j + 1)
        if j == 0:
            acc_ref[...] = down
        elif j + 1 < nj:
            acc_ref[...] += down
        else:
            o_ref[0] = h_ref[0] + (0.5 * gate_ref[0, 0]) * (acc_ref[...] + down)


def _ffn_weights(w_gu, w_down):
    return w_gu.astype(BF16), w_down.astype(BF16)


def _resident(shape):
    return pl.BlockSpec(shape, lambda *_: (0,) * len(shape), pipeline_mode=pl.Buffered(1))


def _ffn(h, mod4, k0, ctx, g, weights, *, tm, attn=None):
    bsz, t, d = h.shape
    wgu, wd = weights
    assert wd.shape[0] % FFN_TF == 0
    tile = pl.BlockSpec((1, tm, d), lambda b, i: (b, i, 0))
    in_specs, args = [tile], [h]
    scratch = [pltpu.VMEM((tm, d), BF16), pltpu.VMEM((tm, d), F32)]
    if attn is not None:
        k_gate, oa, ob, w_out = attn
        half = oa.shape[-1]
        in_specs += [_mod_spec(k_gate, ctx), pl.BlockSpec((1, tm, half), lambda b, i: (b, i, 0)),
                     pl.BlockSpec((1, tm, half), lambda b, i: (b, i, 0)), _resident(w_out.shape)]
        args += [mod4, oa, ob, w_out]
        scratch.append(pltpu.VMEM((1, tm, d), F32))
    in_specs += [_mod_spec(k0, ctx), _mod_spec(k0 + 1, ctx), _mod_spec(k0 + 2, ctx),
                 pl.BlockSpec((1, d), lambda b, i: (0, 0)), _resident(wgu.shape), _resident(wd.shape)]
    args += [mod4, mod4, mod4, g, wgu, wd]
    return pl.pallas_call(
        functools.partial(_ffn_kernel, with_attn=attn is not None),
        grid=(bsz, t // tm),
        in_specs=in_specs,
        out_specs=tile,
        out_shape=jax.ShapeDtypeStruct(h.shape, F32),
        scratch_shapes=scratch,
        compiler_params=_cparams(("parallel", "parallel")),
        name="ffn_ctx" if ctx else "ffn",
    )(*args)


def _head_rms_scale(y, bd_ref):
    y2 = y * y
    hi = y2.astype(BF16)
    lo = (y2 - hi.astype(F32)).astype(BF16)
    bd = bd_ref[...]
    w = bd.shape[0]
    ss = jnp.concatenate(
        [jnp.dot(hi[:, c:c + w], bd, preferred_element_type=F32)
         + jnp.dot(lo[:, c:c + w], bd, preferred_element_type=F32) for c in range(0, y.shape[1], w)],
        axis=-1)
    return lax.rsqrt(ss * (1.0 / HEAD_DIM) + NORM_EPS)


def _rope(z, cos, sin):
    lane = lax.broadcasted_iota(jnp.int32, z.shape, 1)
    up = pltpu.roll(z, LANES - 16, 1)
    dn = pltpu.roll(z, 16, 1)
    return z * cos + jnp.where((lane & 16) == 0, up, dn) * sin


def _in_proj_kernel(*refs, groups, n_gain):
    h_ref, shift_ref, scale_ref, g_ref, w_ref, bd_ref, cos_ref, sin_ref = refs[:8]
    gain_refs = refs[8:8 + n_gain]
    out_refs = refs[8 + n_gain:]
    xm = _modulated_norm(h_ref[0], g_ref[...], shift_ref[0, 0], scale_ref[0, 0]).astype(BF16)
    for (col, gain_idx, rope, qscale), o_ref in zip(groups, out_refs):
        y = jnp.dot(xm, w_ref[:, col * GROUP_W:(col + 1) * GROUP_W], preferred_element_type=F32)
        if gain_idx is not None:
            gain = gain_refs[gain_idx][...]
            if qscale != 1.0:
                gain = gain * qscale
            y = y * _head_rms_scale(y, bd_ref) * gain
        if rope:
            cos = cos_ref[...]
            sin = sin_ref[...]
            for c in range(GROUP_W // LANES):
                sl = slice(c * LANES, (c + 1) * LANES)
                o_ref[0, :, sl] = _rope(y[:, sl], cos, sin).astype(BF16)
        else:
            o_ref[0] = y.astype(BF16)


def _in_proj(h, mod4, ctx, g, w_in, bd, cos, sin, gains, groups, *, tm):
    bsz, t, d = h.shape
    n_out = len(groups)
    in_specs = [pl.BlockSpec((1, tm, d), lambda b, i: (b, i, 0)),
                _mod_spec(3, ctx), _mod_spec(4, ctx),
                pl.BlockSpec((1, d), lambda b, i: (0, 0)),
                pl.BlockSpec(w_in.shape, lambda b, i: (0, 0)),
                pl.BlockSpec(bd.shape, lambda b, i: (0, 0)),
                pl.BlockSpec((tm, LANES), lambda b, i: (i, 0)),
                pl.BlockSpec((tm, LANES), lambda b, i: (i, 0))]
    in_specs += [pl.BlockSpec((1, GROUP_W), lambda b, i: (0, 0)) for _ in gains]
    return pl.pallas_call(
        functools.partial(_in_proj_kernel, groups=groups, n_gain=len(gains)),
        grid=(bsz, t // tm),
        in_specs=in_specs,
        out_specs=[pl.BlockSpec((1, tm, GROUP_W), lambda b, i: (b, i, 0)) for _ in range(n_out)],
        out_shape=[jax.ShapeDtypeStruct((bsz, t, GROUP_W), BF16) for _ in range(n_out)],
        compiler_params=_cparams(("parallel", "parallel")),
        name="in_proj_ctx" if ctx else "in_proj",
    )(h, mod4, mod4, g, w_in, bd, cos, sin, *gains)


def _split_halves(q):
    lane = lax.broadcasted_iota(jnp.int32, q.shape, 1)
    qf = q.astype(F32)
    lo = jnp.where(lane < HEAD_DIM, qf, 0.0).astype(BF16)
    hi = jnp.where(lane >= HEAD_DIM, qf, 0.0).astype(BF16)
    return lo, hi


DIFF_TQ = 256
DIFF_TK = 256
SUBLANES = 8
V_ROWS = LANES + 16


def _diff_kernel(q_ref, kc_ref, kl_ref, vc_ref, vl_ref, lamv_ref, gcol_ref, o_ref,
                 kk_ref, vte_ref, qm_ref, s_ref, m_ref):
    n_ctx = kc_ref.shape[1]
    n_tick = kk_ref.shape[0] // DIFF_TK
    n_blk = q_ref.shape[1] // DIFF_TQ
    kk_ref[0:n_ctx] = kc_ref[0]
    kk_ref[n_ctx:] = kl_ref[0]
    one_row = lax.broadcasted_iota(jnp.int32, (V_ROWS - LANES, DIFF_TK), 0) == 0
    for kc in range(n_tick):
        lo = kc * DIFF_TK
        v = vc_ref[0, lo:lo + DIFF_TK] if lo < n_ctx else vl_ref[0, lo - n_ctx:lo - n_ctx + DIFF_TK]
        vte_ref[kc, 0:LANES, :] = v.astype(F32).T.astype(BF16)
        vte_ref[kc, LANES:, :] = jnp.where(one_row, 1.0, 0.0).astype(BF16)
    lv = lamv_ref[...]
    lam = (jnp.exp(jnp.sum(lv[0:1] * lv[1:2], axis=-1, keepdims=True))
           - jnp.exp(jnp.sum(lv[2:3] * lv[3:4], axis=-1, keepdims=True)) + LAM_INIT)

    def rows(i):
        return pl.ds(pl.multiple_of(i * DIFF_TQ, DIFF_TQ), DIFF_TQ)

    def keys(kc):
        return slice(kc * DIFF_TK, (kc + 1) * DIFF_TK)

    def prep(i, slot):
        lo, hi = _split_halves(q_ref[0, rows(i), :])
        qm_ref[slot, 0] = lo
        qm_ref[slot, 1] = hi
        m_ref[slot] = jnp.full(m_ref.shape[1:], -jnp.inf, F32)

    def score_tick(slot, kc):
        k = kk_ref[keys(kc), :]
        for c in range(2):
            s = lax.dot_general(k, qm_ref[slot, c], _NT, preferred_element_type=F32)
            s_ref[slot, c, keys(kc), :] = s
            part = jnp.max(s.reshape(DIFF_TK // SUBLANES, SUBLANES, DIFF_TQ), axis=0)
            m_ref[slot, c] = jnp.maximum(m_ref[slot, c], part)

    def final_max(slot):
        return [jnp.broadcast_to(jnp.max(m_ref[slot, c], axis=0, keepdims=True), (SUBLANES, DIFF_TQ))
                for c in range(2)]

    def pv_tick(slot, kc, mx, acc):
        for c in range(2):
            s = s_ref[slot, c, keys(kc), :].reshape(DIFF_TK // SUBLANES, SUBLANES, DIFF_TQ)
            p = jnp.exp2(s - mx[c]).reshape(DIFF_TK, DIFF_TQ).astype(BF16)
            d = jnp.dot(vte_ref[kc], p, preferred_element_type=F32)
            acc[c] = d if acc[c] is None else acc[c] + d

    def finish(i, acc):
        l0 = acc[0][LANES:LANES + 1, :]
        l1 = acc[1][LANES:LANES + 1, :]
        ot = acc[0][0:LANES, :] * (1.0 / l0) - acc[1][0:LANES, :] * (lam / l1)
        ms = jnp.mean(ot * ot, axis=0, keepdims=True)
        on = ot * lax.rsqrt(ms + NORM_EPS) * (gcol_ref[...] * (1.0 - LAM_INIT))
        o_ref[0, rows(i), :] = on.T.astype(BF16)

    def block(i, slot, last=False):
        mx = final_max(slot)
        if not last:
            prep(i + 1, 1 - slot)
        acc = [None, None]
        for kc in range(n_tick):
            pv_tick(slot, kc, mx, acc)
            if not last:
                score_tick(1 - slot, kc)
        finish(i, acc)

    prep(0, 0)
    for kc in range(n_tick):
        score_tick(0, kc)

    def pair(t, carry):
        block(2 * t, 0)
        block(2 * t + 1, 1)
        return carry

    lax.fori_loop(0, n_blk // 2 - 1, pair, 0)
    block(n_blk - 2, 0)
    block(n_blk - 1, 1, last=True)


def _diff_attn(aq, ak, av, akc, avc, lamv, gout):
    bsz, s, _ = aq.shape
    c = akc.shape[1]
    assert s % (2 * DIFF_TQ) == 0 and c % DIFF_TK == 0 and s % DIFF_TK == 0
    return pl.pallas_call(
        _diff_kernel,
        grid=(bsz, DIFF_HEADS),
        in_specs=[pl.BlockSpec((1, s, LANES), lambda b, h: (b, 0, h)),
                  pl.BlockSpec((1, c, LANES), lambda b, h: (b, 0, h)),
                  pl.BlockSpec((1, s, LANES), lambda b, h: (b, 0, h)),
                  pl.BlockSpec((1, c, LANES), lambda b, h: (b, 0, h)),
                  pl.BlockSpec((1, s, LANES), lambda b, h: (b, 0, h)),
                  pl.BlockSpec(lamv.shape, lambda b, h: (0, 0)),
                  pl.BlockSpec((LANES, 1), lambda b, h: (0, 0))],
        out_specs=pl.BlockSpec((1, s, LANES), lambda b, h: (b, 0, h)),
        out_shape=jax.ShapeDtypeStruct(aq.shape, BF16),
        scratch_shapes=[pltpu.VMEM((c + s, LANES), BF16),
                        pltpu.VMEM(((c + s) // DIFF_TK, V_ROWS, DIFF_TK), BF16),
                        pltpu.VMEM((2, 2, DIFF_TQ, LANES), BF16),
                        pltpu.VMEM((2, 2, c + s, DIFF_TQ), F32),
                        pltpu.VMEM((2, 2, SUBLANES, DIFF_TQ), F32)],
        compiler_params=_cparams(("parallel", "parallel")),
        name="diff_attn",
    )(aq, akc, ak, avc, av, lamv, gout.reshape(LANES, 1))


NA_QROWS = 4
NA_KROWS = 12
NA_TQ = NA_QROWS * GRID_W
NA_TK = 4 * GRID_W
NA_WIN_TICKS = NA_KROWS * GRID_W // NA_TK
NA_V_ROWS = HEAD_DIM + 16


def _na_variants(rows):
    n_rb = rows // NA_QROWS
    out = []
    for rb in (0, 1, n_rb - 1):
        ws = min(max(rb * NA_QROWS - NA_KH // 2, 0), rows - NA_KROWS)
        out.append((rb * NA_QROWS, ws))
    return out


def _na_build_bias(rpb_ref, bias_ref, rows):
    shape = (GRID_W, LANES)
    kc = lax.broadcasted_iota(jnp.int32, shape, 0)
    lane = lax.broadcasted_iota(jnp.int32, shape, 1)
    qc = lane & (GRID_W - 1)
    cs = jnp.clip(qc - NA_KW // 2, 0, GRID_W - NA_KW)
    col_ok = (kc >= cs) & (kc < cs + NA_KW)
    side_ok = {(True, True): col_ok, (True, False): col_ok & (lane < GRID_W),
               (False, True): col_ok & (lane >= GRID_W)}
    neg = jnp.full(shape, NEG_INF, F32)
    for hh in range(2):
        toeplitz = {}
        for v, (r0, ws) in enumerate(_na_variants(rows)):
            for i in range(NA_KROWS):
                kr = ws + i
                for jp in range(NA_QROWS // 2):
                    ok = []
                    for r in (r0 + 2 * jp, r0 + 2 * jp + 1):
                        rs = min(max(r - NA_KH // 2, 0), rows - NA_KH)
                        ok.append(rs <= kr < rs + NA_KH)
                    tile = neg
                    if any(ok):
                        e = kr - (r0 + 2 * jp) + NA_KH - 1
                        if e not in toeplitz:
                            vec = jnp.broadcast_to(rpb_ref[0, hh, e:e + 1, :] * LOG2E, shape)
                            toeplitz[e] = pltpu.roll(vec, LANES - (NA_KW - 1), 1, stride=1, stride_axis=0)
                        tile = jnp.where(side_ok[tuple(ok)], toeplitz[e], neg)
                    bias_ref[hh, v, i * GRID_W:(i + 1) * GRID_W, jp * LANES:(jp + 1) * LANES] = tile


def _na_kernel(q_ref, kc_ref, k_ref, vc_ref, v_ref, rpb_ref, o_ref,
               bias_ref, vte_ref, qm_ref, s_ref, m_ref, *, rows):
    n_blk = rows // NA_QROWS
    n_lat = v_ref.shape[1] // NA_TK
    assert kc_ref.shape[1] == NA_TK
    n_tick = NA_WIN_TICKS + 1

    @pl.when(pl.program_id(1) == 0)
    def _():
        _na_build_bias(rpb_ref, bias_ref, rows)

    one_row = lax.broadcasted_iota(jnp.int32, (NA_V_ROWS - HEAD_DIM, NA_TK), 0) == 0
    for ch in range(n_lat + 1):
        v = v_ref[0, ch * NA_TK:(ch + 1) * NA_TK] if ch < n_lat else vc_ref[0]
        vt = v.astype(F32).T
        for hh in range(2):
            vte_ref[hh, ch, 0:HEAD_DIM, :] = vt[hh * HEAD_DIM:(hh + 1) * HEAD_DIM].astype(BF16)
            vte_ref[hh, ch, HEAD_DIM:, :] = jnp.where(one_row, 1.0, 0.0).astype(BF16)

    def rows_of(i):
        return pl.ds(pl.multiple_of(i * NA_TQ, NA_TQ), NA_TQ)

    def window(i):
        return (jnp.clip(i - 1, 0, n_lat - NA_WIN_TICKS),
                jnp.where(i == 0, 0, jnp.where(i == n_blk - 1, 2, 1)))

    def prep(i, slot):
        lo, hi = _split_halves(q_ref[0, rows_of(i), :])
        qm_ref[slot, 0] = lo
        qm_ref[slot, 1] = hi
        m_ref[slot] = jnp.full(m_ref.shape[1:], -jnp.inf, F32)

    def score_tick(slot, i, t):
        ch0, variant = window(i)
        if t < NA_WIN_TICKS:
            k = k_ref[0, pl.ds(pl.multiple_of((ch0 + t) * NA_TK, NA_TK), NA_TK), :]
        else:
            k = kc_ref[0]
        for hh in range(2):
            s = lax.dot_general(k, qm_ref[slot, hh], _NT, preferred_element_type=F32)
            if t < NA_WIN_TICKS:
                s = s + bias_ref[hh, variant, t * NA_TK:(t + 1) * NA_TK, :]
            s_ref[slot, hh, t * NA_TK:(t + 1) * NA_TK, :] = s
            part = jnp.max(s.reshape(NA_TK // SUBLANES, SUBLANES, NA_TQ), axis=0)
            m_ref[slot, hh] = jnp.maximum(m_ref[slot, hh], part)

    def final_max(slot):
        return [jnp.broadcast_to(jnp.max(m_ref[slot, hh], axis=0, keepdims=True), (SUBLANES, NA_TQ))
                for hh in range(2)]

    def pv_tick(slot, i, t, mx, acc):
        ch = window(i)[0] + t if t < NA_WIN_TICKS else n_lat
        for hh in range(2):
            s = s_ref[slot, hh, t * NA_TK:(t + 1) * NA_TK, :].reshape(NA_TK // SUBLANES, SUBLANES, NA_TQ)
            p = jnp.exp2(s - mx[hh]).reshape(NA_TK, NA_TQ).astype(BF16)
            d = jnp.dot(vte_ref[hh, ch], p, preferred_element_type=F32)
            acc[hh] = d if acc[hh] is None else acc[hh] + d

    def finish(i, acc):
        ot = jnp.concatenate(
            [acc[hh][0:HEAD_DIM, :] * (1.0 / acc[hh][HEAD_DIM:HEAD_DIM + 1, :]) for hh in range(2)],
            axis=0)
        o_ref[0, rows_of(i), :] = ot.T.astype(BF16)

    def block(i, slot, last=False):
        mx = final_max(slot)
        if not last:
            prep(i + 1, 1 - slot)
        acc = [None, None]
        for t in range(n_tick):
            pv_tick(slot, i, t, mx, acc)
            if not last:
                score_tick(1 - slot, i + 1, t)
        finish(i, acc)

    prep(0, 0)
    for t in range(n_tick):
        score_tick(0, 0, t)

    def pair(u, carry):
        block(2 * u, 0)
        block(2 * u + 1, 1)
        return carry

    lax.fori_loop(0, n_blk // 2 - 1, pair, 0)
    block(n_blk - 2, 0)
    block(n_blk - 1, 1, last=True)


def _na_attn(nq, nk, nv, nkc, nvc, rpb):
    bsz, s, _ = nq.shape
    c = nkc.shape[1]
    rows = s // GRID_W
    n_blk = rows // NA_QROWS
    assert n_blk >= 4 and n_blk % 2 == 0 and NA_QROWS * GRID_W == NA_TK and c == NA_TK
    n_pairs = NA_HEADS // 2
    n_lat = s // NA_TK
    n_r, n_c = rpb.shape[1:]
    padded = jnp.pad(rpb[:, :, ::-1], ((0, 0), (1, 1), (0, GRID_W - n_c)))
    rpb2 = jnp.concatenate([padded[:, 1:], padded[:, :-1]], axis=-1).reshape(n_pairs, 2, n_r + 1, LANES)
    return pl.pallas_call(
        functools.partial(_na_kernel, rows=rows),
        grid=(n_pairs, bsz),
        in_specs=[pl.BlockSpec((1, s, LANES), lambda p, b: (b, 0, p)),
                  pl.BlockSpec((1, c, LANES), lambda p, b: (b, 0, p)),
                  pl.BlockSpec((1, s, LANES), lambda p, b: (b, 0, p)),
                  pl.BlockSpec((1, c, LANES), lambda p, b: (b, 0, p)),
                  pl.BlockSpec((1, s, LANES), lambda p, b: (b, 0, p)),
                  pl.BlockSpec((1,) + rpb2.shape[1:], lambda p, b: (p, 0, 0, 0))],
        out_specs=pl.BlockSpec((1, s, LANES), lambda p, b: (b, 0, p)),
        out_shape=jax.ShapeDtypeStruct(nq.shape, BF16),
        scratch_shapes=[pltpu.VMEM((2, 3, NA_KROWS * GRID_W, NA_TQ), F32),
                        pltpu.VMEM((2, n_lat + 1, NA_V_ROWS, NA_TK), BF16),
                        pltpu.VMEM((2, 2, NA_TQ, LANES), BF16),
                        pltpu.VMEM((2, 2, (NA_WIN_TICKS + 1) * NA_TK, NA_TQ), F32),
                        pltpu.VMEM((2, 2, SUBLANES, NA_TQ), F32)],
        compiler_params=_cparams(("arbitrary", "arbitrary")),
        name="na_attn",
    )(nq, nkc, nk, nvc, nv, rpb2)


def _rope_tables(n_tokens):
    t = jnp.arange(n_tokens, dtype=jnp.int32)
    row = (t // GRID_W).astype(F32)
    col = (t % GRID_W).astype(F32)
    n_freq = HEAD_DIM // 4
    inv_freq = ROPE_BASE ** (-jnp.arange(n_freq, dtype=F32) / n_freq)
    ar = row[:, None] * inv_freq
    ac = col[:, None] * inv_freq
    cos = jnp.concatenate([jnp.cos(ar), jnp.cos(ar), jnp.cos(ac), jnp.cos(ac)], axis=-1)
    sin = jnp.concatenate([-jnp.sin(ar), jnp.sin(ar), -jnp.sin(ac), jnp.sin(ac)], axis=-1)
    reps = LANES // HEAD_DIM
    return jnp.tile(cos, (1, reps)), jnp.tile(sin, (1, reps))


def kernel(x, c, ctx, c_ctx, w_ada, b_ada, norm1, norm2, norm3, ffn1_w_gu, ffn1_w_down, w_in,
           diff_q_norm, diff_k_norm, lam_q1, lam_k1, lam_q2, lam_k2, diff_out_norm,
           na_q_norm, na_k_norm, na_rpb, w_out, ffn2_w_gu, ffn2_w_down):
    bsz, s, d = x.shape
    assert w_ada.shape[0] == 1 and d == D_MODEL and s % (NA_QROWS * GRID_W) == 0
    rows = s // GRID_W

    cond = jnp.zeros((16, d), F32).at[:bsz].set(c).at[bsz].set(c_ctx)
    mod4 = _adaln(cond, w_ada[0], b_ada).reshape(16, N_MOD, 1, d)

    ffn1_w = _ffn_weights(ffn1_w_gu[0], ffn1_w_down[0])
    h = _ffn(x, mod4, 0, False, norm1, ffn1_w, tm=1024)
    n_ctx = ctx.shape[1]
    hc = _ffn(ctx.reshape(1, bsz * n_ctx, d), mod4, 0, True, norm1, ffn1_w, tm=1024)

    w_in_b = w_in[0].astype(BF16)
    seg = np.arange(MXU_W) // HEAD_DIM
    bd = jnp.asarray(seg[:, None] == seg[None, :], BF16)
    cos, sin = _rope_tables(s)
    reps = GROUP_W // HEAD_DIM
    gains = [jnp.tile(g, (1, reps)) for g in (diff_q_norm, diff_k_norm, na_q_norm, na_k_norm)]
    qs = HEAD_DIM ** -0.5
    lat_groups = ((0, 0, True, qs * LOG2E), (1, 1, True, 1.0), (2, None, False, 1.0),
                  (3, 2, False, qs * LOG2E), (4, 3, False, 1.0), (5, None, False, 1.0))
    aq, ak, av, nq, nk, nv = _in_proj(h, mod4, False, norm2, w_in_b, bd, cos, sin, gains,
                                      lat_groups, tm=512)
    ctx_groups = ((1, 1, False, 1.0), (2, None, False, 1.0), (4, 3, False, 1.0), (5, None, False, 1.0))
    ctx_kv = _in_proj(hc, mod4, True, norm2, w_in_b, bd, cos, sin, gains, ctx_groups, tm=512)
    akc, avc, nkc, nvc = (a.reshape(bsz, n_ctx, GROUP_W) for a in ctx_kv)

    lamv = jnp.concatenate([lam_q1, lam_k1, lam_q2, lam_k2], axis=0)
    oa = _diff_attn(aq, ak, av, akc, avc, lamv, diff_out_norm)

    ob = _na_attn(nq, nk, nv, nkc, nvc, na_rpb[0])

    return _ffn(h, mod4, 6, False, norm3, _ffn_weights(ffn2_w_gu[0], ffn2_w_down[0]), tm=1024,
                attn=(5, oa, ob, w_out[0].astype(BF16)))
```

```python
import functools
import math

import numpy as np
import jax
import jax.numpy as jnp
from jax import lax
from jax.experimental import pallas as pl
from jax.experimental.pallas import tpu as pltpu

D_MODEL = 1024
GRID_W = 64
HEAD_DIM = 64
DIFF_HEADS = 4
NA_HEADS = 8
GROUP_W = 512
D_FF = 2816
N_MOD = 9
NA_KH = 8
NA_KW = 16
ROPE_BASE = 10000.0
NORM_EPS = 1e-6
NEG_INF = -1e30
LAM_INIT = 0.8 - 0.6 * math.exp(-0.3 * 0)
LOG2E = math.log2(math.e)

LANES = 128
MXU_W = 256
VMEM_LIMIT = 56 * 1024 * 1024

F32 = jnp.float32
BF16 = jnp.bfloat16


def _cparams(sem):
    return pltpu.CompilerParams(dimension_semantics=sem, vmem_limit_bytes=VMEM_LIMIT)


ADALN_TK = 128


def _adaln_kernel(cond_ref, w_ref, b_ref, o_ref):
    k = pl.program_id(0)
    c = cond_ref[...]
    a = (c * jax.nn.sigmoid(c)).astype(BF16)
    part = jnp.dot(a, w_ref[...].astype(BF16), preferred_element_type=F32)

    @pl.when(k == 0)
    def _():
        o_ref[...] = part + b_ref[...]

    @pl.when(k > 0)
    def _():
        o_ref[...] += part


def _adaln(cond, w_ada, b_ada):
    rows, d = cond.shape
    n = w_ada.shape[1]
    return pl.pallas_call(
        _adaln_kernel,
        grid=(d // ADALN_TK,),
        in_specs=[pl.BlockSpec((rows, ADALN_TK), lambda k: (0, k)),
                  pl.BlockSpec((ADALN_TK, n), lambda k: (k, 0)),
                  pl.BlockSpec((1, n), lambda k: (0, 0))],
        out_specs=pl.BlockSpec((rows, n), lambda k: (0, 0)),
        out_shape=jax.ShapeDtypeStruct((rows, n), F32),
        compiler_params=_cparams(("arbitrary",)),
        name="adaln",
    )(cond, w_ada, b_ada)


def _modulated_norm(x, g, shift, scale):
    ms = jnp.mean(x * x, axis=-1, keepdims=True)
    xn = x * lax.rsqrt(ms + NORM_EPS) * g
    return xn * (1.0 + scale) + shift


def _mod_spec(k, ctx):
    if ctx:
        return pl.BlockSpec((1, 1, 1, D_MODEL), lambda b, *_: (8, k, 0, 0))
    return pl.BlockSpec((1, 1, 1, D_MODEL), lambda b, *_: (b, k, 0, 0))


FFN_TF = 256


def _ffn_kernel(*refs, with_attn):
    if with_attn:
        x_ref, agate_ref, oa_ref, ob_ref, wo_ref = refs[:5]
        shift_ref, scale_ref, gate_ref, g_ref, wgu_ref, wd_ref, o_ref, xm_ref, acc_ref, h_ref = refs[5:]
        half = oa_ref.shape[-1]
        attn = (jnp.dot(oa_ref[0], wo_ref[0:half, :], preferred_element_type=F32)
                + jnp.dot(ob_ref[0], wo_ref[half:, :], preferred_element_type=F32))
        h_ref[0] = x_ref[0] + agate_ref[0, 0] * attn
    else:
        h_ref, shift_ref, scale_ref, gate_ref, g_ref, wgu_ref, wd_ref, o_ref, xm_ref, acc_ref = refs
    f = wd_ref.shape[0]
    nj = f // FFN_TF
    xm_ref[...] = _modulated_norm(h_ref[0], g_ref[...], shift_ref[0, 0], scale_ref[0, 0]).astype(BF16)

    def act(j):
        xm = xm_ref[...]
        g = jnp.dot(xm, wgu_ref[:, j * FFN_TF:(j + 1) * FFN_TF], preferred_element_type=F32)
        u = jnp.dot(xm, wgu_ref[:, f + j * FFN_TF:f + (j + 1) * FFN_TF], preferred_element_type=F32)
        return (g * jax.nn.sigmoid(g) * u).astype(BF16)

    a = act(0)
    for j in range(nj):
        down = jnp.dot(a, wd_ref[j * FFN_TF:(j + 1) * FFN_TF, :], preferred_element_type=F32)
        if j + 1 < nj:
            a = act(j + 1)
        if j == 0:
            acc_ref[...] = down
        elif j + 1 < nj:
            acc_ref[...] += down
        else:
            o_ref[0] = h_ref[0] + (0.5 * gate_ref[0, 0]) * (acc_ref[...] + down)


def _ffn_weights(w_gu, w_down):
    return w_gu.astype(BF16), w_down.astype(BF16)


def _resident(shape):
    return pl.BlockSpec(shape, lambda *_: (0,) * len(shape), pipeline_mode=pl.Buffered(1))


def _ffn(h, mod4, k0, ctx, g, weights, *, tm, attn=None):
    bsz, t, d = h.shape
    wgu, wd = weights
    assert wd.shape[0] % FFN_TF == 0
    tile = pl.BlockSpec((1, tm, d), lambda b, i: (b, i, 0))
    in_specs, args = [tile], [h]
    scratch = [pltpu.VMEM((tm, d), BF16), pltpu.VMEM((tm, d), F32)]
    if attn is not None:
        k_gate, oa, ob, w_out = attn
        half = oa.shape[-1]
        in_specs += [_mod_spec(k_gate, ctx), pl.BlockSpec((1, tm, half), lambda b, i: (b, i, 0)),
                     pl.BlockSpec((1, tm, half), lambda b, i: (b, i, 0)), _resident(w_out.shape)]
        args += [mod4, oa, ob, w_out]
        scratch.append(pltpu.VMEM((1, tm, d), F32))
    in_specs += [_mod_spec(k0, ctx), _mod_spec(k0 + 1, ctx), _mod_spec(k0 + 2, ctx),
                 pl.BlockSpec((1, d), lambda b, i: (0, 0)), _resident(wgu.shape), _resident(wd.shape)]
    args += [mod4, mod4, mod4, g, wgu, wd]
    return pl.pallas_call(
        functools.partial(_ffn_kernel, with_attn=attn is not None),
        grid=(bsz, t // tm),
        in_specs=in_specs,
        out_specs=tile,
        out_shape=jax.ShapeDtypeStruct(h.shape, F32),
        scratch_shapes=scratch,
        compiler_params=_cparams(("parallel", "parallel")),
        name="ffn_ctx" if ctx else "ffn",
    )(*args)


def _head_rms_scale(y, bd_ref):
    y2 = y * y
    hi = y2.astype(BF16)
    lo = (y2 - hi.astype(F32)).astype(BF16)
    bd = bd_ref[...]
    w = bd.shape[0]
    ss = jnp.concatenate(
        [jnp.dot(hi[:, c:c + w], bd, preferred_element_type=F32)
         + jnp.dot(lo[:, c:c + w], bd, preferred_element_type=F32) for c in range(0, y.shape[1], w)],
        axis=-1)
    return lax.rsqrt(ss * (1.0 / HEAD_DIM) + NORM_EPS)


def _rope(z, cos, sin):
    lane = lax.broadcasted_iota(jnp.int32, z.shape, 1)
    up = pltpu.roll(z, LANES - 16, 1)
    dn = pltpu.roll(z, 16, 1)
    return z * cos + jnp.where((lane & 16) == 0, up, dn) * sin


def _in_proj_kernel(*refs, groups, n_gain):
    h_ref, shift_ref, scale_ref, g_ref, w_ref, bd_ref, cos_ref, sin_ref = refs[:8]
    gain_refs = refs[8:8 + n_gain]
    out_refs = refs[8 + n_gain:]
    xm = _modulated_norm(h_ref[0], g_ref[...], shift_ref[0, 0], scale_ref[0, 0]).astype(BF16)
    for (col, gain_idx, rope, qscale), o_ref in zip(groups, out_refs):
        y = jnp.dot(xm, w_ref[:, col * GROUP_W:(col + 1) * GROUP_W], preferred_element_type=F32)
        if gain_idx is not None:
            gain = gain_refs[gain_idx][...]
            if qscale != 1.0:
                gain = gain * qscale
            y = y * _head_rms_scale(y, bd_ref) * gain
        if rope:
            cos = cos_ref[...]
            sin = sin_ref[...]
            for c in range(GROUP_W // LANES):
                sl = slice(c * LANES, (c + 1) * LANES)
                o_ref[0, :, sl] = _rope(y[:, sl], cos, sin).astype(BF16)
        else:
            o_ref[0] = y.astype(BF16)


def _in_proj(h, mod4, ctx, g, w_in, bd, cos, sin, gains, groups, *, tm):
    bsz, t, d = h.shape
    n_out = len(groups)
    in_specs = [pl.BlockSpec((1, tm, d), lambda b, i: (b, i, 0)),
                _mod_spec(3, ctx), _mod_spec(4, ctx),
                pl.BlockSpec((1, d), lambda b, i: (0, 0)),
                pl.BlockSpec(w_in.shape, lambda b, i: (0, 0)),
                pl.BlockSpec(bd.shape, lambda b, i: (0, 0)),
                pl.BlockSpec((tm, LANES), lambda b, i: (i, 0)),
                pl.BlockSpec((tm, LANES), lambda b, i: (i, 0))]
    in_specs += [pl.BlockSpec((1, GROUP_W), lambda b, i: (0, 0)) for _ in gains]
    return pl.pallas_call(
        functools.partial(_in_proj_kernel, groups=groups, n_gain=len(gains)),
        grid=(bsz, t // tm),
        in_specs=in_specs,
        out_specs=[pl.BlockSpec((1, tm, GROUP_W), lambda b, i: (b, i, 0)) for _ in range(n_out)],
        out_shape=[jax.ShapeDtypeStruct((bsz, t, GROUP_W), BF16) for _ in range(n_out)],
        compiler_params=_cparams(("parallel", "parallel")),
        name="in_proj_ctx" if ctx else "in_proj",
    )(h, mod4, mod4, g, w_in, bd, cos, sin, *gains)


def _split_halves(q):
    qt = q.astype(F32).T
    row = lax.broadcasted_iota(jnp.int32, qt.shape, 0)
    lo = jnp.where(row < HEAD_DIM, qt, 0.0).astype(BF16)
    hi = jnp.where(row >= HEAD_DIM, qt, 0.0).astype(BF16)
    return lo, hi


DIFF_TQ = 256
DIFF_TK = 256
SUBLANES = 8
V_ROWS = LANES + 16


def _diff_kernel(q_ref, kc_ref, kl_ref, vc_ref, vl_ref, lamv_ref, gcol_ref, o_ref,
                 kk_ref, vte_ref, qm_ref, s_ref, m_ref):
    n_ctx = kc_ref.shape[1]
    n_tick = kk_ref.shape[0] // DIFF_TK
    n_blk = q_ref.shape[1] // DIFF_TQ
    kk_ref[0:n_ctx] = kc_ref[0]
    kk_ref[n_ctx:] = kl_ref[0]
    one_row = lax.broadcasted_iota(jnp.int32, (V_ROWS - LANES, DIFF_TK), 0) == 0
    for kc in range(n_tick):
        lo = kc * DIFF_TK
        v = vc_ref[0, lo:lo + DIFF_TK] if lo < n_ctx else vl_ref[0, lo - n_ctx:lo - n_ctx + DIFF_TK]
        vte_ref[kc, 0:LANES, :] = v.astype(F32).T.astype(BF16)
        vte_ref[kc, LANES:, :] = jnp.where(one_row, 1.0, 0.0).astype(BF16)
    lv = lamv_ref[...]
    lam = (jnp.exp(jnp.sum(lv[0:1] * lv[1:2], axis=-1, keepdims=True))
           - jnp.exp(jnp.sum(lv[2:3] * lv[3:4], axis=-1, keepdims=True)) + LAM_INIT)

    def rows(i):
        return pl.ds(pl.multiple_of(i * DIFF_TQ, DIFF_TQ), DIFF_TQ)

    def keys(kc):
        return slice(kc * DIFF_TK, (kc + 1) * DIFF_TK)

    def prep(i, slot):
        lo, hi = _split_halves(q_ref[0, rows(i), :])
        qm_ref[slot, 0] = lo
        qm_ref[slot, 1] = hi
        m_ref[slot] = jnp.full(m_ref.shape[1:], -jnp.inf, F32)

    def score_tick(slot, kc):
        k = kk_ref[keys(kc), :]
        for c in range(2):
            s = jnp.dot(k, qm_ref[slot, c], preferred_element_type=F32)
            s_ref[slot, c, keys(kc), :] = s
            part = jnp.max(s.reshape(DIFF_TK // SUBLANES, SUBLANES, DIFF_TQ), axis=0)
            m_ref[slot, c] = jnp.maximum(m_ref[slot, c], part)

    def final_max(slot):
        return [jnp.broadcast_to(jnp.max(m_ref[slot, c], axis=0, keepdims=True), (SUBLANES, DIFF_TQ))
                for c in range(2)]

    def pv_tick(slot, kc, mx, acc):
        for c in range(2):
            s = s_ref[slot, c, keys(kc), :].reshape(DIFF_TK // SUBLANES, SUBLANES, DIFF_TQ)
            p = jnp.exp2(s - mx[c]).reshape(DIFF_TK, DIFF_TQ).astype(BF16)
            d = jnp.dot(vte_ref[kc], p, preferred_element_type=F32)
            acc[c] = d if acc[c] is None else acc[c] + d

    def finish(i, acc):
        l0 = acc[0][LANES:LANES + 1, :]
        l1 = acc[1][LANES:LANES + 1, :]
        ot = acc[0][0:LANES, :] * (1.0 / l0) - acc[1][0:LANES, :] * (lam / l1)
        ms = jnp.mean(ot * ot, axis=0, keepdims=True)
        on = ot * lax.rsqrt(ms + NORM_EPS) * (gcol_ref[...] * (1.0 - LAM_INIT))
        o_ref[0, rows(i), :] = on.T.astype(BF16)

    def block(i, slot, last=False):
        mx = final_max(slot)
        if not last:
            prep(i + 1, 1 - slot)
        acc = [None, None]
        for kc in range(n_tick):
            pv_tick(slot, kc, mx, acc)
            if not last:
                score_tick(1 - slot, kc)
        finish(i, acc)

    prep(0, 0)
    for kc in range(n_tick):
        score_tick(0, kc)

    def pair(t, carry):
        block(2 * t, 0)
        block(2 * t + 1, 1)
        return carry

    lax.fori_loop(0, n_blk // 2 - 1, pair, 0)
    block(n_blk - 2, 0)
    block(n_blk - 1, 1, last=True)


def _diff_attn(aq, ak, av, akc, avc, lamv, gout):
    bsz, s, _ = aq.shape
    c = akc.shape[1]
    assert s % (2 * DIFF_TQ) == 0 and c % DIFF_TK == 0 and s % DIFF_TK == 0
    return pl.pallas_call(
        _diff_kernel,
        grid=(bsz, DIFF_HEADS),
        in_specs=[pl.BlockSpec((1, s, LANES), lambda b, h: (b, 0, h)),
                  pl.BlockSpec((1, c, LANES), lambda b, h: (b, 0, h)),
                  pl.BlockSpec((1, s, LANES), lambda b, h: (b, 0, h)),
                  pl.BlockSpec((1, c, LANES), lambda b, h: (b, 0, h)),
                  pl.BlockSpec((1, s, LANES), lambda b, h: (b, 0, h)),
                  pl.BlockSpec(lamv.shape, lambda b, h: (0, 0)),
                  pl.BlockSpec((LANES, 1), lambda b, h: (0, 0))],
        out_specs=pl.BlockSpec((1, s, LANES), lambda b, h: (b, 0, h)),
        out_shape=jax.ShapeDtypeStruct(aq.shape, BF16),
        scratch_shapes=[pltpu.VMEM((c + s, LANES), BF16),
                        pltpu.VMEM(((c + s) // DIFF_TK, V_ROWS, DIFF_TK), BF16),
                        pltpu.VMEM((2, 2, LANES, DIFF_TQ), BF16),
                        pltpu.VMEM((2, 2, c + s, DIFF_TQ), F32),
                        pltpu.VMEM((2, 2, SUBLANES, DIFF_TQ), F32)],
        compiler_params=_cparams(("parallel", "parallel")),
        name="diff_attn",
    )(aq, akc, ak, avc, av, lamv, gout.reshape(LANES, 1))


NA_QROWS = 4
NA_KROWS = 12
NA_TQ = NA_QROWS * GRID_W
NA_TK = 4 * GRID_W
NA_WIN_TICKS = NA_KROWS * GRID_W // NA_TK
NA_V_ROWS = HEAD_DIM + 16


def _na_variants(rows):
    n_rb = rows // NA_QROWS
    out = []
    for rb in (0, 1, n_rb - 1):
        ws = min(max(rb * NA_QROWS - NA_KH // 2, 0), rows - NA_KROWS)
        out.append((rb * NA_QROWS, ws))
    return out


def _na_build_bias(rpb_ref, bias_ref, rows):
    shape = (GRID_W, LANES)
    kc = lax.broadcasted_iota(jnp.int32, shape, 0)
    lane = lax.broadcasted_iota(jnp.int32, shape, 1)
    qc = lane & (GRID_W - 1)
    cs = jnp.clip(qc - NA_KW // 2, 0, GRID_W - NA_KW)
    col_ok = (kc >= cs) & (kc < cs + NA_KW)
    side_ok = {(True, True): col_ok, (True, False): col_ok & (lane < GRID_W),
               (False, True): col_ok & (lane >= GRID_W)}
    neg = jnp.full(shape, NEG_INF, F32)
    for hh in range(2):
        toeplitz = {}
        for v, (r0, ws) in enumerate(_na_variants(rows)):
            for i in range(NA_KROWS):
                kr = ws + i
                for jp in range(NA_QROWS // 2):
                    ok = []
                    for r in (r0 + 2 * jp, r0 + 2 * jp + 1):
                        rs = min(max(r - NA_KH // 2, 0), rows - NA_KH)
                        ok.append(rs <= kr < rs + NA_KH)
                    tile = neg
                    if any(ok):
                        e = kr - (r0 + 2 * jp) + NA_KH - 1
                        if e not in toeplitz:
                            vec = jnp.broadcast_to(rpb_ref[0, hh, e:e + 1, :] * LOG2E, shape)
                            toeplitz[e] = pltpu.roll(vec, LANES - (NA_KW - 1), 1, stride=1, stride_axis=0)
                        tile = jnp.where(side_ok[tuple(ok)], toeplitz[e], neg)
                    bias_ref[hh, v, i * GRID_W:(i + 1) * GRID_W, jp * LANES:(jp + 1) * LANES] = tile


def _na_kernel(q_ref, kc_ref, k_ref, vc_ref, v_ref, rpb_ref, o_ref,
               bias_ref, vte_ref, qm_ref, s_ref, m_ref, *, rows):
    n_blk = rows // NA_QROWS
    n_lat = v_ref.shape[1] // NA_TK
    assert kc_ref.shape[1] == NA_TK
    n_tick = NA_WIN_TICKS + 1

    @pl.when(pl.program_id(1) == 0)
    def _():
        _na_build_bias(rpb_ref, bias_ref, rows)

    one_row = lax.broadcasted_iota(jnp.int32, (NA_V_ROWS - HEAD_DIM, NA_TK), 0) == 0
    for ch in range(n_lat + 1):
        v = v_ref[0, ch * NA_TK:(ch + 1) * NA_TK] if ch < n_lat else vc_ref[0]
        vt = v.astype(F32).T
        for hh in range(2):
            vte_ref[hh, ch, 0:HEAD_DIM, :] = vt[hh * HEAD_DIM:(hh + 1) * HEAD_DIM].astype(BF16)
            vte_ref[hh, ch, HEAD_DIM:, :] = jnp.where(one_row, 1.0, 0.0).astype(BF16)

    def rows_of(i):
        return pl.ds(pl.multiple_of(i * NA_TQ, NA_TQ), NA_TQ)

    def window(i):
        return (jnp.clip(i - 1, 0, n_lat - NA_WIN_TICKS),
                jnp.where(i == 0, 0, jnp.where(i == n_blk - 1, 2, 1)))

    def prep(i, slot):
        lo, hi = _split_halves(q_ref[0, rows_of(i), :])
        qm_ref[slot, 0] = lo
        qm_ref[slot, 1] = hi
        m_ref[slot] = jnp.full(m_ref.shape[1:], -jnp.inf, F32)

    def score_tick(slot, i, t):
        ch0, variant = window(i)
        if t < NA_WIN_TICKS:
            k = k_ref[0, pl.ds(pl.multiple_of((ch0 + t) * NA_TK, NA_TK), NA_TK), :]
        else:
            k = kc_ref[0]
        for hh in range(2):
            s = jnp.dot(k, qm_ref[slot, hh], preferred_element_type=F32)
            if t < NA_WIN_TICKS:
                s = s + bias_ref[hh, variant, t * NA_TK:(t + 1) * NA_TK, :]
            s_ref[slot, hh, t * NA_TK:(t + 1) * NA_TK, :] = s
            part = jnp.max(s.reshape(NA_TK // SUBLANES, SUBLANES, NA_TQ), axis=0)
            m_ref[slot, hh] = jnp.maximum(m_ref[slot, hh], part)

    def final_max(slot):
        return [jnp.broadcast_to(jnp.max(m_ref[slot, hh], axis=0, keepdims=True), (SUBLANES, NA_TQ))
                for hh in range(2)]

    def pv_tick(slot, i, t, mx, acc):
        ch = window(i)[0] + t if t < NA_WIN_TICKS else n_lat
        for hh in range(2):
            s = s_ref[slot, hh, t * NA_TK:(t + 1) * NA_TK, :].reshape(NA_TK // SUBLANES, SUBLANES, NA_TQ)
            p = jnp.exp2(s - mx[hh]).reshape(NA_TK, NA_TQ).astype(BF16)
            d = jnp.dot(vte_ref[hh, ch], p, preferred_element_type=F32)
            acc[hh] = d if acc[hh] is None else acc[hh] + d

    def finish(i, acc):
        ot = jnp.concatenate(
            [acc[hh][0:HEAD_DIM, :] * (1.0 / acc[hh][HEAD_DIM:HEAD_DIM + 1, :]) for hh in range(2)],
            axis=0)
        o_ref[0, rows_of(i), :] = ot.T.astype(BF16)

    def block(i, slot, last=False):
        mx = final_max(slot)
        if not last:
            prep(i + 1, 1 - slot)
        acc = [None, None]
        for t in range(n_tick):
            pv_tick(slot, i, t, mx, acc)
            if not last:
                score_tick(1 - slot, i + 1, t)
        finish(i, acc)

    prep(0, 0)
    for t in range(n_tick):
        score_tick(0, 0, t)

    def pair(u, carry):
        block(2 * u, 0)
        block(2 * u + 1, 1)
        return carry

    lax.fori_loop(0, n_blk // 2 - 1, pair, 0)
    block(n_blk - 2, 0)
    block(n_blk - 1, 1, last=True)


def _na_attn(nq, nk, nv, nkc, nvc, rpb):
    bsz, s, _ = nq.shape
    c = nkc.shape[1]
    rows = s // GRID_W
    n_blk = rows // NA_QROWS
    assert n_blk >= 4 and n_blk % 2 == 0 and NA_QROWS * GRID_W == NA_TK and c == NA_TK
    n_pairs = NA_HEADS // 2
    n_lat = s // NA_TK
    n_r, n_c = rpb.shape[1:]
    padded = jnp.pad(rpb[:, :, ::-1], ((0, 0), (1, 1), (0, GRID_W - n_c)))
    rpb2 = jnp.concatenate([padded[:, 1:], padded[:, :-1]], axis=-1).reshape(n_pairs, 2, n_r + 1, LANES)
    return pl.pallas_call(
        functools.partial(_na_kernel, rows=rows),
        grid=(n_pairs, bsz),
        in_specs=[pl.BlockSpec((1, s, LANES), lambda p, b: (b, 0, p)),
                  pl.BlockSpec((1, c, LANES), lambda p, b: (b, 0, p)),
                  pl.BlockSpec((1, s, LANES), lambda p, b: (b, 0, p)),
                  pl.BlockSpec((1, c, LANES), lambda p, b: (b, 0, p)),
                  pl.BlockSpec((1, s, LANES), lambda p, b: (b, 0, p)),
                  pl.BlockSpec((1,) + rpb2.shape[1:], lambda p, b: (p, 0, 0, 0))],
        out_specs=pl.BlockSpec((1, s, LANES), lambda p, b: (b, 0, p)),
        out_shape=jax.ShapeDtypeStruct(nq.shape, BF16),
        scratch_shapes=[pltpu.VMEM((2, 3, NA_KROWS * GRID_W, NA_TQ), F32),
                        pltpu.VMEM((2, n_lat + 1, NA_V_ROWS, NA_TK), BF16),
                        pltpu.VMEM((2, 2, LANES, NA_TQ), BF16),
                        pltpu.VMEM((2, 2, (NA_WIN_TICKS + 1) * NA_TK, NA_TQ), F32),
                        pltpu.VMEM((2, 2, SUBLANES, NA_TQ), F32)],
        compiler_params=_cparams(("arbitrary", "arbitrary")),
        name="na_attn",
    )(nq, nkc, nk, nvc, nv, rpb2)


def _rope_tables(n_tokens):
    t = jnp.arange(n_tokens, dtype=jnp.int32)
    row = (t // GRID_W).astype(F32)
    col = (t % GRID_W).astype(F32)
    n_freq = HEAD_DIM // 4
    inv_freq = ROPE_BASE ** (-jnp.arange(n_freq, dtype=F32) / n_freq)
    ar = row[:, None] * inv_freq
    ac = col[:, None] * inv_freq
    cos = jnp.concatenate([jnp.cos(ar), jnp.cos(ar), jnp.cos(ac), jnp.cos(ac)], axis=-1)
    sin = jnp.concatenate([-jnp.sin(ar), jnp.sin(ar), -jnp.sin(ac), jnp.sin(ac)], axis=-1)
    reps = LANES // HEAD_DIM
    return jnp.tile(cos, (1, reps)), jnp.tile(sin, (1, reps))


def kernel(x, c, ctx, c_ctx, w_ada, b_ada, norm1, norm2, norm3, ffn1_w_gu, ffn1_w_down, w_in,
           diff_q_norm, diff_k_norm, lam_q1, lam_k1, lam_q2, lam_k2, diff_out_norm,
           na_q_norm, na_k_norm, na_rpb, w_out, ffn2_w_gu, ffn2_w_down):
    bsz, s, d = x.shape
    assert w_ada.shape[0] == 1 and d == D_MODEL and s % (NA_QROWS * GRID_W) == 0
    rows = s // GRID_W

    cond = jnp.zeros((16, d), F32).at[:bsz].set(c).at[bsz].set(c_ctx)
    mod4 = _adaln(cond, w_ada[0], b_ada).reshape(16, N_MOD, 1, d)

    ffn1_w = _ffn_weights(ffn1_w_gu[0], ffn1_w_down[0])
    h = _ffn(x, mod4, 0, False, norm1, ffn1_w, tm=1024)
    n_ctx = ctx.shape[1]
    hc = _ffn(ctx.reshape(1, bsz * n_ctx, d), mod4, 0, True, norm1, ffn1_w, tm=1024)

    w_in_b = w_in[0].astype(BF16)
    seg = np.arange(MXU_W) // HEAD_DIM
    bd = jnp.asarray(seg[:, None] == seg[None, :], BF16)
    cos, sin = _rope_tables(s)
    reps = GROUP_W // HEAD_DIM
    gains = [jnp.tile(g, (1, reps)) for g in (diff_q_norm, diff_k_norm, na_q_norm, na_k_norm)]
    qs = HEAD_DIM ** -0.5
    lat_groups = ((0, 0, True, qs * LOG2E), (1, 1, True, 1.0), (2, None, False, 1.0),
                  (3, 2, False, qs * LOG2E), (4, 3, False, 1.0), (5, None, False, 1.0))
    aq, ak, av, nq, nk, nv = _in_proj(h, mod4, False, norm2, w_in_b, bd, cos, sin, gains,
                                      lat_groups, tm=512)
    ctx_groups = ((1, 1, False, 1.0), (2, None, False, 1.0), (4, 3, False, 1.0), (5, None, False, 1.0))
    ctx_kv = _in_proj(hc, mod4, True, norm2, w_in_b, bd, cos, sin, gains, ctx_groups, tm=512)
    akc, avc, nkc, nvc = (a.reshape(bsz, n_ctx, GROUP_W) for a in ctx_kv)

    lamv = jnp.concatenate([lam_q1, lam_k1, lam_q2, lam_k2], axis=0)
    oa = _diff_attn(aq, ak, av, akc, avc, lamv, diff_out_norm)

    ob = _na_attn(nq, nk, nv, nkc, nvc, na_rpb[0])

    return _ffn(h, mod4, 6, False, norm3, _ffn_weights(ffn2_w_gu[0], ffn2_w_down[0]), tm=1024,
                attn=(5, oa, ob, w_out[0].astype(BF16)))
```

```python
import functools
import math

import numpy as np
import jax
import jax.numpy as jnp
from jax import lax
from jax.experimental import pallas as pl
from jax.experimental.pallas import tpu as pltpu

D_MODEL = 1024
GRID_W = 64
HEAD_DIM = 64
DIFF_HEADS = 4
NA_HEADS = 8
GROUP_W = 512
D_FF = 2816
N_MOD = 9
NA_KH = 8
NA_KW = 16
ROPE_BASE = 10000.0
NORM_EPS = 1e-6
NEG_INF = -1e30
LAM_INIT = 0.8 - 0.6 * math.exp(-0.3 * 0)
LOG2E = math.log2(math.e)

LANES = 128
MXU_W = 256
VMEM_LIMIT = 56 * 1024 * 1024

F32 = jnp.float32
BF16 = jnp.bfloat16


def _cparams(sem):
    return pltpu.CompilerParams(dimension_semantics=sem, vmem_limit_bytes=VMEM_LIMIT)


ADALN_TK = 128


def _adaln_kernel(cond_ref, w_ref, b_ref, o_ref):
    k = pl.program_id(0)
    c = cond_ref[...]
    a = (c * jax.nn.sigmoid(c)).astype(BF16)
    part = jnp.dot(a, w_ref[...].astype(BF16), preferred_element_type=F32)

    @pl.when(k == 0)
    def _():
        o_ref[...] = part + b_ref[...]

    @pl.when(k > 0)
    def _():
        o_ref[...] += part


def _adaln(cond, w_ada, b_ada):
    rows, d = cond.shape
    n = w_ada.shape[1]
    return pl.pallas_call(
        _adaln_kernel,
        grid=(d // ADALN_TK,),
        in_specs=[pl.BlockSpec((rows, ADALN_TK), lambda k: (0, k)),
                  pl.BlockSpec((ADALN_TK, n), lambda k: (k, 0)),
                  pl.BlockSpec((1, n), lambda k: (0, 0))],
        out_specs=pl.BlockSpec((rows, n), lambda k: (0, 0)),
        out_shape=jax.ShapeDtypeStruct((rows, n), F32),
        compiler_params=_cparams(("arbitrary",)),
        name="adaln",
    )(cond, w_ada, b_ada)


def _modulated_norm(x, g, shift, scale):
    ms = jnp.mean(x * x, axis=-1, keepdims=True)
    xn = x * lax.rsqrt(ms + NORM_EPS) * g
    return xn * (1.0 + scale) + shift


def _mod_spec(k, ctx):
    if ctx:
        return pl.BlockSpec((1, 1, 1, D_MODEL), lambda b, *_: (8, k, 0, 0))
    return pl.BlockSpec((1, 1, 1, D_MODEL), lambda b, *_: (b, k, 0, 0))


FFN_TF = 256


def _ffn_kernel(*refs, with_attn):
    if with_attn:
        x_ref, agate_ref, oa_ref, ob_ref, wo_ref = refs[:5]
        shift_ref, scale_ref, gate_ref, g_ref, wgu_ref, wd_ref, o_ref, xm_ref, acc_ref, h_ref = refs[5:]
        half = oa_ref.shape[-1]
        attn = (jnp.dot(oa_ref[0], wo_ref[0:half, :], preferred_element_type=F32)
                + jnp.dot(ob_ref[0], wo_ref[half:, :], preferred_element_type=F32))
        h_ref[0] = x_ref[0] + agate_ref[0, 0] * attn
    else:
        h_ref, shift_ref, scale_ref, gate_ref, g_ref, wgu_ref, wd_ref, o_ref, xm_ref, acc_ref = refs
    f = wd_ref.shape[0]
    nj = f // FFN_TF
    xm_ref[...] = _modulated_norm(h_ref[0], g_ref[...], shift_ref[0, 0], scale_ref[0, 0]).astype(BF16)

    def act(j):
        xm = xm_ref[...]
        g = jnp.dot(xm, wgu_ref[:, j * FFN_TF:(j + 1) * FFN_TF], preferred_element_type=F32)
        u = jnp.dot(xm, wgu_ref[:, f + j * FFN_TF:f + (j + 1) * FFN_TF], preferred_element_type=F32)
        return (g * jax.nn.sigmoid(g) * u).astype(BF16)

    a = act(0)
    for j in range(nj):
        down = jnp.dot(a, wd_ref[j * FFN_TF:(j + 1) * FFN_TF, :], preferred_element_type=F32)
        if j + 1 < nj:
            a = act(j + 1)
        if j == 0:
            acc_ref[...] = down
        elif j + 1 < nj:
            acc_ref[...] += down
        else:
            o_ref[0] = h_ref[0] + (0.5 * gate_ref[0, 0]) * (acc_ref[...] + down)


def _ffn_weights(w_gu, w_down):
    return w_gu.astype(BF16), w_down.astype(BF16)


def _resident(shape):
    return pl.BlockSpec(shape, lambda *_: (0,) * len(shape), pipeline_mode=pl.Buffered(1))


def _ffn(h, mod4, k0, ctx, g, weights, *, tm, attn=None):
    bsz, t, d = h.shape
    wgu, wd = weights
    assert wd.shape[0] % FFN_TF == 0
    tile = pl.BlockSpec((1, tm, d), lambda b, i: (b, i, 0))
    in_specs, args = [tile], [h]
    scratch = [pltpu.VMEM((tm, d), BF16), pltpu.VMEM((tm, d), F32)]
    if attn is not None:
        k_gate, oa, ob, w_out = attn
        half = oa.shape[-1]
        in_specs += [_mod_spec(k_gate, ctx), pl.BlockSpec((1, tm, half), lambda b, i: (b, i, 0)),
                     pl.BlockSpec((1, tm, half), lambda b, i: (b, i, 0)), _resident(w_out.shape)]
        args += [mod4, oa, ob, w_out]
        scratch.append(pltpu.VMEM((1, tm, d), F32))
    in_specs += [_mod_spec(k0, ctx), _mod_spec(k0 + 1, ctx), _mod_spec(k0 + 2, ctx),
                 pl.BlockSpec((1, d), lambda b, i: (0, 0)), _resident(wgu.shape), _resident(wd.shape)]
    args += [mod4, mod4, mod4, g, wgu, wd]
    return pl.pallas_call(
        functools.partial(_ffn_kernel, with_attn=attn is not None),
        grid=(bsz, t // tm),
        in_specs=in_specs,
        out_specs=tile,
        out_shape=jax.ShapeDtypeStruct(h.shape, F32),
        scratch_shapes=scratch,
        compiler_params=_cparams(("parallel", "parallel")),
        name="ffn_ctx" if ctx else "ffn",
    )(*args)


def _head_rms_scale(y, bd_ref):
    y2 = (y * y).astype(BF16)
    bd = bd_ref[...]
    w = bd.shape[0]
    ss = jnp.concatenate(
        [jnp.dot(y2[:, c:c + w], bd, preferred_element_type=F32) for c in range(0, y.shape[1], w)], axis=-1)
    return lax.rsqrt(ss * (1.0 / HEAD_DIM) + NORM_EPS)


def _rope(z, cos, sin):
    lane = lax.broadcasted_iota(jnp.int32, z.shape, 1)
    up = pltpu.roll(z, LANES - 16, 1)
    dn = pltpu.roll(z, 16, 1)
    return z * cos + jnp.where((lane & 16) == 0, up, dn) * sin


def _in_proj_kernel(*refs, groups, n_gain):
    h_ref, shift_ref, scale_ref, g_ref, w_ref, bd_ref, cos_ref, sin_ref = refs[:8]
    gain_refs = refs[8:8 + n_gain]
    out_refs = refs[8 + n_gain:]
    xm = _modulated_norm(h_ref[0], g_ref[...], shift_ref[0, 0], scale_ref[0, 0]).astype(BF16)
    for (col, gain_idx, rope, qscale), o_ref in zip(groups, out_refs):
        y = jnp.dot(xm, w_ref[:, col * GROUP_W:(col + 1) * GROUP_W], preferred_element_type=F32)
        if gain_idx is not None:
            gain = gain_refs[gain_idx][...]
            if qscale != 1.0:
                gain = gain * qscale
            y = y * _head_rms_scale(y, bd_ref) * gain
        if rope:
            cos = cos_ref[...]
            sin = sin_ref[...]
            for c in range(GROUP_W // LANES):
                sl = slice(c * LANES, (c + 1) * LANES)
                o_ref[0, :, sl] = _rope(y[:, sl], cos, sin).astype(BF16)
        else:
            o_ref[0] = y.astype(BF16)


def _in_proj(h, mod4, ctx, g, w_in, bd, cos, sin, gains, groups, *, tm):
    bsz, t, d = h.shape
    n_out = len(groups)
    in_specs = [pl.BlockSpec((1, tm, d), lambda b, i: (b, i, 0)),
                _mod_spec(3, ctx), _mod_spec(4, ctx),
                pl.BlockSpec((1, d), lambda b, i: (0, 0)),
                pl.BlockSpec(w_in.shape, lambda b, i: (0, 0)),
                pl.BlockSpec(bd.shape, lambda b, i: (0, 0)),
                pl.BlockSpec((tm, LANES), lambda b, i: (i, 0)),
                pl.BlockSpec((tm, LANES), lambda b, i: (i, 0))]
    in_specs += [pl.BlockSpec((1, GROUP_W), lambda b, i: (0, 0)) for _ in gains]
    return pl.pallas_call(
        functools.partial(_in_proj_kernel, groups=groups, n_gain=len(gains)),
        grid=(bsz, t // tm),
        in_specs=in_specs,
        out_specs=[pl.BlockSpec((1, tm, GROUP_W), lambda b, i: (b, i, 0)) for _ in range(n_out)],
        out_shape=[jax.ShapeDtypeStruct((bsz, t, GROUP_W), BF16) for _ in range(n_out)],
        compiler_params=_cparams(("parallel", "parallel")),
        name="in_proj_ctx" if ctx else "in_proj",
    )(h, mod4, mod4, g, w_in, bd, cos, sin, *gains)


def _split_halves(q):
    qt = q.astype(F32).T
    row = lax.broadcasted_iota(jnp.int32, qt.shape, 0)
    lo = jnp.where(row < HEAD_DIM, qt, 0.0).astype(BF16)
    hi = jnp.where(row >= HEAD_DIM, qt, 0.0).astype(BF16)
    return lo, hi


DIFF_TQ = 256
DIFF_TK = 256
SUBLANES = 8


def _diff_kernel(q_ref, kc_ref, kl_ref, vc_ref, vl_ref, lamv_ref, gcol_ref, o_ref,
                 kk_ref, vte_ref, qm_ref, s_ref, m_ref):
    n_ctx = kc_ref.shape[1]
    n_tick = kk_ref.shape[0] // DIFF_TK
    n_blk = q_ref.shape[1] // DIFF_TQ
    kk_ref[0:n_ctx] = kc_ref[0]
    kk_ref[n_ctx:] = kl_ref[0]
    for kc in range(n_tick):
        lo = kc * DIFF_TK
        v = vc_ref[0, lo:lo + DIFF_TK] if lo < n_ctx else vl_ref[0, lo - n_ctx:lo - n_ctx + DIFF_TK]
        vte_ref[kc] = v.astype(F32).T.astype(BF16)
    lv = lamv_ref[...]
    lam = (jnp.exp(jnp.sum(lv[0:1] * lv[1:2], axis=-1, keepdims=True))
           - jnp.exp(jnp.sum(lv[2:3] * lv[3:4], axis=-1, keepdims=True)) + LAM_INIT)

    def rows(i):
        return pl.ds(pl.multiple_of(i * DIFF_TQ, DIFF_TQ), DIFF_TQ)

    def keys(kc):
        return slice(kc * DIFF_TK, (kc + 1) * DIFF_TK)

    def prep(i, slot):
        lo, hi = _split_halves(q_ref[0, rows(i), :])
        qm_ref[slot, 0] = lo
        qm_ref[slot, 1] = hi
        m_ref[slot] = jnp.full(m_ref.shape[1:], -jnp.inf, F32)

    def score_tick(slot, kc):
        k = kk_ref[keys(kc), :]
        for c in range(2):
            s = jnp.dot(k, qm_ref[slot, c], preferred_element_type=F32)
            s_ref[slot, c, keys(kc), :] = s
            part = jnp.max(s.reshape(DIFF_TK // SUBLANES, SUBLANES, DIFF_TQ), axis=0)
            m_ref[slot, c] = jnp.maximum(m_ref[slot, c], part)

    def final_max(slot):
        return [jnp.broadcast_to(jnp.max(m_ref[slot, c], axis=0, keepdims=True), (SUBLANES, DIFF_TQ))
                for c in range(2)]

    def pv_tick(slot, kc, mx, acc):
        for c in range(2):
            s = s_ref[slot, c, keys(kc), :].reshape(DIFF_TK // SUBLANES, SUBLANES, DIFF_TQ)
            p = jnp.exp2(s - mx[c])
            d = [jnp.dot(vte_ref[kc], p.reshape(DIFF_TK, DIFF_TQ).astype(BF16), preferred_element_type=F32),
                 jnp.sum(p, axis=0)]
            acc[c] = d if acc[c] is None else [a + b for a, b in zip(acc[c], d)]

    def finish(i, acc):
        l0 = jnp.sum(acc[0][1], axis=0, keepdims=True)
        l1 = jnp.sum(acc[1][1], axis=0, keepdims=True)
        ot = acc[0][0] * (1.0 / l0) - acc[1][0] * (lam / l1)
        ms = jnp.mean(ot * ot, axis=0, keepdims=True)
        on = ot * lax.rsqrt(ms + NORM_EPS) * (gcol_ref[...] * (1.0 - LAM_INIT))
        o_ref[0, rows(i), :] = on.T.astype(BF16)

    def block(i, slot, last=False):
        mx = final_max(slot)
        if not last:
            prep(i + 1, 1 - slot)
        acc = [None, None]
        for kc in range(n_tick):
            pv_tick(slot, kc, mx, acc)
            if not last:
                score_tick(1 - slot, kc)
        finish(i, acc)

    prep(0, 0)
    for kc in range(n_tick):
        score_tick(0, kc)

    def pair(t, carry):
        block(2 * t, 0)
        block(2 * t + 1, 1)
        return carry

    lax.fori_loop(0, n_blk // 2 - 1, pair, 0)
    block(n_blk - 2, 0)
    block(n_blk - 1, 1, last=True)


def _diff_attn(aq, ak, av, akc, avc, lamv, gout):
    bsz, s, _ = aq.shape
    c = akc.shape[1]
    assert s % (2 * DIFF_TQ) == 0 and c % DIFF_TK == 0 and s % DIFF_TK == 0
    return pl.pallas_call(
        _diff_kernel,
        grid=(bsz, DIFF_HEADS),
        in_specs=[pl.BlockSpec((1, s, LANES), lambda b, h: (b, 0, h)),
                  pl.BlockSpec((1, c, LANES), lambda b, h: (b, 0, h)),
                  pl.BlockSpec((1, s, LANES), lambda b, h: (b, 0, h)),
                  pl.BlockSpec((1, c, LANES), lambda b, h: (b, 0, h)),
                  pl.BlockSpec((1, s, LANES), lambda b, h: (b, 0, h)),
                  pl.BlockSpec(lamv.shape, lambda b, h: (0, 0)),
                  pl.BlockSpec((LANES, 1), lambda b, h: (0, 0))],
        out_specs=pl.BlockSpec((1, s, LANES), lambda b, h: (b, 0, h)),
        out_shape=jax.ShapeDtypeStruct(aq.shape, BF16),
        scratch_shapes=[pltpu.VMEM((c + s, LANES), BF16),
                        pltpu.VMEM(((c + s) // DIFF_TK, LANES, DIFF_TK), BF16),
                        pltpu.VMEM((2, 2, LANES, DIFF_TQ), BF16),
                        pltpu.VMEM((2, 2, c + s, DIFF_TQ), F32),
                        pltpu.VMEM((2, 2, SUBLANES, DIFF_TQ), F32)],
        compiler_params=_cparams(("parallel", "parallel")),
        name="diff_attn",
    )(aq, akc, ak, avc, av, lamv, gout.reshape(LANES, 1))


NA_QROWS = 4
NA_KROWS = 12
NA_TQ = NA_QROWS * GRID_W
NA_TK = 4 * GRID_W
NA_WIN_TICKS = NA_KROWS * GRID_W // NA_TK
NA_UNROLL = 4


def _na_variants(rows):
    n_rb = rows // NA_QROWS
    out = []
    for rb in (0, 1, n_rb - 1):
        ws = min(max(rb * NA_QROWS - NA_KH // 2, 0), rows - NA_KROWS)
        out.append((rb * NA_QROWS, ws))
    return out


def _na_build_bias(rpb_ref, bias_ref, rows):
    shape = (GRID_W, LANES)
    kc = lax.broadcasted_iota(jnp.int32, shape, 0)
    lane = lax.broadcasted_iota(jnp.int32, shape, 1)
    qc = lane & (GRID_W - 1)
    cs = jnp.clip(qc - NA_KW // 2, 0, GRID_W - NA_KW)
    col_ok = (kc >= cs) & (kc < cs + NA_KW)
    side_ok = {(True, True): col_ok, (True, False): col_ok & (lane < GRID_W),
               (False, True): col_ok & (lane >= GRID_W)}
    neg = jnp.full(shape, NEG_INF, F32)
    for hh in range(2):
        toeplitz = {}
        for v, (r0, ws) in enumerate(_na_variants(rows)):
            for i in range(NA_KROWS):
                kr = ws + i
                for jp in range(NA_QROWS // 2):
                    ok = []
                    for r in (r0 + 2 * jp, r0 + 2 * jp + 1):
                        rs = min(max(r - NA_KH // 2, 0), rows - NA_KH)
                        ok.append(rs <= kr < rs + NA_KH)
                    tile = neg
                    if any(ok):
                        e = kr - (r0 + 2 * jp) + NA_KH - 1
                        if e not in toeplitz:
                            vec = jnp.broadcast_to(rpb_ref[0, hh, e:e + 1, :] * LOG2E, shape)
                            toeplitz[e] = pltpu.roll(vec, LANES - (NA_KW - 1), 1, stride=1, stride_axis=0)
                        tile = jnp.where(side_ok[tuple(ok)], toeplitz[e], neg)
                    bias_ref[hh, v, i * GRID_W:(i + 1) * GRID_W, jp * LANES:(jp + 1) * LANES] = tile


def _na_kernel(q_ref, kc_ref, k_ref, vc_ref, v_ref, rpb_ref, o_ref,
               bias_ref, vte_ref, qm_ref, s_ref, m_ref, *, rows):
    n_blk = rows // NA_QROWS
    n_lat = v_ref.shape[1] // NA_TK
    assert kc_ref.shape[1] == NA_TK
    n_tick = NA_WIN_TICKS + 1

    @pl.when(pl.program_id(1) == 0)
    def _():
        _na_build_bias(rpb_ref, bias_ref, rows)

    for ch in range(n_lat + 1):
        v = v_ref[0, ch * NA_TK:(ch + 1) * NA_TK] if ch < n_lat else vc_ref[0]
        vt = v.astype(F32).T
        for hh in range(2):
            vte_ref[hh, ch] = vt[hh * HEAD_DIM:(hh + 1) * HEAD_DIM].astype(BF16)

    def rows_of(i):
        return pl.ds(pl.multiple_of(i * NA_TQ, NA_TQ), NA_TQ)

    def window(i):
        return (jnp.clip(i - 1, 0, n_lat - NA_WIN_TICKS),
                jnp.where(i == 0, 0, jnp.where(i == n_blk - 1, 2, 1)))

    def prep(i, slot):
        lo, hi = _split_halves(q_ref[0, rows_of(i), :])
        qm_ref[slot, 0] = lo
        qm_ref[slot, 1] = hi
        m_ref[slot] = jnp.full(m_ref.shape[1:], -jnp.inf, F32)

    def score_tick(slot, i, t):
        ch0, variant = window(i)
        if t < NA_WIN_TICKS:
            k = k_ref[0, pl.ds(pl.multiple_of((ch0 + t) * NA_TK, NA_TK), NA_TK), :]
        else:
            k = kc_ref[0]
        for hh in range(2):
            s = jnp.dot(k, qm_ref[slot, hh], preferred_element_type=F32)
            if t < NA_WIN_TICKS:
                s = s + bias_ref[hh, variant, t * NA_TK:(t + 1) * NA_TK, :]
            s_ref[slot, hh, t * NA_TK:(t + 1) * NA_TK, :] = s
            part = jnp.max(s.reshape(NA_TK // SUBLANES, SUBLANES, NA_TQ), axis=0)
            m_ref[slot, hh] = jnp.maximum(m_ref[slot, hh], part)

    def final_max(slot):
        return [jnp.broadcast_to(jnp.max(m_ref[slot, hh], axis=0, keepdims=True), (SUBLANES, NA_TQ))
                for hh in range(2)]

    def pv_tick(slot, i, t, mx, acc):
        ch = window(i)[0] + t if t < NA_WIN_TICKS else n_lat
        for hh in range(2):
            s = s_ref[slot, hh, t * NA_TK:(t + 1) * NA_TK, :].reshape(NA_TK // SUBLANES, SUBLANES, NA_TQ)
            p = jnp.exp2(s - mx[hh])
            d = [jnp.dot(vte_ref[hh, ch], p.reshape(NA_TK, NA_TQ).astype(BF16), preferred_element_type=F32),
                 jnp.sum(p, axis=0)]
            acc[hh] = d if acc[hh] is None else [a + b for a, b in zip(acc[hh], d)]

    def finish(i, acc):
        ot = jnp.concatenate(
            [acc[hh][0] * (1.0 / jnp.sum(acc[hh][1], axis=0, keepdims=True)) for hh in range(2)],
            axis=0)
        o_ref[0, rows_of(i), :] = ot.T.astype(BF16)

    def block(i, slot):
        nxt = jnp.minimum(i + 1, n_blk - 1)
        mx = final_max(slot)
        prep(nxt, 1 - slot)
        acc = [None, None]
        for t in range(n_tick):
            pv_tick(slot, i, t, mx, acc)
            score_tick(1 - slot, nxt, t)
        finish(i, acc)

    prep(0, 0)
    for t in range(n_tick):
        score_tick(0, 0, t)

    def group(u, carry):
        for j in range(NA_UNROLL):
            block(NA_UNROLL * u + j, j % 2)
        return carry

    lax.fori_loop(0, n_blk // NA_UNROLL, group, 0)


def _na_attn(nq, nk, nv, nkc, nvc, rpb):
    bsz, s, _ = nq.shape
    c = nkc.shape[1]
    rows = s // GRID_W
    n_blk = rows // NA_QROWS
    assert n_blk >= 4 and n_blk % 2 == 0 and NA_QROWS * GRID_W == NA_TK and c == NA_TK
    n_pairs = NA_HEADS // 2
    n_lat = s // NA_TK
    n_r, n_c = rpb.shape[1:]
    padded = jnp.pad(rpb[:, :, ::-1], ((0, 0), (1, 1), (0, GRID_W - n_c)))
    rpb2 = jnp.concatenate([padded[:, 1:], padded[:, :-1]], axis=-1).reshape(n_pairs, 2, n_r + 1, LANES)
    return pl.pallas_call(
        functools.partial(_na_kernel, rows=rows),
        grid=(n_pairs, bsz),
        in_specs=[pl.BlockSpec((1, s, LANES), lambda p, b: (b, 0, p)),
                  pl.BlockSpec((1, c, LANES), lambda p, b: (b, 0, p)),
                  pl.BlockSpec((1, s, LANES), lambda p, b: (b, 0, p)),
                  pl.BlockSpec((1, c, LANES), lambda p, b: (b, 0, p)),
                  pl.BlockSpec((1, s, LANES), lambda p, b: (b, 0, p)),
                  pl.BlockSpec((1,) + rpb2.shape[1:], lambda p, b: (p, 0, 0, 0))],
        out_specs=pl.BlockSpec((1, s, LANES), lambda p, b: (b, 0, p)),
        out_shape=jax.ShapeDtypeStruct(nq.shape, BF16),
        scratch_shapes=[pltpu.VMEM((2, 3, NA_KROWS * GRID_W, NA_TQ), F32),
                        pltpu.VMEM((2, n_lat + 1, HEAD_DIM, NA_TK), BF16),
                        pltpu.VMEM((2, 2, LANES, NA_TQ), BF16),
                        pltpu.VMEM((2, 2, (NA_WIN_TICKS + 1) * NA_TK, NA_TQ), F32),
                        pltpu.VMEM((2, 2, SUBLANES, NA_TQ), F32)],
        compiler_params=_cparams(("arbitrary", "arbitrary")),
        name="na_attn",
    )(nq, nkc, nk, nvc, nv, rpb2)


def _rope_tables(n_tokens):
    t = jnp.arange(n_tokens, dtype=jnp.int32)
    row = (t // GRID_W).astype(F32)
    col = (t % GRID_W).astype(F32)
    n_freq = HEAD_DIM // 4
    inv_freq = ROPE_BASE ** (-jnp.arange(n_freq, dtype=F32) / n_freq)
    ar = row[:, None] * inv_freq
    ac = col[:, None] * inv_freq
    cos = jnp.concatenate([jnp.cos(ar), jnp.cos(ar), jnp.cos(ac), jnp.cos(ac)], axis=-1)
    sin = jnp.concatenate([-jnp.sin(ar), jnp.sin(ar), -jnp.sin(ac), jnp.sin(ac)], axis=-1)
    reps = LANES // HEAD_DIM
    return jnp.tile(cos, (1, reps)), jnp.tile(sin, (1, reps))


def kernel(x, c, ctx, c_ctx, w_ada, b_ada, norm1, norm2, norm3, ffn1_w_gu, ffn1_w_down, w_in,
           diff_q_norm, diff_k_norm, lam_q1, lam_k1, lam_q2, lam_k2, diff_out_norm,
           na_q_norm, na_k_norm, na_rpb, w_out, ffn2_w_gu, ffn2_w_down):
    bsz, s, d = x.shape
    assert w_ada.shape[0] == 1 and d == D_MODEL and s % (NA_QROWS * GRID_W) == 0
    rows = s // GRID_W

    cond = jnp.zeros((16, d), F32).at[:bsz].set(c).at[bsz].set(c_ctx)
    mod4 = _adaln(cond, w_ada[0], b_ada).reshape(16, N_MOD, 1, d)

    ffn1_w = _ffn_weights(ffn1_w_gu[0], ffn1_w_down[0])
    h = _ffn(x, mod4, 0, False, norm1, ffn1_w, tm=1024)
    n_ctx = ctx.shape[1]
    hc = _ffn(ctx.reshape(1, bsz * n_ctx, d), mod4, 0, True, norm1, ffn1_w, tm=1024)

    w_in_b = w_in[0].astype(BF16)
    seg = np.arange(MXU_W) // HEAD_DIM
    bd = jnp.asarray(seg[:, None] == seg[None, :], BF16)
    cos, sin = _rope_tables(s)
    reps = GROUP_W // HEAD_DIM
    gains = [jnp.tile(g, (1, reps)) for g in (diff_q_norm, diff_k_norm, na_q_norm, na_k_norm)]
    qs = HEAD_DIM ** -0.5
    lat_groups = ((0, 0, True, qs * LOG2E), (1, 1, True, 1.0), (2, None, False, 1.0),
                  (3, 2, False, qs * LOG2E), (4, 3, False, 1.0), (5, None, False, 1.0))
    aq, ak, av, nq, nk, nv = _in_proj(h, mod4, False, norm2, w_in_b, bd, cos, sin, gains,
                                      lat_groups, tm=512)
    ctx_groups = ((1, 1, False, 1.0), (2, None, False, 1.0), (4, 3, False, 1.0), (5, None, False, 1.0))
    ctx_kv = _in_proj(hc, mod4, True, norm2, w_in_b, bd, cos, sin, gains, ctx_groups, tm=512)
    akc, avc, nkc, nvc = (a.reshape(bsz, n_ctx, GROUP_W) for a in ctx_kv)

    lamv = jnp.concatenate([lam_q1, lam_k1, lam_q2, lam_k2], axis=0)
    oa = _diff_attn(aq, ak, av, akc, avc, lamv, diff_out_norm)

    ob = _na_attn(nq, nk, nv, nkc, nvc, na_rpb[0])

    return _ffn(h, mod4, 6, False, norm3, _ffn_weights(ffn2_w_gu[0], ffn2_w_down[0]), tm=1024,
                attn=(5, oa, ob, w_out[0].astype(BF16)))
```

```python
import functools
import math

import numpy as np
import jax
import jax.numpy as jnp
from jax import lax
from jax.experimental import pallas as pl
from jax.experimental.pallas import tpu as pltpu

D_MODEL = 1024
GRID_W = 64
HEAD_DIM = 64
DIFF_HEADS = 4
NA_HEADS = 8
GROUP_W = 512
D_FF = 2816
N_MOD = 9
NA_KH = 8
NA_KW = 16
ROPE_BASE = 10000.0
NORM_EPS = 1e-6
NEG_INF = -1e30
LAM_INIT = 0.8 - 0.6 * math.exp(-0.3 * 0)
LOG2E = math.log2(math.e)

LANES = 128
MXU_W = 256
VMEM_LIMIT = 56 * 1024 * 1024

F32 = jnp.float32
BF16 = jnp.bfloat16


def _cparams(sem):
    return pltpu.CompilerParams(dimension_semantics=sem, vmem_limit_bytes=VMEM_LIMIT)


ADALN_TK = 128


def _adaln_kernel(cond_ref, w_ref, b_ref, o_ref):
    k = pl.program_id(0)
    c = cond_ref[...]
    a = (c * jax.nn.sigmoid(c)).astype(BF16)
    part = jnp.dot(a, w_ref[...].astype(BF16), preferred_element_type=F32)

    @pl.when(k == 0)
    def _():
        o_ref[...] = part + b_ref[...]

    @pl.when(k > 0)
    def _():
        o_ref[...] += part


def _adaln(cond, w_ada, b_ada):
    rows, d = cond.shape
    n = w_ada.shape[1]
    return pl.pallas_call(
        _adaln_kernel,
        grid=(d // ADALN_TK,),
        in_specs=[pl.BlockSpec((rows, ADALN_TK), lambda k: (0, k)),
                  pl.BlockSpec((ADALN_TK, n), lambda k: (k, 0)),
                  pl.BlockSpec((1, n), lambda k: (0, 0))],
        out_specs=pl.BlockSpec((rows, n), lambda k: (0, 0)),
        out_shape=jax.ShapeDtypeStruct((rows, n), F32),
        compiler_params=_cparams(("arbitrary",)),
        name="adaln",
    )(cond, w_ada, b_ada)


def _modulated_norm(x, g, shift, scale):
    ms = jnp.mean(x * x, axis=-1, keepdims=True)
    xn = x * lax.rsqrt(ms + NORM_EPS) * g
    return xn * (1.0 + scale) + shift


def _mod_spec(k, ctx):
    if ctx:
        return pl.BlockSpec((1, 1, 1, D_MODEL), lambda b, *_: (8, k, 0, 0))
    return pl.BlockSpec((1, 1, 1, D_MODEL), lambda b, *_: (b, k, 0, 0))


FFN_TF = 256


def _ffn_kernel(*refs, with_attn):
    if with_attn:
        x_ref, agate_ref, oa_ref, ob_ref, wo_ref = refs[:5]
        shift_ref, scale_ref, gate_ref, g_ref, wgu_ref, wd_ref, o_ref, xm_ref, acc_ref, h_ref = refs[5:]
        half = oa_ref.shape[-1]
        attn = (jnp.dot(oa_ref[0], wo_ref[0:half, :], preferred_element_type=F32)
                + jnp.dot(ob_ref[0], wo_ref[half:, :], preferred_element_type=F32))
        h_ref[0] = x_ref[0] + agate_ref[0, 0] * attn
    else:
        h_ref, shift_ref, scale_ref, gate_ref, g_ref, wgu_ref, wd_ref, o_ref, xm_ref, acc_ref = refs
    f = wd_ref.shape[0]
    nj = f // FFN_TF
    xm_ref[...] = _modulated_norm(h_ref[0], g_ref[...], shift_ref[0, 0], scale_ref[0, 0]).astype(BF16)

    def act(j):
        xm = xm_ref[...]
        g = jnp.dot(xm, wgu_ref[:, j * FFN_TF:(j + 1) * FFN_TF], preferred_element_type=F32)
        u = jnp.dot(xm, wgu_ref[:, f + j * FFN_TF:f + (j + 1) * FFN_TF], preferred_element_type=F32)
        return (g * jax.nn.sigmoid(g) * u).astype(BF16)

    a = act(0)
    for j in range(nj):
        down = jnp.dot(a, wd_ref[j * FFN_TF:(j + 1) * FFN_TF, :], preferred_element_type=F32)
        if j + 1 < nj:
            a = act(j + 1)
        if j == 0:
            acc_ref[...] = down
        elif j + 1 < nj:
            acc_ref[...] += down
        else:
            o_ref[0] = h_ref[0] + (0.5 * gate_ref[0, 0]) * (acc_ref[...] + down)


def _ffn_weights(w_gu, w_down):
    return w_gu.astype(BF16), w_down.astype(BF16)


def _resident(shape):
    return pl.BlockSpec(shape, lambda *_: (0,) * len(shape), pipeline_mode=pl.Buffered(1))


def _ffn(h, mod4, k0, ctx, g, weights, *, tm, attn=None):
    bsz, t, d = h.shape
    wgu, wd = weights
    assert wd.shape[0] % FFN_TF == 0
    tile = pl.BlockSpec((1, tm, d), lambda b, i: (b, i, 0))
    in_specs, args = [tile], [h]
    scratch = [pltpu.VMEM((tm, d), BF16), pltpu.VMEM((tm, d), F32)]
    if attn is not None:
        k_gate, oa, ob, w_out = attn
        half = oa.shape[-1]
        in_specs += [_mod_spec(k_gate, ctx), pl.BlockSpec((1, tm, half), lambda b, i: (b, i, 0)),
                     pl.BlockSpec((1, tm, half), lambda b, i: (b, i, 0)), _resident(w_out.shape)]
        args += [mod4, oa, ob, w_out]
        scratch.append(pltpu.VMEM((1, tm, d), F32))
    in_specs += [_mod_spec(k0, ctx), _mod_spec(k0 + 1, ctx), _mod_spec(k0 + 2, ctx),
                 pl.BlockSpec((1, d), lambda b, i: (0, 0)), _resident(wgu.shape), _resident(wd.shape)]
    args += [mod4, mod4, mod4, g, wgu, wd]
    return pl.pallas_call(
        functools.partial(_ffn_kernel, with_attn=attn is not None),
        grid=(bsz, t // tm),
        in_specs=in_specs,
        out_specs=tile,
        out_shape=jax.ShapeDtypeStruct(h.shape, F32),
        scratch_shapes=scratch,
        compiler_params=_cparams(("parallel", "parallel")),
        name="ffn_ctx" if ctx else "ffn",
    )(*args)


def _head_rms_scale(y, bd_ref):
    y2 = (y * y).astype(BF16)
    bd = bd_ref[...]
    w = bd.shape[0]
    ss = jnp.concatenate(
        [jnp.dot(y2[:, c:c + w], bd, preferred_element_type=F32) for c in range(0, y.shape[1], w)], axis=-1)
    return lax.rsqrt(ss * (1.0 / HEAD_DIM) + NORM_EPS)


def _rope(z, cos, sin):
    lane = lax.broadcasted_iota(jnp.int32, z.shape, 1)
    up = pltpu.roll(z, LANES - 16, 1)
    dn = pltpu.roll(z, 16, 1)
    return z * cos + jnp.where((lane & 16) == 0, up, dn) * sin


def _in_proj_kernel(*refs, groups, n_gain):
    h_ref, shift_ref, scale_ref, g_ref, w_ref, bd_ref, cos_ref, sin_ref = refs[:8]
    gain_refs = refs[8:8 + n_gain]
    out_refs = refs[8 + n_gain:]
    xm = _modulated_norm(h_ref[0], g_ref[...], shift_ref[0, 0], scale_ref[0, 0]).astype(BF16)
    for (col, gain_idx, rope, qscale), o_ref in zip(groups, out_refs):
        y = jnp.dot(xm, w_ref[:, col * GROUP_W:(col + 1) * GROUP_W], preferred_element_type=F32)
        if gain_idx is not None:
            gain = gain_refs[gain_idx][...]
            if qscale != 1.0:
                gain = gain * qscale
            y = y * _head_rms_scale(y, bd_ref) * gain
        if rope:
            cos = cos_ref[...]
            sin = sin_ref[...]
            for c in range(GROUP_W // LANES):
                sl = slice(c * LANES, (c + 1) * LANES)
                o_ref[0, :, sl] = _rope(y[:, sl], cos, sin).astype(BF16)
        else:
            o_ref[0] = y.astype(BF16)


def _in_proj(h, mod4, ctx, g, w_in, bd, cos, sin, gains, groups, *, tm):
    bsz, t, d = h.shape
    n_out = len(groups)
    in_specs = [pl.BlockSpec((1, tm, d), lambda b, i: (b, i, 0)),
                _mod_spec(3, ctx), _mod_spec(4, ctx),
                pl.BlockSpec((1, d), lambda b, i: (0, 0)),
                pl.BlockSpec(w_in.shape, lambda b, i: (0, 0)),
                pl.BlockSpec(bd.shape, lambda b, i: (0, 0)),
                pl.BlockSpec((tm, LANES), lambda b, i: (i, 0)),
                pl.BlockSpec((tm, LANES), lambda b, i: (i, 0))]
    in_specs += [pl.BlockSpec((1, GROUP_W), lambda b, i: (0, 0)) for _ in gains]
    return pl.pallas_call(
        functools.partial(_in_proj_kernel, groups=groups, n_gain=len(gains)),
        grid=(bsz, t // tm),
        in_specs=in_specs,
        out_specs=[pl.BlockSpec((1, tm, GROUP_W), lambda b, i: (b, i, 0)) for _ in range(n_out)],
        out_shape=[jax.ShapeDtypeStruct((bsz, t, GROUP_W), BF16) for _ in range(n_out)],
        compiler_params=_cparams(("parallel", "parallel")),
        name="in_proj_ctx" if ctx else "in_proj",
    )(h, mod4, mod4, g, w_in, bd, cos, sin, *gains)


def _split_halves(q):
    qt = q.astype(F32).T
    row = lax.broadcasted_iota(jnp.int32, qt.shape, 0)
    lo = jnp.where(row < HEAD_DIM, qt, 0.0).astype(BF16)
    hi = jnp.where(row >= HEAD_DIM, qt, 0.0).astype(BF16)
    return lo, hi


DIFF_TQ = 256
DIFF_TK = 256
SUBLANES = 8


def _interleave(*stages):
    order = sorted(((k + 0.5) / len(st), n, k) for n, st in enumerate(stages) for k in range(len(st)))
    for _, n, k in order:
        stages[n][k]()


def _diff_pipeline(q_ref, kc_ref, kl_ref, vc_ref, vl_ref, lamv_ref, gcol_ref, o_ref,
                   kk_ref, vte_ref, qm_ref, s_ref, m_ref):
    n_ctx = kc_ref.shape[1]
    n_tick = kk_ref.shape[0] // DIFF_TK
    n_blk = q_ref.shape[1] // DIFF_TQ
    kk_ref[0:n_ctx] = kc_ref[0]
    kk_ref[n_ctx:] = kl_ref[0]
    for kc in range(n_tick):
        lo = kc * DIFF_TK
        v = vc_ref[0, lo:lo + DIFF_TK] if lo < n_ctx else vl_ref[0, lo - n_ctx:lo - n_ctx + DIFF_TK]
        vte_ref[kc] = v.astype(F32).T.astype(BF16)
    lv = lamv_ref[...]
    lam = (jnp.exp(jnp.sum(lv[0:1] * lv[1:2], axis=-1, keepdims=True))
           - jnp.exp(jnp.sum(lv[2:3] * lv[3:4], axis=-1, keepdims=True)) + LAM_INIT)

    def rows(i):
        return pl.ds(pl.multiple_of(i * DIFF_TQ, DIFF_TQ), DIFF_TQ)

    def keys(kc):
        return slice(kc * DIFF_TK, (kc + 1) * DIFF_TK)

    def prep(i, slot):
        lo, hi = _split_halves(q_ref[0, rows(i), :])
        qm_ref[slot, 0] = lo
        qm_ref[slot, 1] = hi
        m_ref[slot] = jnp.full(m_ref.shape[1:], -jnp.inf, F32)

    def score_tick(slot, kc):
        k = kk_ref[keys(kc), :]
        for c in range(2):
            s = jnp.dot(k, qm_ref[slot, c], preferred_element_type=F32)
            s_ref[slot, c, keys(kc), :] = s
            part = jnp.max(s.reshape(DIFF_TK // SUBLANES, SUBLANES, DIFF_TQ), axis=0)
            m_ref[slot, c] = jnp.maximum(m_ref[slot, c], part)

    def final_max(slot):
        return [jnp.broadcast_to(jnp.max(m_ref[slot, c], axis=0, keepdims=True), (SUBLANES, DIFF_TQ))
                for c in range(2)]

    def pv_tick(slot, kc, mx, acc):
        for c in range(2):
            s = s_ref[slot, c, keys(kc), :].reshape(DIFF_TK // SUBLANES, SUBLANES, DIFF_TQ)
            p = jnp.exp2(s - mx[c])
            d = [jnp.dot(vte_ref[kc], p.reshape(DIFF_TK, DIFF_TQ).astype(BF16), preferred_element_type=F32),
                 jnp.sum(p, axis=0)]
            acc[c] = d if acc[c] is None else [a + b for a, b in zip(acc[c], d)]

    def finish(i, acc):
        l0 = jnp.sum(acc[0][1], axis=0, keepdims=True)
        l1 = jnp.sum(acc[1][1], axis=0, keepdims=True)
        ot = acc[0][0] * (1.0 / l0) - acc[1][0] * (lam / l1)
        ms = jnp.mean(ot * ot, axis=0, keepdims=True)
        on = ot * lax.rsqrt(ms + NORM_EPS) * (gcol_ref[...] * (1.0 - LAM_INIT))
        o_ref[0, rows(i), :] = on.T.astype(BF16)

    def start():
        return ([functools.partial(prep, 0, 0)]
                + [functools.partial(score_tick, 0, kc) for kc in range(n_tick)])

    def block(i, slot, last=False):
        state = {}

        def head():
            state["mx"] = final_max(slot)
            state["acc"] = [None, None]
            if not last:
                prep(i + 1, 1 - slot)

        def tick(kc):
            pv_tick(slot, kc, state["mx"], state["acc"])
            if not last:
                score_tick(1 - slot, kc)

        return ([head] + [functools.partial(tick, kc) for kc in range(n_tick)]
                + [lambda: finish(i, state["acc"])])

    return start, block


def _diff_scratch(c, s):
    return [pltpu.VMEM((c + s, LANES), BF16),
            pltpu.VMEM(((c + s) // DIFF_TK, LANES, DIFF_TK), BF16),
            pltpu.VMEM((2, 2, LANES, DIFF_TQ), BF16),
            pltpu.VMEM((2, 2, c + s, DIFF_TQ), F32),
            pltpu.VMEM((2, 2, SUBLANES, DIFF_TQ), F32)]


NA_QROWS = 4
NA_KROWS = 12
NA_TQ = NA_QROWS * GRID_W
NA_TK = 4 * GRID_W
NA_WIN_TICKS = NA_KROWS * GRID_W // NA_TK


def _na_variants(rows):
    n_rb = rows // NA_QROWS
    out = []
    for rb in (0, 1, n_rb - 1):
        ws = min(max(rb * NA_QROWS - NA_KH // 2, 0), rows - NA_KROWS)
        out.append((rb * NA_QROWS, ws))
    return out


def _na_build_bias(rpb_ref, bias_ref, rows):
    shape = (GRID_W, LANES)
    kc = lax.broadcasted_iota(jnp.int32, shape, 0)
    lane = lax.broadcasted_iota(jnp.int32, shape, 1)
    qc = lane & (GRID_W - 1)
    cs = jnp.clip(qc - NA_KW // 2, 0, GRID_W - NA_KW)
    col_ok = (kc >= cs) & (kc < cs + NA_KW)
    side_ok = {(True, True): col_ok, (True, False): col_ok & (lane < GRID_W),
               (False, True): col_ok & (lane >= GRID_W)}
    neg = jnp.full(shape, NEG_INF, F32)
    for hh in range(2):
        toeplitz = {}
        for v, (r0, ws) in enumerate(_na_variants(rows)):
            for i in range(NA_KROWS):
                kr = ws + i
                for jp in range(NA_QROWS // 2):
                    ok = []
                    for r in (r0 + 2 * jp, r0 + 2 * jp + 1):
                        rs = min(max(r - NA_KH // 2, 0), rows - NA_KH)
                        ok.append(rs <= kr < rs + NA_KH)
                    tile = neg
                    if any(ok):
                        e = kr - (r0 + 2 * jp) + NA_KH - 1
                        if e not in toeplitz:
                            vec = jnp.broadcast_to(rpb_ref[0, hh, e:e + 1, :] * LOG2E, shape)
                            toeplitz[e] = pltpu.roll(vec, LANES - (NA_KW - 1), 1, stride=1, stride_axis=0)
                        tile = jnp.where(side_ok[tuple(ok)], toeplitz[e], neg)
                    bias_ref[hh, v, i * GRID_W:(i + 1) * GRID_W, jp * LANES:(jp + 1) * LANES] = tile


def _na_pipeline(q_ref, kc_ref, k_ref, vc_ref, v_ref, rpb_ref, o_ref,
                 bias_ref, vte_ref, qm_ref, s_ref, m_ref, *, rows):
    n_blk = rows // NA_QROWS
    n_lat = v_ref.shape[1] // NA_TK
    assert kc_ref.shape[1] == NA_TK
    n_tick = NA_WIN_TICKS + 1

    @pl.when(pl.program_id(1) == 0)
    def _():
        _na_build_bias(rpb_ref, bias_ref, rows)

    for ch in range(n_lat + 1):
        v = v_ref[0, ch * NA_TK:(ch + 1) * NA_TK] if ch < n_lat else vc_ref[0]
        vt = v.astype(F32).T
        for hh in range(2):
            vte_ref[hh, ch] = vt[hh * HEAD_DIM:(hh + 1) * HEAD_DIM].astype(BF16)

    def rows_of(i):
        return pl.ds(pl.multiple_of(i * NA_TQ, NA_TQ), NA_TQ)

    def window(i):
        return (jnp.clip(i - 1, 0, n_lat - NA_WIN_TICKS),
                jnp.where(i == 0, 0, jnp.where(i == n_blk - 1, 2, 1)))

    def prep(i, slot):
        lo, hi = _split_halves(q_ref[0, rows_of(i), :])
        qm_ref[slot, 0] = lo
        qm_ref[slot, 1] = hi
        m_ref[slot] = jnp.full(m_ref.shape[1:], -jnp.inf, F32)

    def score_tick(slot, i, t):
        ch0, variant = window(i)
        if t < NA_WIN_TICKS:
            k = k_ref[0, pl.ds(pl.multiple_of((ch0 + t) * NA_TK, NA_TK), NA_TK), :]
        else:
            k = kc_ref[0]
        for hh in range(2):
            s = jnp.dot(k, qm_ref[slot, hh], preferred_element_type=F32)
            if t < NA_WIN_TICKS:
                s = s + bias_ref[hh, variant, t * NA_TK:(t + 1) * NA_TK, :]
            s_ref[slot, hh, t * NA_TK:(t + 1) * NA_TK, :] = s
            part = jnp.max(s.reshape(NA_TK // SUBLANES, SUBLANES, NA_TQ), axis=0)
            m_ref[slot, hh] = jnp.maximum(m_ref[slot, hh], part)

    def final_max(slot):
        return [jnp.broadcast_to(jnp.max(m_ref[slot, hh], axis=0, keepdims=True), (SUBLANES, NA_TQ))
                for hh in range(2)]

    def pv_tick(slot, i, t, mx, acc):
        ch = window(i)[0] + t if t < NA_WIN_TICKS else n_lat
        for hh in range(2):
            s = s_ref[slot, hh, t * NA_TK:(t + 1) * NA_TK, :].reshape(NA_TK // SUBLANES, SUBLANES, NA_TQ)
            p = jnp.exp2(s - mx[hh])
            d = [jnp.dot(vte_ref[hh, ch], p.reshape(NA_TK, NA_TQ).astype(BF16), preferred_element_type=F32),
                 jnp.sum(p, axis=0)]
            acc[hh] = d if acc[hh] is None else [a + b for a, b in zip(acc[hh], d)]

    def finish(i, acc):
        ot = jnp.concatenate(
            [acc[hh][0] * (1.0 / jnp.sum(acc[hh][1], axis=0, keepdims=True)) for hh in range(2)],
            axis=0)
        o_ref[0, rows_of(i), :] = ot.T.astype(BF16)

    def start():
        return ([functools.partial(prep, 0, 0)]
                + [functools.partial(score_tick, 0, 0, t) for t in range(n_tick)])

    def block(i, slot, last=False):
        state = {}

        def head():
            state["mx"] = final_max(slot)
            state["acc"] = [None, None]
            if not last:
                prep(i + 1, 1 - slot)

        def tick(t):
            pv_tick(slot, i, t, state["mx"], state["acc"])
            if not last:
                score_tick(1 - slot, i + 1, t)

        return ([head] + [functools.partial(tick, t) for t in range(n_tick)]
                + [lambda: finish(i, state["acc"])])

    return start, block


def _na_scratch(n_lat):
    return [pltpu.VMEM((2, 3, NA_KROWS * GRID_W, NA_TQ), F32),
            pltpu.VMEM((2, n_lat + 1, HEAD_DIM, NA_TK), BF16),
            pltpu.VMEM((2, 2, LANES, NA_TQ), BF16),
            pltpu.VMEM((2, 2, (NA_WIN_TICKS + 1) * NA_TK, NA_TQ), F32),
            pltpu.VMEM((2, 2, SUBLANES, NA_TQ), F32)]


N_DIFF_IN, N_NA_IN = 7, 6


def _attn_kernel(*refs, rows):
    d_in = refs[:N_DIFF_IN]
    n_in = refs[N_DIFF_IN:N_DIFF_IN + N_NA_IN]
    oa_ref, ob_ref = refs[N_DIFF_IN + N_NA_IN:N_DIFF_IN + N_NA_IN + 2]
    scratch = refs[N_DIFF_IN + N_NA_IN + 2:]
    d_start, d_block = _diff_pipeline(*d_in, oa_ref, *scratch[:5])
    n_start, n_block = _na_pipeline(*n_in, ob_ref, *scratch[5:], rows=rows)
    n_blk = rows // NA_QROWS
    _interleave(d_start(), n_start())

    def pair(u, carry):
        for j in range(2):
            _interleave(d_block(2 * u + j, j), n_block(2 * u + j, j))
        return carry

    lax.fori_loop(0, n_blk // 2 - 1, pair, 0)
    _interleave(d_block(n_blk - 2, 0), n_block(n_blk - 2, 0))
    _interleave(d_block(n_blk - 1, 1, last=True), n_block(n_blk - 1, 1, last=True))


def _attention(aq, ak, av, akc, avc, lamv, gout, nq, nk, nv, nkc, nvc, rpb):
    bsz, s, _ = aq.shape
    c = akc.shape[1]
    rows = s // GRID_W
    n_blk = rows // NA_QROWS
    assert NA_HEADS // 2 == DIFF_HEADS and NA_TQ == DIFF_TQ and NA_QROWS * GRID_W == NA_TK and c == NA_TK
    assert n_blk >= 4 and n_blk % 2 == 0 and c % DIFF_TK == 0 and s % DIFF_TK == 0
    n_r, n_c = rpb.shape[1:]
    padded = jnp.pad(rpb[:, :, ::-1], ((0, 0), (1, 1), (0, GRID_W - n_c)))
    rpb2 = jnp.concatenate([padded[:, 1:], padded[:, :-1]], axis=-1).reshape(DIFF_HEADS, 2, n_r + 1, LANES)
    lat = pl.BlockSpec((1, s, LANES), lambda h, b: (b, 0, h))
    ctx = pl.BlockSpec((1, c, LANES), lambda h, b: (b, 0, h))
    return pl.pallas_call(
        functools.partial(_attn_kernel, rows=rows),
        grid=(DIFF_HEADS, bsz),
        in_specs=[lat, ctx, lat, ctx, lat,
                  pl.BlockSpec(lamv.shape, lambda h, b: (0, 0)),
                  pl.BlockSpec((LANES, 1), lambda h, b: (0, 0)),
                  lat, ctx, lat, ctx, lat,
                  pl.BlockSpec((1,) + rpb2.shape[1:], lambda h, b: (h, 0, 0, 0))],
        out_specs=[lat, lat],
        out_shape=[jax.ShapeDtypeStruct(aq.shape, BF16), jax.ShapeDtypeStruct(nq.shape, BF16)],
        scratch_shapes=_diff_scratch(c, s) + _na_scratch(s // NA_TK),
        compiler_params=_cparams(("arbitrary", "arbitrary")),
        name="attention",
    )(aq, akc, ak, avc, av, lamv, gout.reshape(LANES, 1), nq, nkc, nk, nvc, nv, rpb2)


def _rope_tables(n_tokens):
    t = jnp.arange(n_tokens, dtype=jnp.int32)
    row = (t // GRID_W).astype(F32)
    col = (t % GRID_W).astype(F32)
    n_freq = HEAD_DIM // 4
    inv_freq = ROPE_BASE ** (-jnp.arange(n_freq, dtype=F32) / n_freq)
    ar = row[:, None] * inv_freq
    ac = col[:, None] * inv_freq
    cos = jnp.concatenate([jnp.cos(ar), jnp.cos(ar), jnp.cos(ac), jnp.cos(ac)], axis=-1)
    sin = jnp.concatenate([-jnp.sin(ar), jnp.sin(ar), -jnp.sin(ac), jnp.sin(ac)], axis=-1)
    reps = LANES // HEAD_DIM
    return jnp.tile(cos, (1, reps)), jnp.tile(sin, (1, reps))


def kernel(x, c, ctx, c_ctx, w_ada, b_ada, norm1, norm2, norm3, ffn1_w_gu, ffn1_w_down, w_in,
           diff_q_norm, diff_k_norm, lam_q1, lam_k1, lam_q2, lam_k2, diff_out_norm,
           na_q_norm, na_k_norm, na_rpb, w_out, ffn2_w_gu, ffn2_w_down):
    bsz, s, d = x.shape
    assert w_ada.shape[0] == 1 and d == D_MODEL and s % (NA_QROWS * GRID_W) == 0
    rows = s // GRID_W

    cond = jnp.zeros((16, d), F32).at[:bsz].set(c).at[bsz].set(c_ctx)
    mod4 = _adaln(cond, w_ada[0], b_ada).reshape(16, N_MOD, 1, d)

    ffn1_w = _ffn_weights(ffn1_w_gu[0], ffn1_w_down[0])
    h = _ffn(x, mod4, 0, False, norm1, ffn1_w, tm=1024)
    n_ctx = ctx.shape[1]
    hc = _ffn(ctx.reshape(1, bsz * n_ctx, d), mod4, 0, True, norm1, ffn1_w, tm=1024)

    w_in_b = w_in[0].astype(BF16)
    seg = np.arange(MXU_W) // HEAD_DIM
    bd = jnp.asarray(seg[:, None] == seg[None, :], BF16)
    cos, sin = _rope_tables(s)
    reps = GROUP_W // HEAD_DIM
    gains = [jnp.tile(g, (1, reps)) for g in (diff_q_norm, diff_k_norm, na_q_norm, na_k_norm)]
    qs = HEAD_DIM ** -0.5
    lat_groups = ((0, 0, True, qs * LOG2E), (1, 1, True, 1.0), (2, None, False, 1.0),
                  (3, 2, False, qs * LOG2E), (4, 3, False, 1.0), (5, None, False, 1.0))
    aq, ak, av, nq, nk, nv = _in_proj(h, mod4, False, norm2, w_in_b, bd, cos, sin, gains,
                                      lat_groups, tm=512)
    ctx_groups = ((1, 1, False, 1.0), (2, None, False, 1.0), (4, 3, False, 1.0), (5, None, False, 1.0))
    ctx_kv = _in_proj(hc, mod4, True, norm2, w_in_b, bd, cos, sin, gains, ctx_groups, tm=512)
    akc, avc, nkc, nvc = (a.reshape(bsz, n_ctx, GROUP_W) for a in ctx_kv)

    lamv = jnp.concatenate([lam_q1, lam_k1, lam_q2, lam_k2], axis=0)
    oa, ob = _attention(aq, ak, av, akc, avc, lamv, diff_out_norm, nq, nk, nv, nkc, nvc, na_rpb[0])

    return _ffn(h, mod4, 6, False, norm3, _ffn_weights(ffn2_w_gu[0], ffn2_w_down[0]), tm=1024,
                attn=(5, oa, ob, w_out[0].astype(BF16)))
```

```python
import functools
import math

import numpy as np
import jax
import jax.numpy as jnp
from jax import lax
from jax.experimental import pallas as pl
from jax.experimental.pallas import tpu as pltpu

D_MODEL = 1024
GRID_W = 64
HEAD_DIM = 64
DIFF_HEADS = 4
NA_HEADS = 8
GROUP_W = 512
D_FF = 2816
N_MOD = 9
NA_KH = 8
NA_KW = 16
ROPE_BASE = 10000.0
NORM_EPS = 1e-6
NEG_INF = -1e30
LAM_INIT = 0.8 - 0.6 * math.exp(-0.3 * 0)
LOG2E = math.log2(math.e)

LANES = 128
MXU_W = 256
VMEM_LIMIT = 56 * 1024 * 1024

F32 = jnp.float32
BF16 = jnp.bfloat16


def _cparams(sem):
    return pltpu.CompilerParams(dimension_semantics=sem, vmem_limit_bytes=VMEM_LIMIT)


ADALN_TK = 128


def _adaln_kernel(cond_ref, w_ref, b_ref, o_ref):
    k = pl.program_id(0)
    c = cond_ref[...]
    a = (c * jax.nn.sigmoid(c)).astype(BF16)
    part = jnp.dot(a, w_ref[...].astype(BF16), preferred_element_type=F32)

    @pl.when(k == 0)
    def _():
        o_ref[...] = part + b_ref[...]

    @pl.when(k > 0)
    def _():
        o_ref[...] += part


def _adaln(cond, w_ada, b_ada):
    rows, d = cond.shape
    n = w_ada.shape[1]
    return pl.pallas_call(
        _adaln_kernel,
        grid=(d // ADALN_TK,),
        in_specs=[pl.BlockSpec((rows, ADALN_TK), lambda k: (0, k)),
                  pl.BlockSpec((ADALN_TK, n), lambda k: (k, 0)),
                  pl.BlockSpec((1, n), lambda k: (0, 0))],
        out_specs=pl.BlockSpec((rows, n), lambda k: (0, 0)),
        out_shape=jax.ShapeDtypeStruct((rows, n), F32),
        compiler_params=_cparams(("arbitrary",)),
        name="adaln",
    )(cond, w_ada, b_ada)


def _modulated_norm(x, g, shift, scale):
    ms = jnp.mean(x * x, axis=-1, keepdims=True)
    xn = x * lax.rsqrt(ms + NORM_EPS) * g
    return xn * (1.0 + scale) + shift


def _mod_spec(k, ctx):
    if ctx:
        return pl.BlockSpec((1, 1, 1, D_MODEL), lambda b, *_: (8, k, 0, 0))
    return pl.BlockSpec((1, 1, 1, D_MODEL), lambda b, *_: (b, k, 0, 0))


FFN_TF = 256


def _ffn_kernel(*refs, with_attn, n_cast):
    refs = list(refs)
    h_ref = refs.pop(0)
    if with_attn:
        x_ref = h_ref
        agate_ref, oa_ref, ob_ref, wo_ref = (refs.pop(0) for _ in range(4))
    shift_ref, scale_ref, gate_ref, g_ref, wgu_ref, wd_ref = (refs.pop(0) for _ in range(6))
    cast_in = [refs.pop(0) for _ in range(n_cast)]
    o_ref = refs.pop(0)
    cast_out = [refs.pop(0) for _ in range(n_cast)]
    xm_ref, acc_ref = refs[:2]
    for src, dst in zip(cast_in, cast_out):
        dst[...] = src[...].astype(BF16)
    if with_attn:
        h_ref = refs[2]
        half = oa_ref.shape[-1]
        attn = (jnp.dot(oa_ref[0], wo_ref[0:half, :], preferred_element_type=F32)
                + jnp.dot(ob_ref[0], wo_ref[half:, :], preferred_element_type=F32))
        h_ref[0] = x_ref[0] + agate_ref[0, 0] * attn
    f = wd_ref.shape[0]
    nj = f // FFN_TF
    xm_ref[...] = _modulated_norm(h_ref[0], g_ref[...], shift_ref[0, 0], scale_ref[0, 0]).astype(BF16)

    def act(j):
        xm = xm_ref[...]
        g = jnp.dot(xm, wgu_ref[:, j * FFN_TF:(j + 1) * FFN_TF], preferred_element_type=F32)
        u = jnp.dot(xm, wgu_ref[:, f + j * FFN_TF:f + (j + 1) * FFN_TF], preferred_element_type=F32)
        return (g * jax.nn.sigmoid(g) * u).astype(BF16)

    a = act(0)
    for j in range(nj):
        down = jnp.dot(a, wd_ref[j * FFN_TF:(j + 1) * FFN_TF, :], preferred_element_type=F32)
        if j + 1 < nj:
            a = act(j + 1)
        if j == 0:
            acc_ref[...] = down
        elif j + 1 < nj:
            acc_ref[...] += down
        else:
            o_ref[0] = h_ref[0] + (0.5 * gate_ref[0, 0]) * (acc_ref[...] + down)


def _ffn_weights(w_gu, w_down):
    return w_gu.astype(BF16), w_down.astype(BF16)


def _resident(shape):
    return pl.BlockSpec(shape, lambda *_: (0,) * len(shape), pipeline_mode=pl.Buffered(1))


def _ffn(h, mod4, k0, ctx, g, weights, *, tm, attn=None, casts=()):
    bsz, t, d = h.shape
    wgu, wd = weights
    assert wd.shape[0] % FFN_TF == 0
    n_i = t // tm
    tile = pl.BlockSpec((1, tm, d), lambda b, i: (b, i, 0))
    in_specs, args = [tile], [h]
    scratch = [pltpu.VMEM((tm, d), BF16), pltpu.VMEM((tm, d), F32)]
    if attn is not None:
        k_gate, oa, ob, w_out = attn
        half = oa.shape[-1]
        in_specs += [_mod_spec(k_gate, ctx), pl.BlockSpec((1, tm, half), lambda b, i: (b, i, 0)),
                     pl.BlockSpec((1, tm, half), lambda b, i: (b, i, 0)), _resident(w_out.shape)]
        args += [mod4, oa, ob, w_out]
        scratch.append(pltpu.VMEM((1, tm, d), F32))
    in_specs += [_mod_spec(k0, ctx), _mod_spec(k0 + 1, ctx), _mod_spec(k0 + 2, ctx),
                 pl.BlockSpec((1, d), lambda b, i: (0, 0)), _resident(wgu.shape), _resident(wd.shape)]
    args += [mod4, mod4, mod4, g, wgu, wd]
    slabs = []
    for w in casts:
        rows_per_step = w.shape[0] // (bsz * n_i)
        assert rows_per_step * bsz * n_i == w.shape[0] and rows_per_step % 16 == 0
        slabs.append(pl.BlockSpec((rows_per_step, w.shape[1]), lambda b, i: (b * n_i + i, 0)))
    out = pl.pallas_call(
        functools.partial(_ffn_kernel, with_attn=attn is not None, n_cast=len(casts)),
        grid=(bsz, n_i),
        in_specs=in_specs + slabs,
        out_specs=[tile] + slabs,
        out_shape=[jax.ShapeDtypeStruct(h.shape, F32)] + [jax.ShapeDtypeStruct(w.shape, BF16) for w in casts],
        scratch_shapes=scratch,
        compiler_params=_cparams(("parallel", "parallel")),
        name="ffn_ctx" if ctx else "ffn",
    )(*args, *casts)
    return out if casts else out[0]


def _head_rms_scale(y, bd_ref):
    y2 = (y * y).astype(BF16)
    bd = bd_ref[...]
    w = bd.shape[0]
    ss = jnp.concatenate(
        [jnp.dot(y2[:, c:c + w], bd, preferred_element_type=F32) for c in range(0, y.shape[1], w)], axis=-1)
    return lax.rsqrt(ss * (1.0 / HEAD_DIM) + NORM_EPS)


def _rope(z, cos, sin):
    lane = lax.broadcasted_iota(jnp.int32, z.shape, 1)
    up = pltpu.roll(z, LANES - 16, 1)
    dn = pltpu.roll(z, 16, 1)
    return z * cos + jnp.where((lane & 16) == 0, up, dn) * sin


def _in_proj_kernel(*refs, groups, n_gain):
    h_ref, shift_ref, scale_ref, g_ref, w_ref, bd_ref, cos_ref, sin_ref = refs[:8]
    gain_refs = refs[8:8 + n_gain]
    out_refs = refs[8 + n_gain:]
    xm = _modulated_norm(h_ref[0], g_ref[...], shift_ref[0, 0], scale_ref[0, 0]).astype(BF16)
    for (col, gain_idx, rope, qscale), o_ref in zip(groups, out_refs):
        y = jnp.dot(xm, w_ref[:, col * GROUP_W:(col + 1) * GROUP_W], preferred_element_type=F32)
        if gain_idx is not None:
            gain = gain_refs[gain_idx][...]
            if qscale != 1.0:
                gain = gain * qscale
            y = y * _head_rms_scale(y, bd_ref) * gain
        if rope:
            cos = cos_ref[...]
            sin = sin_ref[...]
            for c in range(GROUP_W // LANES):
                sl = slice(c * LANES, (c + 1) * LANES)
                o_ref[0, :, sl] = _rope(y[:, sl], cos, sin).astype(BF16)
        else:
            o_ref[0] = y.astype(BF16)


def _in_proj(h, mod4, ctx, g, w_in, bd, cos, sin, gains, groups, *, tm):
    bsz, t, d = h.shape
    n_out = len(groups)
    in_specs = [pl.BlockSpec((1, tm, d), lambda b, i: (b, i, 0)),
                _mod_spec(3, ctx), _mod_spec(4, ctx),
                pl.BlockSpec((1, d), lambda b, i: (0, 0)),
                pl.BlockSpec(w_in.shape, lambda b, i: (0, 0)),
                pl.BlockSpec(bd.shape, lambda b, i: (0, 0)),
                pl.BlockSpec((tm, LANES), lambda b, i: (i, 0)),
                pl.BlockSpec((tm, LANES), lambda b, i: (i, 0))]
    in_specs += [pl.BlockSpec((1, GROUP_W), lambda b, i: (0, 0)) for _ in gains]
    return pl.pallas_call(
        functools.partial(_in_proj_kernel, groups=groups, n_gain=len(gains)),
        grid=(bsz, t // tm),
        in_specs=in_specs,
        out_specs=[pl.BlockSpec((1, tm, GROUP_W), lambda b, i: (b, i, 0)) for _ in range(n_out)],
        out_shape=[jax.ShapeDtypeStruct((bsz, t, GROUP_W), BF16) for _ in range(n_out)],
        compiler_params=_cparams(("parallel", "parallel")),
        name="in_proj_ctx" if ctx else "in_proj",
    )(h, mod4, mod4, g, w_in, bd, cos, sin, *gains)


def _split_halves(q):
    qt = q.astype(F32).T
    row = lax.broadcasted_iota(jnp.int32, qt.shape, 0)
    lo = jnp.where(row < HEAD_DIM, qt, 0.0).astype(BF16)
    hi = jnp.where(row >= HEAD_DIM, qt, 0.0).astype(BF16)
    return lo, hi


DIFF_TQ = 256
DIFF_TK = 256
SUBLANES = 8


def _interleave(*stages):
    order = sorted(((k + 0.5) / len(st), n, k) for n, st in enumerate(stages) for k in range(len(st)))
    for _, n, k in order:
        stages[n][k]()


def _diff_pipeline(q_ref, kc_ref, kl_ref, vc_ref, vl_ref, lamv_ref, gcol_ref, o_ref,
                   kk_ref, vte_ref, qm_ref, s_ref, m_ref):
    n_ctx = kc_ref.shape[1]
    n_tick = kk_ref.shape[0] // DIFF_TK
    n_blk = q_ref.shape[1] // DIFF_TQ
    kk_ref[0:n_ctx] = kc_ref[0]
    kk_ref[n_ctx:] = kl_ref[0]
    for kc in range(n_tick):
        lo = kc * DIFF_TK
        v = vc_ref[0, lo:lo + DIFF_TK] if lo < n_ctx else vl_ref[0, lo - n_ctx:lo - n_ctx + DIFF_TK]
        vte_ref[kc] = v.astype(F32).T.astype(BF16)
    lv = lamv_ref[...]
    lam = (jnp.exp(jnp.sum(lv[0:1] * lv[1:2], axis=-1, keepdims=True))
           - jnp.exp(jnp.sum(lv[2:3] * lv[3:4], axis=-1, keepdims=True)) + LAM_INIT)

    def rows(i):
        return pl.ds(pl.multiple_of(i * DIFF_TQ, DIFF_TQ), DIFF_TQ)

    def keys(kc):
        return slice(kc * DIFF_TK, (kc + 1) * DIFF_TK)

    def prep(i, slot):
        lo, hi = _split_halves(q_ref[0, rows(i), :])
        qm_ref[slot, 0] = lo
        qm_ref[slot, 1] = hi
        m_ref[slot] = jnp.full(m_ref.shape[1:], -jnp.inf, F32)

    def score_tick(slot, kc):
        k = kk_ref[keys(kc), :]
        for c in range(2):
            s = jnp.dot(k, qm_ref[slot, c], preferred_element_type=F32)
            s_ref[slot, c, keys(kc), :] = s
            part = jnp.max(s.reshape(DIFF_TK // SUBLANES, SUBLANES, DIFF_TQ), axis=0)
            m_ref[slot, c] = jnp.maximum(m_ref[slot, c], part)

    def final_max(slot):
        return [jnp.broadcast_to(jnp.max(m_ref[slot, c], axis=0, keepdims=True), (SUBLANES, DIFF_TQ))
                for c in range(2)]

    def pv_tick(slot, kc, mx, acc):
        for c in range(2):
            s = s_ref[slot, c, keys(kc), :].reshape(DIFF_TK // SUBLANES, SUBLANES, DIFF_TQ)
            p = jnp.exp2(s - mx[c])
            d = [jnp.dot(vte_ref[kc], p.reshape(DIFF_TK, DIFF_TQ).astype(BF16), preferred_element_type=F32),
                 jnp.sum(p, axis=0)]
            acc[c] = d if acc[c] is None else [a + b for a, b in zip(acc[c], d)]

    def finish(i, acc):
        l0 = jnp.sum(acc[0][1], axis=0, keepdims=True)
        l1 = jnp.sum(acc[1][1], axis=0, keepdims=True)
        ot = acc[0][0] * (1.0 / l0) - acc[1][0] * (lam / l1)
        ms = jnp.mean(ot * ot, axis=0, keepdims=True)
        on = ot * lax.rsqrt(ms + NORM_EPS) * (gcol_ref[...] * (1.0 - LAM_INIT))
        o_ref[0, rows(i), :] = on.T.astype(BF16)

    def start():
        return ([functools.partial(prep, 0, 0)]
                + [functools.partial(score_tick, 0, kc) for kc in range(n_tick)])

    def block(i, slot, last=False):
        state = {}

        def head():
            state["mx"] = final_max(slot)
            state["acc"] = [None, None]
            if not last:
                prep(i + 1, 1 - slot)

        def tick(kc):
            pv_tick(slot, kc, state["mx"], state["acc"])
            if not last:
                score_tick(1 - slot, kc)

        return ([head] + [functools.partial(tick, kc) for kc in range(n_tick)]
                + [lambda: finish(i, state["acc"])])

    return start, block


def _diff_scratch(c, s):
    return [pltpu.VMEM((c + s, LANES), BF16),
            pltpu.VMEM(((c + s) // DIFF_TK, LANES, DIFF_TK), BF16),
            pltpu.VMEM((2, 2, LANES, DIFF_TQ), BF16),
            pltpu.VMEM((2, 2, c + s, DIFF_TQ), F32),
            pltpu.VMEM((2, 2, SUBLANES, DIFF_TQ), F32)]


NA_QROWS = 4
NA_KROWS = 12
NA_TQ = NA_QROWS * GRID_W
NA_TK = 4 * GRID_W
NA_WIN_TICKS = NA_KROWS * GRID_W // NA_TK


def _na_variants(rows):
    n_rb = rows // NA_QROWS
    out = []
    for rb in (0, 1, n_rb - 1):
        ws = min(max(rb * NA_QROWS - NA_KH // 2, 0), rows - NA_KROWS)
        out.append((rb * NA_QROWS, ws))
    return out


def _na_build_bias(rpb_ref, bias_ref, rows):
    shape = (GRID_W, LANES)
    kc = lax.broadcasted_iota(jnp.int32, shape, 0)
    lane = lax.broadcasted_iota(jnp.int32, shape, 1)
    qc = lane & (GRID_W - 1)
    cs = jnp.clip(qc - NA_KW // 2, 0, GRID_W - NA_KW)
    col_ok = (kc >= cs) & (kc < cs + NA_KW)
    side_ok = {(True, True): col_ok, (True, False): col_ok & (lane < GRID_W),
               (False, True): col_ok & (lane >= GRID_W)}
    neg = jnp.full(shape, NEG_INF, F32)
    for hh in range(2):
        toeplitz = {}
        for v, (r0, ws) in enumerate(_na_variants(rows)):
            for i in range(NA_KROWS):
                kr = ws + i
                for jp in range(NA_QROWS // 2):
                    ok = []
                    for r in (r0 + 2 * jp, r0 + 2 * jp + 1):
                        rs = min(max(r - NA_KH // 2, 0), rows - NA_KH)
                        ok.append(rs <= kr < rs + NA_KH)
                    tile = neg
                    if any(ok):
                        e = kr - (r0 + 2 * jp) + NA_KH - 1
                        if e not in toeplitz:
                            vec = jnp.broadcast_to(rpb_ref[0, hh, e:e + 1, :] * LOG2E, shape)
                            toeplitz[e] = pltpu.roll(vec, LANES - (NA_KW - 1), 1, stride=1, stride_axis=0)
                        tile = jnp.where(side_ok[tuple(ok)], toeplitz[e], neg)
                    bias_ref[hh, v, i * GRID_W:(i + 1) * GRID_W, jp * LANES:(jp + 1) * LANES] = tile


def _na_pipeline(q_ref, kc_ref, k_ref, vc_ref, v_ref, rpb_ref, o_ref,
                 bias_ref, vte_ref, qm_ref, s_ref, m_ref, *, rows):
    n_blk = rows // NA_QROWS
    n_lat = v_ref.shape[1] // NA_TK
    assert kc_ref.shape[1] == NA_TK
    n_tick = NA_WIN_TICKS + 1

    @pl.when(pl.program_id(1) == 0)
    def _():
        _na_build_bias(rpb_ref, bias_ref, rows)

    for ch in range(n_lat + 1):
        v = v_ref[0, ch * NA_TK:(ch + 1) * NA_TK] if ch < n_lat else vc_ref[0]
        vt = v.astype(F32).T
        for hh in range(2):
            vte_ref[hh, ch] = vt[hh * HEAD_DIM:(hh + 1) * HEAD_DIM].astype(BF16)

    def rows_of(i):
        return pl.ds(pl.multiple_of(i * NA_TQ, NA_TQ), NA_TQ)

    def window(i):
        return (jnp.clip(i - 1, 0, n_lat - NA_WIN_TICKS),
                jnp.where(i == 0, 0, jnp.where(i == n_blk - 1, 2, 1)))

    def prep(i, slot):
        lo, hi = _split_halves(q_ref[0, rows_of(i), :])
        qm_ref[slot, 0] = lo
        qm_ref[slot, 1] = hi
        m_ref[slot] = jnp.full(m_ref.shape[1:], -jnp.inf, F32)

    def score_tick(slot, i, t):
        ch0, variant = window(i)
        if t < NA_WIN_TICKS:
            k = k_ref[0, pl.ds(pl.multiple_of((ch0 + t) * NA_TK, NA_TK), NA_TK), :]
        else:
            k = kc_ref[0]
        for hh in range(2):
            s = jnp.dot(k, qm_ref[slot, hh], preferred_element_type=F32)
            if t < NA_WIN_TICKS:
                s = s + bias_ref[hh, variant, t * NA_TK:(t + 1) * NA_TK, :]
            s_ref[slot, hh, t * NA_TK:(t + 1) * NA_TK, :] = s
            part = jnp.max(s.reshape(NA_TK // SUBLANES, SUBLANES, NA_TQ), axis=0)
            m_ref[slot, hh] = jnp.maximum(m_ref[slot, hh], part)

    def final_max(slot):
        return [jnp.broadcast_to(jnp.max(m_ref[slot, hh], axis=0, keepdims=True), (SUBLANES, NA_TQ))
                for hh in range(2)]

    def pv_tick(slot, i, t, mx, acc):
        ch = window(i)[0] + t if t < NA_WIN_TICKS else n_lat
        for hh in range(2):
            s = s_ref[slot, hh, t * NA_TK:(t + 1) * NA_TK, :].reshape(NA_TK // SUBLANES, SUBLANES, NA_TQ)
            p = jnp.exp2(s - mx[hh])
            d = [jnp.dot(vte_ref[hh, ch], p.reshape(NA_TK, NA_TQ).astype(BF16), preferred_element_type=F32),
                 jnp.sum(p, axis=0)]
            acc[hh] = d if acc[hh] is None else [a + b for a, b in zip(acc[hh], d)]

    def finish(i, acc):
        ot = jnp.concatenate(
            [acc[hh][0] * (1.0 / jnp.sum(acc[hh][1], axis=0, keepdims=True)) for hh in range(2)],
            axis=0)
        o_ref[0, rows_of(i), :] = ot.T.astype(BF16)

    def start():
        return ([functools.partial(prep, 0, 0)]
                + [functools.partial(score_tick, 0, 0, t) for t in range(n_tick)])

    def block(i, slot, last=False):
        state = {}

        def head():
            state["mx"] = final_max(slot)
            state["acc"] = [None, None]
            if not last:
                prep(i + 1, 1 - slot)

        def tick(t):
            pv_tick(slot, i, t, state["mx"], state["acc"])
            if not last:
                score_tick(1 - slot, i + 1, t)

        return ([head] + [functools.partial(tick, t) for t in range(n_tick)]
                + [lambda: finish(i, state["acc"])])

    return start, block


def _na_scratch(n_lat):
    return [pltpu.VMEM((2, 3, NA_KROWS * GRID_W, NA_TQ), F32),
            pltpu.VMEM((2, n_lat + 1, HEAD_DIM, NA_TK), BF16),
            pltpu.VMEM((2, 2, LANES, NA_TQ), BF16),
            pltpu.VMEM((2, 2, (NA_WIN_TICKS + 1) * NA_TK, NA_TQ), F32),
            pltpu.VMEM((2, 2, SUBLANES, NA_TQ), F32)]


N_DIFF_IN, N_NA_IN = 7, 6


def _attn_kernel(*refs, rows):
    d_in = refs[:N_DIFF_IN]
    n_in = refs[N_DIFF_IN:N_DIFF_IN + N_NA_IN]
    oa_ref, ob_ref = refs[N_DIFF_IN + N_NA_IN:N_DIFF_IN + N_NA_IN + 2]
    scratch = refs[N_DIFF_IN + N_NA_IN + 2:]
    d_start, d_block = _diff_pipeline(*d_in, oa_ref, *scratch[:5])
    n_start, n_block = _na_pipeline(*n_in, ob_ref, *scratch[5:], rows=rows)
    n_blk = rows // NA_QROWS
    _interleave(d_start(), n_start())

    def pair(u, carry):
        for j in range(2):
            _interleave(d_block(2 * u + j, j), n_block(2 * u + j, j))
        return carry

    lax.fori_loop(0, n_blk // 2 - 1, pair, 0)
    _interleave(d_block(n_blk - 2, 0), n_block(n_blk - 2, 0))
    _interleave(d_block(n_blk - 1, 1, last=True), n_block(n_blk - 1, 1, last=True))


def _attention(aq, ak, av, akc, avc, lamv, gout, nq, nk, nv, nkc, nvc, rpb):
    bsz, s, _ = aq.shape
    c = akc.shape[1]
    rows = s // GRID_W
    n_blk = rows // NA_QROWS
    assert NA_HEADS // 2 == DIFF_HEADS and NA_TQ == DIFF_TQ and NA_QROWS * GRID_W == NA_TK and c == NA_TK
    assert n_blk >= 4 and n_blk % 2 == 0 and c % DIFF_TK == 0 and s % DIFF_TK == 0
    n_r, n_c = rpb.shape[1:]
    padded = jnp.pad(rpb[:, :, ::-1], ((0, 0), (1, 1), (0, GRID_W - n_c)))
    rpb2 = jnp.concatenate([padded[:, 1:], padded[:, :-1]], axis=-1).reshape(DIFF_HEADS, 2, n_r + 1, LANES)
    lat = pl.BlockSpec((1, s, LANES), lambda h, b: (b, 0, h))
    ctx = pl.BlockSpec((1, c, LANES), lambda h, b: (b, 0, h))
    return pl.pallas_call(
        functools.partial(_attn_kernel, rows=rows),
        grid=(DIFF_HEADS, bsz),
        in_specs=[lat, ctx, lat, ctx, lat,
                  pl.BlockSpec(lamv.shape, lambda h, b: (0, 0)),
                  pl.BlockSpec((LANES, 1), lambda h, b: (0, 0)),
                  lat, ctx, lat, ctx, lat,
                  pl.BlockSpec((1,) + rpb2.shape[1:], lambda h, b: (h, 0, 0, 0))],
        out_specs=[lat, lat],
        out_shape=[jax.ShapeDtypeStruct(aq.shape, BF16), jax.ShapeDtypeStruct(nq.shape, BF16)],
        scratch_shapes=_diff_scratch(c, s) + _na_scratch(s // NA_TK),
        compiler_params=_cparams(("arbitrary", "arbitrary")),
        name="attention",
    )(aq, akc, ak, avc, av, lamv, gout.reshape(LANES, 1), nq, nkc, nk, nvc, nv, rpb2)


def _rope_tables(n_tokens):
    t = jnp.arange(n_tokens, dtype=jnp.int32)
    row = (t // GRID_W).astype(F32)
    col = (t % GRID_W).astype(F32)
    n_freq = HEAD_DIM // 4
    inv_freq = ROPE_BASE ** (-jnp.arange(n_freq, dtype=F32) / n_freq)
    ar = row[:, None] * inv_freq
    ac = col[:, None] * inv_freq
    cos = jnp.concatenate([jnp.cos(ar), jnp.cos(ar), jnp.cos(ac), jnp.cos(ac)], axis=-1)
    sin = jnp.concatenate([-jnp.sin(ar), jnp.sin(ar), -jnp.sin(ac), jnp.sin(ac)], axis=-1)
    reps = LANES // HEAD_DIM
    return jnp.tile(cos, (1, reps)), jnp.tile(sin, (1, reps))


def kernel(x, c, ctx, c_ctx, w_ada, b_ada, norm1, norm2, norm3, ffn1_w_gu, ffn1_w_down, w_in,
           diff_q_norm, diff_k_norm, lam_q1, lam_k1, lam_q2, lam_k2, diff_out_norm,
           na_q_norm, na_k_norm, na_rpb, w_out, ffn2_w_gu, ffn2_w_down):
    bsz, s, d = x.shape
    assert w_ada.shape[0] == 1 and d == D_MODEL and s % (NA_QROWS * GRID_W) == 0
    rows = s // GRID_W

    cond = jnp.zeros((16, d), F32).at[:bsz].set(c).at[bsz].set(c_ctx)
    mod4 = _adaln(cond, w_ada[0], b_ada).reshape(16, N_MOD, 1, d)

    ffn1_w = _ffn_weights(ffn1_w_gu[0], ffn1_w_down[0])
    h, w_in_b, w_out_b, w_gu2, w_dn2 = _ffn(x, mod4, 0, False, norm1, ffn1_w, tm=1024,
                                            casts=(w_in[0], w_out[0], ffn2_w_gu[0], ffn2_w_down[0]))
    n_ctx = ctx.shape[1]
    hc = _ffn(ctx.reshape(1, bsz * n_ctx, d), mod4, 0, True, norm1, ffn1_w, tm=1024)

    seg = np.arange(MXU_W) // HEAD_DIM
    bd = jnp.asarray(seg[:, None] == seg[None, :], BF16)
    cos, sin = _rope_tables(s)
    reps = GROUP_W // HEAD_DIM
    gains = [jnp.tile(g, (1, reps)) for g in (diff_q_norm, diff_k_norm, na_q_norm, na_k_norm)]
    qs = HEAD_DIM ** -0.5
    lat_groups = ((0, 0, True, qs * LOG2E), (1, 1, True, 1.0), (2, None, False, 1.0),
                  (3, 2, False, qs * LOG2E), (4, 3, False, 1.0), (5, None, False, 1.0))
    aq, ak, av, nq, nk, nv = _in_proj(h, mod4, False, norm2, w_in_b, bd, cos, sin, gains,
                                      lat_groups, tm=512)
    ctx_groups = ((1, 1, False, 1.0), (2, None, False, 1.0), (4, 3, False, 1.0), (5, None, False, 1.0))
    ctx_kv = _in_proj(hc, mod4, True, norm2, w_in_b, bd, cos, sin, gains, ctx_groups, tm=512)
    akc, avc, nkc, nvc = (a.reshape(bsz, n_ctx, GROUP_W) for a in ctx_kv)

    lamv = jnp.concatenate([lam_q1, lam_k1, lam_q2, lam_k2], axis=0)
    oa, ob = _attention(aq, ak, av, akc, avc, lamv, diff_out_norm, nq, nk, nv, nkc, nvc, na_rpb[0])

    return _ffn(h, mod4, 6, False, norm3, (w_gu2, w_dn2), tm=1024, attn=(5, oa, ob, w_out_b))
```

```python
import functools
import math

import numpy as np
import jax
import jax.numpy as jnp
from jax import lax
from jax.experimental import pallas as pl
from jax.experimental.pallas import tpu as pltpu

D_MODEL = 1024
GRID_W = 64
HEAD_DIM = 64
DIFF_HEADS = 4
NA_HEADS = 8
GROUP_W = 512
D_FF = 2816
N_MOD = 9
NA_KH = 8
NA_KW = 16
ROPE_BASE = 10000.0
NORM_EPS = 1e-6
NEG_INF = -1e30
LAM_INIT = 0.8 - 0.6 * math.exp(-0.3 * 0)
LOG2E = math.log2(math.e)

LANES = 128
MXU_W = 256
VMEM_LIMIT = 56 * 1024 * 1024

F32 = jnp.float32
BF16 = jnp.bfloat16


def _cparams(sem):
    return pltpu.CompilerParams(dimension_semantics=sem, vmem_limit_bytes=VMEM_LIMIT)


ADALN_TK = 128


def _adaln_kernel(cond_ref, w_ref, b_ref, o_ref):
    k = pl.program_id(0)
    c = cond_ref[...]
    a = (c * jax.nn.sigmoid(c)).astype(BF16)
    part = jnp.dot(a, w_ref[...].astype(BF16), preferred_element_type=F32)

    @pl.when(k == 0)
    def _():
        o_ref[...] = part + b_ref[...]

    @pl.when(k > 0)
    def _():
        o_ref[...] += part


def _adaln(cond, w_ada, b_ada):
    rows, d = cond.shape
    n = w_ada.shape[1]
    return pl.pallas_call(
        _adaln_kernel,
        grid=(d // ADALN_TK,),
        in_specs=[pl.BlockSpec((rows, ADALN_TK), lambda k: (0, k)),
                  pl.BlockSpec((ADALN_TK, n), lambda k: (k, 0)),
                  pl.BlockSpec((1, n), lambda k: (0, 0))],
        out_specs=pl.BlockSpec((rows, n), lambda k: (0, 0)),
        out_shape=jax.ShapeDtypeStruct((rows, n), F32),
        compiler_params=_cparams(("arbitrary",)),
        name="adaln",
    )(cond, w_ada, b_ada)


def _modulated_norm(x, g, shift, scale):
    ms = jnp.mean(x * x, axis=-1, keepdims=True)
    xn = x * lax.rsqrt(ms + NORM_EPS) * g
    return xn * (1.0 + scale) + shift


def _mod_spec(k, ctx):
    if ctx:
        return pl.BlockSpec((1, 1, 1, D_MODEL), lambda b, *_: (8, k, 0, 0))
    return pl.BlockSpec((1, 1, 1, D_MODEL), lambda b, *_: (b, k, 0, 0))


FFN_TF = 256


def _ffn_kernel(*refs, with_attn, n_cast):
    refs = list(refs)
    h_ref = refs.pop(0)
    if with_attn:
        x_ref = h_ref
        agate_ref, oa_ref, ob_ref, wo_ref = (refs.pop(0) for _ in range(4))
    shift_ref, scale_ref, gate_ref, g_ref, wgu_ref, wd_ref = (refs.pop(0) for _ in range(6))
    cast_in = [refs.pop(0) for _ in range(n_cast)]
    o_ref = refs.pop(0)
    cast_out = [refs.pop(0) for _ in range(n_cast)]
    xm_ref, acc_ref = refs[:2]
    for src, dst in zip(cast_in, cast_out):
        dst[...] = src[...].astype(BF16)
    if with_attn:
        h_ref = refs[2]
        half = oa_ref.shape[-1]
        attn = (jnp.dot(oa_ref[0], wo_ref[0:half, :], preferred_element_type=F32)
                + jnp.dot(ob_ref[0], wo_ref[half:, :], preferred_element_type=F32))
        h_ref[0] = x_ref[0] + agate_ref[0, 0] * attn
    f = wd_ref.shape[0]
    nj = f // FFN_TF
    xm_ref[...] = _modulated_norm(h_ref[0], g_ref[...], shift_ref[0, 0], scale_ref[0, 0]).astype(BF16)

    def act(j):
        xm = xm_ref[...]
        g = jnp.dot(xm, wgu_ref[:, j * FFN_TF:(j + 1) * FFN_TF], preferred_element_type=F32)
        u = jnp.dot(xm, wgu_ref[:, f + j * FFN_TF:f + (j + 1) * FFN_TF], preferred_element_type=F32)
        return (g * jax.nn.sigmoid(g) * u).astype(BF16)

    a = act(0)
    for j in range(nj):
        down = jnp.dot(a, wd_ref[j * FFN_TF:(j + 1) * FFN_TF, :], preferred_element_type=F32)
        if j + 1 < nj:
            a = act(j + 1)
        if j == 0:
            acc_ref[...] = down
        elif j + 1 < nj:
            acc_ref[...] += down
        else:
            o_ref[0] = h_ref[0] + (0.5 * gate_ref[0, 0]) * (acc_ref[...] + down)


def _ffn_weights(w_gu, w_down):
    return w_gu.astype(BF16), w_down.astype(BF16)


def _resident(shape):
    return pl.BlockSpec(shape, lambda *_: (0,) * len(shape), pipeline_mode=pl.Buffered(1))


def _ffn(h, mod4, k0, ctx, g, weights, *, tm, attn=None, casts=()):
    bsz, t, d = h.shape
    wgu, wd = weights
    assert wd.shape[0] % FFN_TF == 0
    n_i = t // tm
    tile = pl.BlockSpec((1, tm, d), lambda b, i: (b, i, 0))
    in_specs, args = [tile], [h]
    scratch = [pltpu.VMEM((tm, d), BF16), pltpu.VMEM((tm, d), F32)]
    if attn is not None:
        k_gate, oa, ob, w_out = attn
        half = oa.shape[-1]
        in_specs += [_mod_spec(k_gate, ctx), pl.BlockSpec((1, tm, half), lambda b, i: (b, i, 0)),
                     pl.BlockSpec((1, tm, half), lambda b, i: (b, i, 0)), _resident(w_out.shape)]
        args += [mod4, oa, ob, w_out]
        scratch.append(pltpu.VMEM((1, tm, d), F32))
    in_specs += [_mod_spec(k0, ctx), _mod_spec(k0 + 1, ctx), _mod_spec(k0 + 2, ctx),
                 pl.BlockSpec((1, d), lambda b, i: (0, 0)), _resident(wgu.shape), _resident(wd.shape)]
    args += [mod4, mod4, mod4, g, wgu, wd]
    slabs = []
    for w in casts:
        rows_per_step = w.shape[0] // (bsz * n_i)
        assert rows_per_step * bsz * n_i == w.shape[0] and rows_per_step % 16 == 0
        slabs.append(pl.BlockSpec((rows_per_step, w.shape[1]), lambda b, i: (b * n_i + i, 0)))
    out = pl.pallas_call(
        functools.partial(_ffn_kernel, with_attn=attn is not None, n_cast=len(casts)),
        grid=(bsz, n_i),
        in_specs=in_specs + slabs,
        out_specs=[tile] + slabs,
        out_shape=[jax.ShapeDtypeStruct(h.shape, F32)] + [jax.ShapeDtypeStruct(w.shape, BF16) for w in casts],
        scratch_shapes=scratch,
        compiler_params=_cparams(("parallel", "parallel")),
        name="ffn_ctx" if ctx else "ffn",
    )(*args, *casts)
    return out if casts else out[0]


def _head_rms_scale(y, bd_ref):
    y2 = (y * y).astype(BF16)
    bd = bd_ref[...]
    w = bd.shape[0]
    ss = jnp.concatenate(
        [jnp.dot(y2[:, c:c + w], bd, preferred_element_type=F32) for c in range(0, y.shape[1], w)], axis=-1)
    return lax.rsqrt(ss * (1.0 / HEAD_DIM) + NORM_EPS)


def _rope(z, cos, sin):
    lane = lax.broadcasted_iota(jnp.int32, z.shape, 1)
    up = pltpu.roll(z, LANES - 16, 1)
    dn = pltpu.roll(z, 16, 1)
    return z * cos + jnp.where((lane & 16) == 0, up, dn) * sin


def _in_proj_kernel(*refs, groups, n_gain):
    h_ref, shift_ref, scale_ref, g_ref, w_ref, bd_ref, cos_ref, sin_ref = refs[:8]
    gain_refs = refs[8:8 + n_gain]
    out_refs = refs[8 + n_gain:]
    xm = _modulated_norm(h_ref[0], g_ref[...], shift_ref[0, 0], scale_ref[0, 0]).astype(BF16)
    for (col, gain_idx, rope, qscale), o_ref in zip(groups, out_refs):
        y = jnp.dot(xm, w_ref[:, col * GROUP_W:(col + 1) * GROUP_W], preferred_element_type=F32)
        if gain_idx is not None:
            gain = gain_refs[gain_idx][...]
            if qscale != 1.0:
                gain = gain * qscale
            y = y * _head_rms_scale(y, bd_ref) * gain
        if rope:
            cos = cos_ref[...]
            sin = sin_ref[...]
            for c in range(GROUP_W // LANES):
                sl = slice(c * LANES, (c + 1) * LANES)
                o_ref[0, :, sl] = _rope(y[:, sl], cos, sin).astype(BF16)
        else:
            o_ref[0] = y.astype(BF16)


def _in_proj(h, mod4, ctx, g, w_in, bd, cos, sin, gains, groups, *, tm):
    bsz, t, d = h.shape
    n_out = len(groups)
    in_specs = [pl.BlockSpec((1, tm, d), lambda b, i: (b, i, 0)),
                _mod_spec(3, ctx), _mod_spec(4, ctx),
                pl.BlockSpec((1, d), lambda b, i: (0, 0)),
                pl.BlockSpec(w_in.shape, lambda b, i: (0, 0)),
                pl.BlockSpec(bd.shape, lambda b, i: (0, 0)),
                pl.BlockSpec((tm, LANES), lambda b, i: (i, 0)),
                pl.BlockSpec((tm, LANES), lambda b, i: (i, 0))]
    in_specs += [pl.BlockSpec((1, GROUP_W), lambda b, i: (0, 0)) for _ in gains]
    return pl.pallas_call(
        functools.partial(_in_proj_kernel, groups=groups, n_gain=len(gains)),
        grid=(bsz, t // tm),
        in_specs=in_specs,
        out_specs=[pl.BlockSpec((1, tm, GROUP_W), lambda b, i: (b, i, 0)) for _ in range(n_out)],
        out_shape=[jax.ShapeDtypeStruct((bsz, t, GROUP_W), BF16) for _ in range(n_out)],
        compiler_params=_cparams(("parallel", "parallel")),
        name="in_proj_ctx" if ctx else "in_proj",
    )(h, mod4, mod4, g, w_in, bd, cos, sin, *gains)


def _split_halves(q):
    qt = q.astype(F32).T
    row = lax.broadcasted_iota(jnp.int32, qt.shape, 0)
    lo = jnp.where(row < HEAD_DIM, qt, 0.0).astype(BF16)
    hi = jnp.where(row >= HEAD_DIM, qt, 0.0).astype(BF16)
    return lo, hi


DIFF_TQ = 256
DIFF_TK = 256
SUBLANES = 8


def _interleave(*stages):
    order = sorted(((k + 0.5) / len(st), n, k) for n, st in enumerate(stages) for k in range(len(st)))
    for _, n, k in order:
        stages[n][k]()


def _diff_pipeline(q_ref, kc_ref, kl_ref, vc_ref, vl_ref, lamv_ref, gcol_ref, o_ref,
                   kk_ref, vte_ref, qm_ref, s_ref, m_ref):
    n_ctx = kc_ref.shape[1]
    n_tick = kk_ref.shape[0] // DIFF_TK
    n_blk = q_ref.shape[1] // DIFF_TQ
    kk_ref[0:n_ctx] = kc_ref[0]
    kk_ref[n_ctx:] = kl_ref[0]
    for kc in range(n_tick):
        lo = kc * DIFF_TK
        v = vc_ref[0, lo:lo + DIFF_TK] if lo < n_ctx else vl_ref[0, lo - n_ctx:lo - n_ctx + DIFF_TK]
        vte_ref[kc] = v.astype(F32).T.astype(BF16)
    lv = lamv_ref[...]
    lam = (jnp.exp(jnp.sum(lv[0:1] * lv[1:2], axis=-1, keepdims=True))
           - jnp.exp(jnp.sum(lv[2:3] * lv[3:4], axis=-1, keepdims=True)) + LAM_INIT)

    def rows(i):
        return pl.ds(pl.multiple_of(i * DIFF_TQ, DIFF_TQ), DIFF_TQ)

    def keys(kc):
        return slice(kc * DIFF_TK, (kc + 1) * DIFF_TK)

    def prep(i, slot):
        lo, hi = _split_halves(q_ref[0, rows(i), :])
        qm_ref[slot, 0] = lo
        qm_ref[slot, 1] = hi
        m_ref[slot] = jnp.full(m_ref.shape[1:], -jnp.inf, F32)

    def score_tick(slot, kc):
        k = kk_ref[keys(kc), :]
        for c in range(2):
            s = jnp.dot(k, qm_ref[slot, c], preferred_element_type=F32)
            s_ref[slot, c, keys(kc), :] = s
            part = jnp.max(s.reshape(DIFF_TK // SUBLANES, SUBLANES, DIFF_TQ), axis=0)
            m_ref[slot, c] = jnp.maximum(m_ref[slot, c], part)

    def final_max(slot):
        return [jnp.broadcast_to(jnp.max(m_ref[slot, c], axis=0, keepdims=True), (SUBLANES, DIFF_TQ))
                for c in range(2)]

    def pv_tick(slot, kc, mx, acc):
        for c in range(2):
            s = s_ref[slot, c, keys(kc), :].reshape(DIFF_TK // SUBLANES, SUBLANES, DIFF_TQ)
            p = jnp.exp2(s - mx[c])
            d = [jnp.dot(vte_ref[kc], p.reshape(DIFF_TK, DIFF_TQ).astype(BF16), preferred_element_type=F32),
                 jnp.sum(p, axis=0)]
            acc[c] = d if acc[c] is None else [a + b for a, b in zip(acc[c], d)]

    def finish(i, acc):
        l0 = jnp.sum(acc[0][1], axis=0, keepdims=True)
        l1 = jnp.sum(acc[1][1], axis=0, keepdims=True)
        ot = acc[0][0] * (1.0 / l0) - acc[1][0] * (lam / l1)
        ms = jnp.mean(ot * ot, axis=0, keepdims=True)
        on = ot * lax.rsqrt(ms + NORM_EPS) * (gcol_ref[...] * (1.0 - LAM_INIT))
        o_ref[0, rows(i), :] = on.T.astype(BF16)

    def start():
        return ([functools.partial(prep, 0, 0)]
                + [functools.partial(score_tick, 0, kc) for kc in range(n_tick)])

    def block(i, slot, last=False):
        state = {}

        def head():
            state["mx"] = final_max(slot)
            state["acc"] = [None, None]
            if not last:
                prep(i + 1, 1 - slot)

        def tick(kc):
            pv_tick(slot, kc, state["mx"], state["acc"])
            if not last:
                score_tick(1 - slot, kc)

        return ([head] + [functools.partial(tick, kc) for kc in range(n_tick)]
                + [lambda: finish(i, state["acc"])])

    return start, block


def _diff_scratch(c, s):
    return [pltpu.VMEM((c + s, LANES), BF16),
            pltpu.VMEM(((c + s) // DIFF_TK, LANES, DIFF_TK), BF16),
            pltpu.VMEM((2, 2, LANES, DIFF_TQ), BF16),
            pltpu.VMEM((2, 2, c + s, DIFF_TQ), F32),
            pltpu.VMEM((2, 2, SUBLANES, DIFF_TQ), F32)]


NA_QROWS = 4
NA_KROWS = 12
NA_TQ = NA_QROWS * GRID_W
NA_TK = 4 * GRID_W
NA_WIN_TICKS = NA_KROWS * GRID_W // NA_TK


def _na_variants(rows):
    n_rb = rows // NA_QROWS
    out = []
    for rb in (0, 1, n_rb - 1):
        ws = min(max(rb * NA_QROWS - NA_KH // 2, 0), rows - NA_KROWS)
        out.append((rb * NA_QROWS, ws))
    return out


def _na_build_bias(rpb_ref, bias_ref, rows):
    shape = (GRID_W, LANES)
    kc = lax.broadcasted_iota(jnp.int32, shape, 0)
    lane = lax.broadcasted_iota(jnp.int32, shape, 1)
    qc = lane & (GRID_W - 1)
    cs = jnp.clip(qc - NA_KW // 2, 0, GRID_W - NA_KW)
    col_ok = (kc >= cs) & (kc < cs + NA_KW)
    side_ok = {(True, True): col_ok, (True, False): col_ok & (lane < GRID_W),
               (False, True): col_ok & (lane >= GRID_W)}
    neg = jnp.full(shape, NEG_INF, F32)
    for hh in range(2):
        toeplitz = {}
        for v, (r0, ws) in enumerate(_na_variants(rows)):
            for i in range(NA_KROWS):
                kr = ws + i
                for jp in range(NA_QROWS // 2):
                    ok = []
                    for r in (r0 + 2 * jp, r0 + 2 * jp + 1):
                        rs = min(max(r - NA_KH // 2, 0), rows - NA_KH)
                        ok.append(rs <= kr < rs + NA_KH)
                    tile = neg
                    if any(ok):
                        e = kr - (r0 + 2 * jp) + NA_KH - 1
                        if e not in toeplitz:
                            vec = jnp.broadcast_to(rpb_ref[0, hh, e:e + 1, :] * LOG2E, shape)
                            toeplitz[e] = pltpu.roll(vec, LANES - (NA_KW - 1), 1, stride=1, stride_axis=0)
                        tile = jnp.where(side_ok[tuple(ok)], toeplitz[e], neg)
                    bias_ref[hh, v, i * GRID_W:(i + 1) * GRID_W, jp * LANES:(jp + 1) * LANES] = tile


def _na_pipeline(q_ref, kc_ref, k_ref, vc_ref, v_ref, rpb_ref, o_ref,
                 bias_ref, vte_ref, qm_ref, s_ref, m_ref, *, rows):
    n_blk = rows // NA_QROWS
    n_lat = v_ref.shape[1] // NA_TK
    assert kc_ref.shape[1] == NA_TK
    n_tick = NA_WIN_TICKS + 1

    @pl.when(pl.program_id(1) == 0)
    def _():
        _na_build_bias(rpb_ref, bias_ref, rows)

    for ch in range(n_lat + 1):
        v = v_ref[0, ch * NA_TK:(ch + 1) * NA_TK] if ch < n_lat else vc_ref[0]
        vt = v.astype(F32).T
        for hh in range(2):
            vte_ref[hh, ch] = vt[hh * HEAD_DIM:(hh + 1) * HEAD_DIM].astype(BF16)

    def rows_of(i):
        return pl.ds(pl.multiple_of(i * NA_TQ, NA_TQ), NA_TQ)

    def window(i):
        return (jnp.clip(i - 1, 0, n_lat - NA_WIN_TICKS),
                jnp.where(i == 0, 0, jnp.where(i == n_blk - 1, 2, 1)))

    def prep(i, slot):
        lo, hi = _split_halves(q_ref[0, rows_of(i), :])
        qm_ref[slot, 0] = lo
        qm_ref[slot, 1] = hi
        m_ref[slot] = jnp.full(m_ref.shape[1:], -jnp.inf, F32)

    def score_tick(slot, i, t):
        ch0, variant = window(i)
        if t < NA_WIN_TICKS:
            k = k_ref[0, pl.ds(pl.multiple_of((ch0 + t) * NA_TK, NA_TK), NA_TK), :]
        else:
            k = kc_ref[0]
        for hh in range(2):
            s = jnp.dot(k, qm_ref[slot, hh], preferred_element_type=F32)
            if t < NA_WIN_TICKS:
                s = s + bias_ref[hh, variant, t * NA_TK:(t + 1) * NA_TK, :]
            s_ref[slot, hh, t * NA_TK:(t + 1) * NA_TK, :] = s
            part = jnp.max(s.reshape(NA_TK // SUBLANES, SUBLANES, NA_TQ), axis=0)
            m_ref[slot, hh] = jnp.maximum(m_ref[slot, hh], part)

    def final_max(slot):
        return [jnp.broadcast_to(jnp.max(m_ref[slot, hh], axis=0, keepdims=True), (SUBLANES, NA_TQ))
                for hh in range(2)]

    def pv_tick(slot, i, t, mx, acc):
        ch = window(i)[0] + t if t < NA_WIN_TICKS else n_lat
        for hh in range(2):
            s = s_ref[slot, hh, t * NA_TK:(t + 1) * NA_TK, :].reshape(NA_TK // SUBLANES, SUBLANES, NA_TQ)
            p = jnp.exp2(s - mx[hh])
            d = [jnp.dot(vte_ref[hh, ch], p.reshape(NA_TK, NA_TQ).astype(BF16), preferred_element_type=F32),
                 jnp.sum(p, axis=0)]
            acc[hh] = d if acc[hh] is None else [a + b for a, b in zip(acc[hh], d)]

    def finish(i, acc):
        ot = jnp.concatenate(
            [acc[hh][0] * (1.0 / jnp.sum(acc[hh][1], axis=0, keepdims=True)) for hh in range(2)],
            axis=0)
        o_ref[0, rows_of(i), :] = ot.T.astype(BF16)

    def start():
        return ([functools.partial(prep, 0, 0)]
                + [functools.partial(score_tick, 0, 0, t) for t in range(n_tick)])

    def block(i, slot, last=False):
        state = {}

        def head():
            state["mx"] = final_max(slot)
            state["acc"] = [None, None]
            if not last:
                prep(i + 1, 1 - slot)

        def tick(t):
            pv_tick(slot, i, t, state["mx"], state["acc"])
            if not last:
                score_tick(1 - slot, i + 1, t)

        return ([head] + [functools.partial(tick, t) for t in range(n_tick)]
                + [lambda: finish(i, state["acc"])])

    return start, block


def _na_scratch(n_lat):
    return [pltpu.VMEM((2, 3, NA_KROWS * GRID_W, NA_TQ), F32),
            pltpu.VMEM((2, n_lat + 1, HEAD_DIM, NA_TK), BF16),
            pltpu.VMEM((2, 2, LANES, NA_TQ), BF16),
            pltpu.VMEM((2, 2, (NA_WIN_TICKS + 1) * NA_TK, NA_TQ), F32),
            pltpu.VMEM((2, 2, SUBLANES, NA_TQ), F32)]


N_DIFF_IN, N_NA_IN = 7, 6


def _attn_kernel(*refs, rows):
    d_in = refs[:N_DIFF_IN]
    n_in = refs[N_DIFF_IN:N_DIFF_IN + N_NA_IN]
    oa_ref, ob_ref = refs[N_DIFF_IN + N_NA_IN:N_DIFF_IN + N_NA_IN + 2]
    scratch = refs[N_DIFF_IN + N_NA_IN + 2:]
    d_start, d_block = _diff_pipeline(*d_in, oa_ref, *scratch[:5])
    n_start, n_block = _na_pipeline(*n_in, ob_ref, *scratch[5:], rows=rows)
    n_blk = rows // NA_QROWS
    _interleave(d_start(), n_start())

    def pair(u, carry):
        for j in range(2):
            _interleave(d_block(2 * u + j, j), n_block(2 * u + j, j))
        return carry

    lax.fori_loop(0, n_blk // 2 - 1, pair, 0)
    _interleave(d_block(n_blk - 2, 0), n_block(n_blk - 2, 0))
    _interleave(d_block(n_blk - 1, 1, last=True), n_block(n_blk - 1, 1, last=True))


def _attention(aq, ak, av, akc, avc, lamv, gout, nq, nk, nv, nkc, nvc, rpb):
    bsz, s, _ = aq.shape
    c = akc.shape[1]
    rows = s // GRID_W
    n_blk = rows // NA_QROWS
    assert NA_HEADS // 2 == DIFF_HEADS and NA_TQ == DIFF_TQ and NA_QROWS * GRID_W == NA_TK and c == NA_TK
    assert n_blk >= 4 and n_blk % 2 == 0 and c % DIFF_TK == 0 and s % DIFF_TK == 0
    n_r, n_c = rpb.shape[1:]
    padded = jnp.pad(rpb[:, :, ::-1], ((0, 0), (1, 1), (0, GRID_W - n_c)))
    rpb2 = jnp.concatenate([padded[:, 1:], padded[:, :-1]], axis=-1).reshape(DIFF_HEADS, 2, n_r + 1, LANES)
    lat = pl.BlockSpec((1, s, LANES), lambda h, b: (b, 0, h))
    ctx = pl.BlockSpec((1, c, LANES), lambda h, b: (b, 0, h))
    return pl.pallas_call(
        functools.partial(_attn_kernel, rows=rows),
        grid=(DIFF_HEADS, bsz),
        in_specs=[lat, ctx, lat, ctx, lat,
                  pl.BlockSpec(lamv.shape, lambda h, b: (0, 0)),
                  pl.BlockSpec((LANES, 1), lambda h, b: (0, 0)),
                  lat, ctx, lat, ctx, lat,
                  pl.BlockSpec((1,) + rpb2.shape[1:], lambda h, b: (h, 0, 0, 0))],
        out_specs=[lat, lat],
        out_shape=[jax.ShapeDtypeStruct(aq.shape, BF16), jax.ShapeDtypeStruct(nq.shape, BF16)],
        scratch_shapes=_diff_scratch(c, s) + _na_scratch(s // NA_TK),
        compiler_params=_cparams(("arbitrary", "arbitrary")),
        name="attention",
    )(aq, akc, ak, avc, av, lamv, gout.reshape(LANES, 1), nq, nkc, nk, nvc, nv, rpb2)


def _rope_tables(n_tokens):
    t = jnp.arange(n_tokens, dtype=jnp.int32)
    row = (t // GRID_W).astype(F32)
    col = (t % GRID_W).astype(F32)
    n_freq = HEAD_DIM // 4
    inv_freq = ROPE_BASE ** (-jnp.arange(n_freq, dtype=F32) / n_freq)
    ar = row[:, None] * inv_freq
    ac = col[:, None] * inv_freq
    cos = jnp.concatenate([jnp.cos(ar), jnp.cos(ar), jnp.cos(ac), jnp.cos(ac)], axis=-1)
    sin = jnp.concatenate([-jnp.sin(ar), jnp.sin(ar), -jnp.sin(ac), jnp.sin(ac)], axis=-1)
    reps = LANES // HEAD_DIM
    return jnp.tile(cos, (1, reps)), jnp.tile(sin, (1, reps))


def kernel(x, c, ctx, c_ctx, w_ada, b_ada, norm1, norm2, norm3, ffn1_w_gu, ffn1_w_down, w_in,
           diff_q_norm, diff_k_norm, lam_q1, lam_k1, lam_q2, lam_k2, diff_out_norm,
           na_q_norm, na_k_norm, na_rpb, w_out, ffn2_w_gu, ffn2_w_down):
    bsz, s, d = x.shape
    assert w_ada.shape[0] == 1 and d == D_MODEL and s % (NA_QROWS * GRID_W) == 0
    rows = s // GRID_W

    cond = jnp.zeros((16, d), F32).at[:bsz].set(c).at[bsz].set(c_ctx)
    mod4 = _adaln(cond, w_ada[0], b_ada).reshape(16, N_MOD, 1, d)

    ffn1_w = _ffn_weights(ffn1_w_gu[0], ffn1_w_down[0])
    h, w_in_b, w_out_b, w_gu2, w_dn2 = _ffn(x, mod4, 0, False, norm1, ffn1_w, tm=1024,
                                            casts=(w_in[0], w_out[0], ffn2_w_gu[0], ffn2_w_down[0]))
    n_ctx = ctx.shape[1]
    hc = _ffn(ctx.reshape(1, bsz * n_ctx, d), mod4, 0, True, norm1, ffn1_w, tm=1024)

    seg = np.arange(MXU_W) // HEAD_DIM
    bd = jnp.asarray(seg[:, None] == seg[None, :], BF16)
    cos, sin = _rope_tables(s)
    reps = GROUP_W // HEAD_DIM
    gains = [jnp.tile(g, (1, reps)) for g in (diff_q_norm, diff_k_norm, na_q_norm, na_k_norm)]
    qs = HEAD_DIM ** -0.5
    lat_groups = ((0, 0, True, qs * LOG2E), (1, 1, True, 1.0), (2, None, False, 1.0),
                  (3, 2, False, qs * LOG2E), (4, 3, False, 1.0), (5, None, False, 1.0))
    aq, ak, av, nq, nk, nv = _in_proj(h, mod4, False, norm2, w_in_b, bd, cos, sin, gains,
                                      lat_groups, tm=1024)
    ctx_groups = ((1, 1, False, 1.0), (2, None, False, 1.0), (4, 3, False, 1.0), (5, None, False, 1.0))
    ctx_kv = _in_proj(hc, mod4, True, norm2, w_in_b, bd, cos, sin, gains, ctx_groups, tm=512)
    akc, avc, nkc, nvc = (a.reshape(bsz, n_ctx, GROUP_W) for a in ctx_kv)

    lamv = jnp.concatenate([lam_q1, lam_k1, lam_q2, lam_k2], axis=0)
    oa, ob = _attention(aq, ak, av, akc, avc, lamv, diff_out_norm, nq, nk, nv, nkc, nvc, na_rpb[0])

    return _ffn(h, mod4, 6, False, norm3, (w_gu2, w_dn2), tm=1024, attn=(5, oa, ob, w_out_b))
```

```python
import functools
import math

import numpy as np
import jax
import jax.numpy as jnp
from jax import lax
from jax.experimental import pallas as pl
from jax.experimental.pallas import tpu as pltpu

D_MODEL = 1024
GRID_W = 64
HEAD_DIM = 64
DIFF_HEADS = 4
NA_HEADS = 8
GROUP_W = 512
D_FF = 2816
N_MOD = 9
NA_KH = 8
NA_KW = 16
ROPE_BASE = 10000.0
NORM_EPS = 1e-6
NEG_INF = -1e30
LAM_INIT = 0.8 - 0.6 * math.exp(-0.3 * 0)
LOG2E = math.log2(math.e)

LANES = 128
BF16_ROWS = 16
MXU_W = 256
VMEM_LIMIT = 56 * 1024 * 1024

F32 = jnp.float32
BF16 = jnp.bfloat16


def _cparams(sem):
    return pltpu.CompilerParams(dimension_semantics=sem, vmem_limit_bytes=VMEM_LIMIT)


ADALN_TK = 128


def _adaln_kernel(cond_ref, w_ref, b_ref, o_ref):
    k = pl.program_id(0)
    c = cond_ref[...]
    a = (c * jax.nn.sigmoid(c)).astype(BF16)
    part = jnp.dot(a, w_ref[...].astype(BF16), preferred_element_type=F32)

    @pl.when(k == 0)
    def _():
        o_ref[...] = part + b_ref[...]

    @pl.when(k > 0)
    def _():
        o_ref[...] += part


def _adaln(cond, w_ada, b_ada):
    rows, d = cond.shape
    n = w_ada.shape[1]
    return pl.pallas_call(
        _adaln_kernel,
        grid=(d // ADALN_TK,),
        in_specs=[pl.BlockSpec((rows, ADALN_TK), lambda k: (0, k)),
                  pl.BlockSpec((ADALN_TK, n), lambda k: (k, 0)),
                  pl.BlockSpec((1, n), lambda k: (0, 0))],
        out_specs=pl.BlockSpec((rows, n), lambda k: (0, 0)),
        out_shape=jax.ShapeDtypeStruct((rows, n), F32),
        compiler_params=_cparams(("arbitrary",)),
        name="adaln",
    )(cond, w_ada, b_ada)


def _modulated_norm(x, g, shift, scale):
    ms = jnp.mean(x * x, axis=-1, keepdims=True)
    xn = x * lax.rsqrt(ms + NORM_EPS) * g
    return xn * (1.0 + scale) + shift


def _mod_spec(k, ctx):
    if ctx:
        return pl.BlockSpec((1, 1, 1, D_MODEL), lambda b, *_: (8, k, 0, 0))
    return pl.BlockSpec((1, 1, 1, D_MODEL), lambda b, *_: (b, k, 0, 0))


FFN_TF = 256
FFN_TM = 1024


def _ffn_kernel(*refs, with_ctx, with_attn, n_cast):
    refs = list(refs)
    x_ref = refs.pop(0)
    if with_ctx:
        ctx_ref = refs.pop(0)
    if with_attn:
        agate_ref, oa_ref, ob_ref, wo_ref = (refs.pop(0) for _ in range(4))
    shift_ref, scale_ref, gate_ref, g_ref, wgu_ref, wd_ref = (refs.pop(0) for _ in range(6))
    cast_in = [refs.pop(0) for _ in range(n_cast)]
    o_ref = refs.pop(0)
    cast_out = [refs.pop(0) for _ in range(n_cast)]
    xm_ref, acc_ref = refs[:2]
    for src, dst in zip(cast_in, cast_out):
        dst[...] = src[...].astype(BF16)
    def stream():
        if with_attn:
            return h_ref[0]
        if with_ctx:
            return jnp.where(pl.program_id(0) == pl.num_programs(0) - 1, ctx_ref[0], x_ref[0])
        return x_ref[0]

    if with_attn:
        h_ref = refs[2]
        half = oa_ref.shape[-1]
        attn = (jnp.dot(oa_ref[0], wo_ref[0:half, :], preferred_element_type=F32)
                + jnp.dot(ob_ref[0], wo_ref[half:, :], preferred_element_type=F32))
        h_ref[0] = x_ref[0] + agate_ref[0, 0] * attn
    f = wd_ref.shape[0]
    nj = f // FFN_TF
    xm_ref[...] = _modulated_norm(stream(), g_ref[...], shift_ref[0, 0], scale_ref[0, 0]).astype(BF16)

    def act(j):
        xm = xm_ref[...]
        g = jnp.dot(xm, wgu_ref[:, j * FFN_TF:(j + 1) * FFN_TF], preferred_element_type=F32)
        u = jnp.dot(xm, wgu_ref[:, f + j * FFN_TF:f + (j + 1) * FFN_TF], preferred_element_type=F32)
        return (g * jax.nn.sigmoid(g) * u).astype(BF16)

    a = act(0)
    for j in range(nj):
        down = jnp.dot(a, wd_ref[j * FFN_TF:(j + 1) * FFN_TF, :], preferred_element_type=F32)
        if j + 1 < nj:
            a = act(j + 1)
        if j == 0:
            acc_ref[...] = down
        elif j + 1 < nj:
            acc_ref[...] += down
        else:
            o_ref[0] = stream() + (0.5 * gate_ref[0, 0]) * (acc_ref[...] + down)


def _ffn_weights(w_gu, w_down):
    return w_gu.astype(BF16), w_down.astype(BF16)


def _resident(shape):
    return pl.BlockSpec(shape, lambda *_: (0,) * len(shape), pipeline_mode=pl.Buffered(1))


def _ffn(h, bsz, mod4, k0, g, weights, *, tm, ctx_tokens=None, attn=None, casts=()):
    _, t, d = h.shape
    wgu, wd = weights
    assert wd.shape[0] % FFN_TF == 0
    n_i = t // tm
    n_b = bsz + (ctx_tokens is not None)
    tile = pl.BlockSpec((1, tm, d), lambda b, i: (b, i, 0))
    in_specs, args = [tile], [h]
    scratch = [pltpu.VMEM((tm, d), BF16), pltpu.VMEM((tm, d), F32)]
    if ctx_tokens is not None:
        assert ctx_tokens.shape == (1, t, d)
        assert attn is None
        in_specs = [pl.BlockSpec((1, tm, d), lambda b, i: (jnp.minimum(b, bsz - 1), jnp.where(b < bsz, i, n_i - 1), 0)),
                    pl.BlockSpec((1, tm, d), lambda b, i: (0, jnp.where(b < bsz, 0, i), 0),
                                 pipeline_mode=pl.Buffered(1))]
        args.append(ctx_tokens)
    if attn is not None:
        k_gate, oa, ob, w_out = attn
        half = oa.shape[-1]
        in_specs += [_mod_spec(k_gate, False), pl.BlockSpec((1, tm, half), lambda b, i: (b, i, 0)),
                     pl.BlockSpec((1, tm, half), lambda b, i: (b, i, 0)), _resident(w_out.shape)]
        args += [mod4, oa, ob, w_out]
        scratch.append(pltpu.VMEM((1, tm, d), F32))
    in_specs += [_mod_spec(k0, False), _mod_spec(k0 + 1, False), _mod_spec(k0 + 2, False),
                 pl.BlockSpec((1, d), lambda b, i: (0, 0)), _resident(wgu.shape), _resident(wd.shape)]
    args += [mod4, mod4, mod4, g, wgu, wd]
    slabs = []
    n_conv = bsz * n_i
    for w in casts:
        rows_per_step = w.shape[0] // n_conv
        assert rows_per_step * n_conv == w.shape[0] and rows_per_step % BF16_ROWS == 0
        slabs.append(pl.BlockSpec((rows_per_step, w.shape[1]),
                                  lambda b, i: (jnp.minimum(b * n_i + i, n_conv - 1), 0)))
    out = pl.pallas_call(
        functools.partial(_ffn_kernel, with_ctx=ctx_tokens is not None, with_attn=attn is not None,
                          n_cast=len(casts)),
        grid=(n_b, n_i),
        in_specs=in_specs + slabs,
        out_specs=[tile] + slabs,
        out_shape=[jax.ShapeDtypeStruct((n_b, t, d), F32)] + [jax.ShapeDtypeStruct(w.shape, BF16) for w in casts],
        scratch_shapes=scratch,
        compiler_params=_cparams(("arbitrary", "arbitrary")),
        name="ffn",
    )(*args, *casts)
    return out if casts else out[0]


PROJ_TM = 1024
PROJ_TM_CTX = 512

def _head_rms_scale(y, bd_ref):
    y2 = (y * y).astype(BF16)
    bd = bd_ref[...]
    w = bd.shape[0]
    ss = jnp.concatenate(
        [jnp.dot(y2[:, c:c + w], bd, preferred_element_type=F32) for c in range(0, y.shape[1], w)], axis=-1)
    return lax.rsqrt(ss * (1.0 / HEAD_DIM) + NORM_EPS)


def _rope(z, cos, sin):
    lane = lax.broadcasted_iota(jnp.int32, z.shape, 1)
    up = pltpu.roll(z, LANES - 16, 1)
    dn = pltpu.roll(z, 16, 1)
    return z * cos + jnp.where((lane & 16) == 0, up, dn) * sin


def _in_proj_kernel(*refs, groups, n_gain):
    h_ref, shift_ref, scale_ref, g_ref, w_ref, bd_ref, cos_ref, sin_ref = refs[:8]
    gain_refs = refs[8:8 + n_gain]
    out_refs = refs[8 + n_gain:]
    xm = _modulated_norm(h_ref[0], g_ref[...], shift_ref[0, 0], scale_ref[0, 0]).astype(BF16)
    for (col, gain_idx, rope, qscale), o_ref in zip(groups, out_refs):
        y = jnp.dot(xm, w_ref[:, col * GROUP_W:(col + 1) * GROUP_W], preferred_element_type=F32)
        if gain_idx is not None:
            gain = gain_refs[gain_idx][...]
            if qscale != 1.0:
                gain = gain * qscale
            y = y * _head_rms_scale(y, bd_ref) * gain
        if rope:
            cos = cos_ref[...]
            sin = sin_ref[...]
            for c in range(GROUP_W // LANES):
                sl = slice(c * LANES, (c + 1) * LANES)
                o_ref[0, :, sl] = _rope(y[:, sl], cos, sin).astype(BF16)
        else:
            o_ref[0] = y.astype(BF16)


def _in_proj(h, first, bsz, mod4, ctx, g, w_in, bd, cos, sin, gains, groups, *, tm):
    _, t, d = h.shape
    n_out = len(groups)
    in_specs = [pl.BlockSpec((1, tm, d), lambda b, i: (b + first, i, 0)),
                _mod_spec(3, ctx), _mod_spec(4, ctx),
                pl.BlockSpec((1, d), lambda b, i: (0, 0)),
                pl.BlockSpec(w_in.shape, lambda b, i: (0, 0)),
                pl.BlockSpec(bd.shape, lambda b, i: (0, 0)),
                pl.BlockSpec((tm, LANES), lambda b, i: (i, 0)),
                pl.BlockSpec((tm, LANES), lambda b, i: (i, 0))]
    in_specs += [pl.BlockSpec((1, GROUP_W), lambda b, i: (0, 0)) for _ in gains]
    return pl.pallas_call(
        functools.partial(_in_proj_kernel, groups=groups, n_gain=len(gains)),
        grid=(bsz, t // tm),
        in_specs=in_specs,
        out_specs=[pl.BlockSpec((1, tm, GROUP_W), lambda b, i: (b, i, 0)) for _ in range(n_out)],
        out_shape=[jax.ShapeDtypeStruct((bsz, t, GROUP_W), BF16) for _ in range(n_out)],
        compiler_params=_cparams(("parallel", "parallel")),
        name="in_proj_ctx" if ctx else "in_proj",
    )(h, mod4, mod4, g, w_in, bd, cos, sin, *gains)


def _split_halves(q):
    qt = q.astype(F32).T
    row = lax.broadcasted_iota(jnp.int32, qt.shape, 0)
    lo = jnp.where(row < HEAD_DIM, qt, 0.0).astype(BF16)
    hi = jnp.where(row >= HEAD_DIM, qt, 0.0).astype(BF16)
    return lo, hi


DIFF_TQ = 256
DIFF_TK = 256
SUBLANES = 8


def _interleave(*stages):
    order = sorted(((k + 0.5) / len(st), n, k) for n, st in enumerate(stages) for k in range(len(st)))
    for _, n, k in order:
        stages[n][k]()


def _diff_pipeline(q_ref, kc_ref, kl_ref, vc_ref, vl_ref, lamv_ref, gcol_ref, o_ref,
                   kk_ref, vte_ref, qm_ref, s_ref, m_ref):
    n_ctx = kc_ref.shape[1]
    n_tick = kk_ref.shape[0] // DIFF_TK
    n_blk = q_ref.shape[1] // DIFF_TQ
    kk_ref[0:n_ctx] = kc_ref[0]
    kk_ref[n_ctx:] = kl_ref[0]
    for kc in range(n_tick):
        lo = kc * DIFF_TK
        v = vc_ref[0, lo:lo + DIFF_TK] if lo < n_ctx else vl_ref[0, lo - n_ctx:lo - n_ctx + DIFF_TK]
        vte_ref[kc] = v.astype(F32).T.astype(BF16)
    lv = lamv_ref[...]
    lam = (jnp.exp(jnp.sum(lv[0:1] * lv[1:2], axis=-1, keepdims=True))
           - jnp.exp(jnp.sum(lv[2:3] * lv[3:4], axis=-1, keepdims=True)) + LAM_INIT)

    def rows(i):
        return pl.ds(pl.multiple_of(i * DIFF_TQ, DIFF_TQ), DIFF_TQ)

    def keys(kc):
        return slice(kc * DIFF_TK, (kc + 1) * DIFF_TK)

    def prep(i, slot):
        lo, hi = _split_halves(q_ref[0, rows(i), :])
        qm_ref[slot, 0] = lo
        qm_ref[slot, 1] = hi
        m_ref[slot] = jnp.full(m_ref.shape[1:], -jnp.inf, F32)

    def score_tick(slot, kc):
        k = kk_ref[keys(kc), :]
        for c in range(2):
            s = jnp.dot(k, qm_ref[slot, c], preferred_element_type=F32)
            s_ref[slot, c, keys(kc), :] = s
            part = jnp.max(s.reshape(DIFF_TK // SUBLANES, SUBLANES, DIFF_TQ), axis=0)
            m_ref[slot, c] = jnp.maximum(m_ref[slot, c], part)

    def final_max(slot):
        return [jnp.broadcast_to(jnp.max(m_ref[slot, c], axis=0, keepdims=True), (SUBLANES, DIFF_TQ))
                for c in range(2)]

    def pv_tick(slot, kc, mx, acc):
        for c in range(2):
            s = s_ref[slot, c, keys(kc), :].reshape(DIFF_TK // SUBLANES, SUBLANES, DIFF_TQ)
            p = jnp.exp2(s - mx[c])
            d = [jnp.dot(vte_ref[kc], p.reshape(DIFF_TK, DIFF_TQ).astype(BF16), preferred_element_type=F32),
                 jnp.sum(p, axis=0)]
            acc[c] = d if acc[c] is None else [a + b for a, b in zip(acc[c], d)]

    def finish(i, acc):
        l0 = jnp.sum(acc[0][1], axis=0, keepdims=True)
        l1 = jnp.sum(acc[1][1], axis=0, keepdims=True)
        ot = acc[0][0] * (1.0 / l0) - acc[1][0] * (lam / l1)
        ms = jnp.mean(ot * ot, axis=0, keepdims=True)
        on = ot * lax.rsqrt(ms + NORM_EPS) * (gcol_ref[...] * (1.0 - LAM_INIT))
        o_ref[0, rows(i), :] = on.T.astype(BF16)

    def start():
        return ([functools.partial(prep, 0, 0)]
                + [functools.partial(score_tick, 0, kc) for kc in range(n_tick)])

    def block(i, slot, last=False):
        state = {}

        def head():
            state["mx"] = final_max(slot)
            state["acc"] = [None, None]
            if not last:
                prep(i + 1, 1 - slot)

        def tick(kc):
            pv_tick(slot, kc, state["mx"], state["acc"])
            if not last:
                score_tick(1 - slot, kc)

        return ([head] + [functools.partial(tick, kc) for kc in range(n_tick)]
                + [lambda: finish(i, state["acc"])])

    return start, block


def _diff_scratch(c, s):
    return [pltpu.VMEM((c + s, LANES), BF16),
            pltpu.VMEM(((c + s) // DIFF_TK, LANES, DIFF_TK), BF16),
            pltpu.VMEM((2, 2, LANES, DIFF_TQ), BF16),
            pltpu.VMEM((2, 2, c + s, DIFF_TQ), F32),
            pltpu.VMEM((2, 2, SUBLANES, DIFF_TQ), F32)]


NA_QROWS = 4
NA_KROWS = 12
NA_TQ = NA_QROWS * GRID_W
NA_TK = 4 * GRID_W
NA_WIN_TICKS = NA_KROWS * GRID_W // NA_TK


def _na_variants(rows):
    n_rb = rows // NA_QROWS
    out = []
    for rb in (0, 1, n_rb - 1):
        ws = min(max(rb * NA_QROWS - NA_KH // 2, 0), rows - NA_KROWS)
        out.append((rb * NA_QROWS, ws))
    return out


def _na_build_bias(rpb_ref, bias_ref, rows):
    shape = (GRID_W, LANES)
    kc = lax.broadcasted_iota(jnp.int32, shape, 0)
    lane = lax.broadcasted_iota(jnp.int32, shape, 1)
    qc = lane & (GRID_W - 1)
    cs = jnp.clip(qc - NA_KW // 2, 0, GRID_W - NA_KW)
    col_ok = (kc >= cs) & (kc < cs + NA_KW)
    side_ok = {(True, True): col_ok, (True, False): col_ok & (lane < GRID_W),
               (False, True): col_ok & (lane >= GRID_W)}
    neg = jnp.full(shape, NEG_INF, F32)
    for hh in range(2):
        toeplitz = {}
        for v, (r0, ws) in enumerate(_na_variants(rows)):
            for i in range(NA_KROWS):
                kr = ws + i
                for jp in range(NA_QROWS // 2):
                    ok = []
                    for r in (r0 + 2 * jp, r0 + 2 * jp + 1):
                        rs = min(max(r - NA_KH // 2, 0), rows - NA_KH)
                        ok.append(rs <= kr < rs + NA_KH)
                    tile = neg
                    if any(ok):
                        e = kr - (r0 + 2 * jp) + NA_KH - 1
                        if e not in toeplitz:
                            vec = jnp.broadcast_to(rpb_ref[0, hh, e:e + 1, :] * LOG2E, shape)
                            toeplitz[e] = pltpu.roll(vec, LANES - (NA_KW - 1), 1, stride=1, stride_axis=0)
                        tile = jnp.where(side_ok[tuple(ok)], toeplitz[e], neg)
                    bias_ref[hh, v, i * GRID_W:(i + 1) * GRID_W, jp * LANES:(jp + 1) * LANES] = tile


def _na_pipeline(q_ref, kc_ref, k_ref, vc_ref, v_ref, rpb_ref, o_ref,
                 bias_ref, vte_ref, qm_ref, s_ref, m_ref, *, rows):
    n_blk = rows // NA_QROWS
    n_lat = v_ref.shape[1] // NA_TK
    assert kc_ref.shape[1] == NA_TK
    n_tick = NA_WIN_TICKS + 1

    @pl.when(pl.program_id(1) == 0)
    def _():
        _na_build_bias(rpb_ref, bias_ref, rows)

    for ch in range(n_lat + 1):
        v = v_ref[0, ch * NA_TK:(ch + 1) * NA_TK] if ch < n_lat else vc_ref[0]
        vt = v.astype(F32).T
        for hh in range(2):
            vte_ref[hh, ch] = vt[hh * HEAD_DIM:(hh + 1) * HEAD_DIM].astype(BF16)

    def rows_of(i):
        return pl.ds(pl.multiple_of(i * NA_TQ, NA_TQ), NA_TQ)

    def window(i):
        return (jnp.clip(i - 1, 0, n_lat - NA_WIN_TICKS),
                jnp.where(i == 0, 0, jnp.where(i == n_blk - 1, 2, 1)))

    def prep(i, slot):
        lo, hi = _split_halves(q_ref[0, rows_of(i), :])
        qm_ref[slot, 0] = lo
        qm_ref[slot, 1] = hi
        m_ref[slot] = jnp.full(m_ref.shape[1:], -jnp.inf, F32)

    def score_tick(slot, i, t):
        ch0, variant = window(i)
        if t < NA_WIN_TICKS:
            k = k_ref[0, pl.ds(pl.multiple_of((ch0 + t) * NA_TK, NA_TK), NA_TK), :]
        else:
            k = kc_ref[0]
        for hh in range(2):
            s = jnp.dot(k, qm_ref[slot, hh], preferred_element_type=F32)
            if t < NA_WIN_TICKS:
                s = s + bias_ref[hh, variant, t * NA_TK:(t + 1) * NA_TK, :]
            s_ref[slot, hh, t * NA_TK:(t + 1) * NA_TK, :] = s
            part = jnp.max(s.reshape(NA_TK // SUBLANES, SUBLANES, NA_TQ), axis=0)
            m_ref[slot, hh] = jnp.maximum(m_ref[slot, hh], part)

    def final_max(slot):
        return [jnp.broadcast_to(jnp.max(m_ref[slot, hh], axis=0, keepdims=True), (SUBLANES, NA_TQ))
                for hh in range(2)]

    def pv_tick(slot, i, t, mx, acc):
        ch = window(i)[0] + t if t < NA_WIN_TICKS else n_lat
        for hh in range(2):
            s = s_ref[slot, hh, t * NA_TK:(t + 1) * NA_TK, :].reshape(NA_TK // SUBLANES, SUBLANES, NA_TQ)
            p = jnp.exp2(s - mx[hh])
            d = [jnp.dot(vte_ref[hh, ch], p.reshape(NA_TK, NA_TQ).astype(BF16), preferred_element_type=F32),
                 jnp.sum(p, axis=0)]
            acc[hh] = d if acc[hh] is None else [a + b for a, b in zip(acc[hh], d)]

    def finish(i, acc):
        ot = jnp.concatenate(
            [acc[hh][0] * (1.0 / jnp.sum(acc[hh][1], axis=0, keepdims=True)) for hh in range(2)],
            axis=0)
        o_ref[0, rows_of(i), :] = ot.T.astype(BF16)

    def start():
        return ([functools.partial(prep, 0, 0)]
                + [functools.partial(score_tick, 0, 0, t) for t in range(n_tick)])

    def block(i, slot, last=False):
        state = {}

        def head():
            state["mx"] = final_max(slot)
            state["acc"] = [None, None]
            if not last:
                prep(i + 1, 1 - slot)

        def tick(t):
            pv_tick(slot, i, t, state["mx"], state["acc"])
            if not last:
                score_tick(1 - slot, i + 1, t)

        return ([head] + [functools.partial(tick, t) for t in range(n_tick)]
                + [lambda: finish(i, state["acc"])])

    return start, block


def _na_scratch(n_lat):
    return [pltpu.VMEM((2, 3, NA_KROWS * GRID_W, NA_TQ), F32),
            pltpu.VMEM((2, n_lat + 1, HEAD_DIM, NA_TK), BF16),
            pltpu.VMEM((2, 2, LANES, NA_TQ), BF16),
            pltpu.VMEM((2, 2, (NA_WIN_TICKS + 1) * NA_TK, NA_TQ), F32),
            pltpu.VMEM((2, 2, SUBLANES, NA_TQ), F32)]


N_DIFF_IN, N_NA_IN = 7, 6


def _attn_kernel(*refs, rows):
    d_in = refs[:N_DIFF_IN]
    n_in = refs[N_DIFF_IN:N_DIFF_IN + N_NA_IN]
    oa_ref, ob_ref = refs[N_DIFF_IN + N_NA_IN:N_DIFF_IN + N_NA_IN + 2]
    scratch = refs[N_DIFF_IN + N_NA_IN + 2:]
    d_start, d_block = _diff_pipeline(*d_in, oa_ref, *scratch[:5])
    n_start, n_block = _na_pipeline(*n_in, ob_ref, *scratch[5:], rows=rows)
    n_blk = rows // NA_QROWS
    _interleave(d_start(), n_start())

    def pair(u, carry):
        for j in range(2):
            _interleave(d_block(2 * u + j, j), n_block(2 * u + j, j))
        return carry

    lax.fori_loop(0, n_blk // 2 - 1, pair, 0)
    _interleave(d_block(n_blk - 2, 0), n_block(n_blk - 2, 0))
    _interleave(d_block(n_blk - 1, 1, last=True), n_block(n_blk - 1, 1, last=True))


def _attention(aq, ak, av, akc, avc, lamv, gout, nq, nk, nv, nkc, nvc, rpb):
    bsz, s, _ = aq.shape
    c = akc.shape[1]
    rows = s // GRID_W
    n_blk = rows // NA_QROWS
    assert NA_HEADS // 2 == DIFF_HEADS and NA_TQ == DIFF_TQ and NA_QROWS * GRID_W == NA_TK and c == NA_TK
    assert n_blk >= 4 and n_blk % 2 == 0 and c % DIFF_TK == 0 and s % DIFF_TK == 0
    n_r, n_c = rpb.shape[1:]
    padded = jnp.pad(rpb[:, :, ::-1], ((0, 0), (1, 1), (0, GRID_W - n_c)))
    rpb2 = jnp.concatenate([padded[:, 1:], padded[:, :-1]], axis=-1).reshape(DIFF_HEADS, 2, n_r + 1, LANES)
    lat = pl.BlockSpec((1, s, LANES), lambda h, b: (b, 0, h))
    ctx = pl.BlockSpec((1, c, LANES), lambda h, b: (b, 0, h))
    return pl.pallas_call(
        functools.partial(_attn_kernel, rows=rows),
        grid=(DIFF_HEADS, bsz),
        in_specs=[lat, ctx, lat, ctx, lat,
                  pl.BlockSpec(lamv.shape, lambda h, b: (0, 0)),
                  pl.BlockSpec((LANES, 1), lambda h, b: (0, 0)),
                  lat, ctx, lat, ctx, lat,
                  pl.BlockSpec((1,) + rpb2.shape[1:], lambda h, b: (h, 0, 0, 0))],
        out_specs=[lat, lat],
        out_shape=[jax.ShapeDtypeStruct(aq.shape, BF16), jax.ShapeDtypeStruct(nq.shape, BF16)],
        scratch_shapes=_diff_scratch(c, s) + _na_scratch(s // NA_TK),
        compiler_params=_cparams(("arbitrary", "arbitrary")),
        name="attention",
    )(aq, akc, ak, avc, av, lamv, gout.reshape(LANES, 1), nq, nkc, nk, nvc, nv, rpb2)


def _rope_tables(n_tokens):
    t = jnp.arange(n_tokens, dtype=jnp.int32)
    row = (t // GRID_W).astype(F32)
    col = (t % GRID_W).astype(F32)
    n_freq = HEAD_DIM // 4
    inv_freq = ROPE_BASE ** (-jnp.arange(n_freq, dtype=F32) / n_freq)
    ar = row[:, None] * inv_freq
    ac = col[:, None] * inv_freq
    cos = jnp.concatenate([jnp.cos(ar), jnp.cos(ar), jnp.cos(ac), jnp.cos(ac)], axis=-1)
    sin = jnp.concatenate([-jnp.sin(ar), jnp.sin(ar), -jnp.sin(ac), jnp.sin(ac)], axis=-1)
    reps = LANES // HEAD_DIM
    return jnp.tile(cos, (1, reps)), jnp.tile(sin, (1, reps))


def kernel(x, c, ctx, c_ctx, w_ada, b_ada, norm1, norm2, norm3, ffn1_w_gu, ffn1_w_down, w_in,
           diff_q_norm, diff_k_norm, lam_q1, lam_k1, lam_q2, lam_k2, diff_out_norm,
           na_q_norm, na_k_norm, na_rpb, w_out, ffn2_w_gu, ffn2_w_down):
    bsz, s, d = x.shape
    assert w_ada.shape[0] == 1 and d == D_MODEL and s % (NA_QROWS * GRID_W) == 0
    rows = s // GRID_W

    cond = jnp.zeros((16, d), F32).at[:bsz].set(c).at[bsz].set(c_ctx)
    mod4 = _adaln(cond, w_ada[0], b_ada).reshape(16, N_MOD, 1, d)

    n_ctx = ctx.shape[1]
    assert bsz * n_ctx == s
    ffn1_w = _ffn_weights(ffn1_w_gu[0], ffn1_w_down[0])
    h, w_in_b, w_out_b, w_gu2, w_dn2 = _ffn(x, bsz, mod4, 0, norm1, ffn1_w, tm=FFN_TM,
                                            ctx_tokens=ctx.reshape(1, s, d),
                                            casts=(w_in[0], w_out[0], ffn2_w_gu[0], ffn2_w_down[0]))

    seg = np.arange(MXU_W) // HEAD_DIM
    bd = jnp.asarray(seg[:, None] == seg[None, :], BF16)
    cos, sin = _rope_tables(s)
    reps = GROUP_W // HEAD_DIM
    gains = [jnp.tile(g, (1, reps)) for g in (diff_q_norm, diff_k_norm, na_q_norm, na_k_norm)]
    qs = HEAD_DIM ** -0.5
    lat_groups = ((0, 0, True, qs * LOG2E), (1, 1, True, 1.0), (2, None, False, 1.0),
                  (3, 2, False, qs * LOG2E), (4, 3, False, 1.0), (5, None, False, 1.0))
    aq, ak, av, nq, nk, nv = _in_proj(h, 0, bsz, mod4, False, norm2, w_in_b, bd, cos, sin, gains,
                                      lat_groups, tm=PROJ_TM)
    ctx_groups = ((1, 1, False, 1.0), (2, None, False, 1.0), (4, 3, False, 1.0), (5, None, False, 1.0))
    ctx_kv = _in_proj(h, bsz, 1, mod4, True, norm2, w_in_b, bd, cos, sin, gains, ctx_groups, tm=PROJ_TM_CTX)
    akc, avc, nkc, nvc = (a.reshape(bsz, n_ctx, GROUP_W) for a in ctx_kv)

    lamv = jnp.concatenate([lam_q1, lam_k1, lam_q2, lam_k2], axis=0)
    oa, ob = _attention(aq, ak, av, akc, avc, lamv, diff_out_norm, nq, nk, nv, nkc, nvc, na_rpb[0])

    return _ffn(h, bsz, mod4, 6, norm3, (w_gu2, w_dn2), tm=FFN_TM, attn=(5, oa, ob, w_out_b))
```

```python
import functools
import math

import numpy as np
import jax
import jax.numpy as jnp
from jax import lax
from jax.experimental import pallas as pl
from jax.experimental.pallas import tpu as pltpu

D_MODEL = 1024
GRID_W = 64
HEAD_DIM = 64
DIFF_HEADS = 4
NA_HEADS = 8
GROUP_W = 512
D_FF = 2816
N_MOD = 9
NA_KH = 8
NA_KW = 16
ROPE_BASE = 10000.0
NORM_EPS = 1e-6
NEG_INF = -1e30
LAM_INIT = 0.8 - 0.6 * math.exp(-0.3 * 0)
LOG2E = math.log2(math.e)

LANES = 128
MXU_W = 256
VMEM_LIMIT = 56 * 1024 * 1024

F32 = jnp.float32
BF16 = jnp.bfloat16


def _cparams(sem):
    return pltpu.CompilerParams(dimension_semantics=sem, vmem_limit_bytes=VMEM_LIMIT)


ADALN_TK = 128


def _adaln_kernel(cond_ref, w_ref, b_ref, o_ref):
    k = pl.program_id(0)
    c = cond_ref[...]
    a = (c * jax.nn.sigmoid(c)).astype(BF16)
    part = jnp.dot(a, w_ref[...].astype(BF16), preferred_element_type=F32)

    @pl.when(k == 0)
    def _():
        o_ref[...] = part + b_ref[...]

    @pl.when(k > 0)
    def _():
        o_ref[...] += part


def _adaln(cond, w_ada, b_ada):
    rows, d = cond.shape
    n = w_ada.shape[1]
    return pl.pallas_call(
        _adaln_kernel,
        grid=(d // ADALN_TK,),
        in_specs=[pl.BlockSpec((rows, ADALN_TK), lambda k: (0, k)),
                  pl.BlockSpec((ADALN_TK, n), lambda k: (k, 0)),
                  pl.BlockSpec((1, n), lambda k: (0, 0))],
        out_specs=pl.BlockSpec((rows, n), lambda k: (0, 0)),
        out_shape=jax.ShapeDtypeStruct((rows, n), F32),
        compiler_params=_cparams(("arbitrary",)),
        name="adaln",
    )(cond, w_ada, b_ada)


def _modulated_norm(x, g, shift, scale):
    ms = jnp.mean(x * x, axis=-1, keepdims=True)
    xn = x * lax.rsqrt(ms + NORM_EPS) * g
    return xn * (1.0 + scale) + shift


def _mod_spec(k, ctx):
    if ctx:
        return pl.BlockSpec((1, 1, 1, D_MODEL), lambda b, *_: (8, k, 0, 0))
    return pl.BlockSpec((1, 1, 1, D_MODEL), lambda b, *_: (b, k, 0, 0))


FFN_TF = 256


def _ffn_kernel(*refs, with_attn, n_cast):
    refs = list(refs)
    h_ref = refs.pop(0)
    if with_attn:
        x_ref = h_ref
        agate_ref, oa_ref, ob_ref, wo_ref = (refs.pop(0) for _ in range(4))
    shift_ref, scale_ref, gate_ref, g_ref, wgu_ref, wd_ref = (refs.pop(0) for _ in range(6))
    cast_in = [refs.pop(0) for _ in range(n_cast)]
    o_ref = refs.pop(0)
    cast_out = [refs.pop(0) for _ in range(n_cast)]
    xm_ref, acc_ref = refs[:2]
    for src, dst in zip(cast_in, cast_out):
        dst[...] = src[...].astype(BF16)
    if with_attn:
        h_ref = refs[2]
        half = oa_ref.shape[-1]
        attn = (jnp.dot(oa_ref[0], wo_ref[0:half, :], preferred_element_type=F32)
                + jnp.dot(ob_ref[0], wo_ref[half:, :], preferred_element_type=F32))
        h_ref[0] = x_ref[0] + agate_ref[0, 0] * attn
    f = wd_ref.shape[0]
    nj = f // FFN_TF
    xm_ref[...] = _modulated_norm(h_ref[0], g_ref[...], shift_ref[0, 0], scale_ref[0, 0]).astype(BF16)

    def act(j):
        xm = xm_ref[...]
        g = jnp.dot(xm, wgu_ref[:, j * FFN_TF:(j + 1) * FFN_TF], preferred_element_type=F32)
        u = jnp.dot(xm, wgu_ref[:, f + j * FFN_TF:f + (j + 1) * FFN_TF], preferred_element_type=F32)
        return (g * jax.nn.sigmoid(g) * u).astype(BF16)

    a = act(0)
    for j in range(nj):
        down = jnp.dot(a, wd_ref[j * FFN_TF:(j + 1) * FFN_TF, :], preferred_element_type=F32)
        if j + 1 < nj:
            a = act(j + 1)
        if j == 0:
            acc_ref[...] = down
        elif j + 1 < nj:
            acc_ref[...] += down
        else:
            o_ref[0] = h_ref[0] + (0.5 * gate_ref[0, 0]) * (acc_ref[...] + down)


def _ffn_weights(w_gu, w_down):
    return w_gu.astype(BF16), w_down.astype(BF16)


def _resident(shape):
    return pl.BlockSpec(shape, lambda *_: (0,) * len(shape), pipeline_mode=pl.Buffered(1))


def _ffn(h, mod4, k0, ctx, g, weights, *, tm, attn=None, casts=()):
    bsz, t, d = h.shape
    wgu, wd = weights
    assert wd.shape[0] % FFN_TF == 0
    n_i = t // tm
    tile = pl.BlockSpec((1, tm, d), lambda b, i: (b, i, 0))
    in_specs, args = [tile], [h]
    scratch = [pltpu.VMEM((tm, d), BF16), pltpu.VMEM((tm, d), F32)]
    if attn is not None:
        k_gate, oa, ob, w_out = attn
        half = oa.shape[-1]
        in_specs += [_mod_spec(k_gate, ctx), pl.BlockSpec((1, tm, half), lambda b, i: (b, i, 0)),
                     pl.BlockSpec((1, tm, half), lambda b, i: (b, i, 0)), _resident(w_out.shape)]
        args += [mod4, oa, ob, w_out]
        scratch.append(pltpu.VMEM((1, tm, d), F32))
    in_specs += [_mod_spec(k0, ctx), _mod_spec(k0 + 1, ctx), _mod_spec(k0 + 2, ctx),
                 pl.BlockSpec((1, d), lambda b, i: (0, 0)), _resident(wgu.shape), _resident(wd.shape)]
    args += [mod4, mod4, mod4, g, wgu, wd]
    slabs = []
    for w in casts:
        rows_per_step = w.shape[0] // (bsz * n_i)
        assert rows_per_step * bsz * n_i == w.shape[0] and rows_per_step % 16 == 0
        slabs.append(pl.BlockSpec((rows_per_step, w.shape[1]), lambda b, i: (b * n_i + i, 0)))
    out = pl.pallas_call(
        functools.partial(_ffn_kernel, with_attn=attn is not None, n_cast=len(casts)),
        grid=(bsz, n_i),
        in_specs=in_specs + slabs,
        out_specs=[tile] + slabs,
        out_shape=[jax.ShapeDtypeStruct(h.shape, F32)] + [jax.ShapeDtypeStruct(w.shape, BF16) for w in casts],
        scratch_shapes=scratch,
        compiler_params=_cparams(("parallel", "parallel")),
        name="ffn_ctx" if ctx else "ffn",
    )(*args, *casts)
    return out if casts else out[0]


def _head_rms_scale(y, bd_ref):
    y2 = (y * y).astype(BF16)
    bd = bd_ref[...]
    w = bd.shape[0]
    ss = jnp.concatenate(
        [jnp.dot(y2[:, c:c + w], bd, preferred_element_type=F32) for c in range(0, y.shape[1], w)], axis=-1)
    return lax.rsqrt(ss * (1.0 / HEAD_DIM) + NORM_EPS)


def _rope(z, cos, sin):
    lane = lax.broadcasted_iota(jnp.int32, z.shape, 1)
    up = pltpu.roll(z, LANES - 16, 1)
    dn = pltpu.roll(z, 16, 1)
    return z * cos + jnp.where((lane & 16) == 0, up, dn) * sin


def _in_proj_kernel(*refs, groups, n_gain):
    h_ref, shift_ref, scale_ref, g_ref, w_ref, bd_ref, cos_ref, sin_ref = refs[:8]
    gain_refs = refs[8:8 + n_gain]
    out_refs = refs[8 + n_gain:]
    xm = _modulated_norm(h_ref[0], g_ref[...], shift_ref[0, 0], scale_ref[0, 0]).astype(BF16)
    for (col, gain_idx, rope, qscale), o_ref in zip(groups, out_refs):
        y = jnp.dot(xm, w_ref[:, col * GROUP_W:(col + 1) * GROUP_W], preferred_element_type=F32)
        if gain_idx is not None:
            gain = gain_refs[gain_idx][...]
            if qscale != 1.0:
                gain = gain * qscale
            y = y * _head_rms_scale(y, bd_ref) * gain
        if rope:
            cos = cos_ref[...]
            sin = sin_ref[...]
            for c in range(GROUP_W // LANES):
                sl = slice(c * LANES, (c + 1) * LANES)
                o_ref[0, :, sl] = _rope(y[:, sl], cos, sin).astype(BF16)
        else:
            o_ref[0] = y.astype(BF16)


def _in_proj(h, mod4, ctx, g, w_in, bd, cos, sin, gains, groups, *, tm):
    bsz, t, d = h.shape
    n_out = len(groups)
    in_specs = [pl.BlockSpec((1, tm, d), lambda b, i: (b, i, 0)),
                _mod_spec(3, ctx), _mod_spec(4, ctx),
                pl.BlockSpec((1, d), lambda b, i: (0, 0)),
                pl.BlockSpec(w_in.shape, lambda b, i: (0, 0)),
                pl.BlockSpec(bd.shape, lambda b, i: (0, 0)),
                pl.BlockSpec((tm, LANES), lambda b, i: (i, 0)),
                pl.BlockSpec((tm, LANES), lambda b, i: (i, 0))]
    in_specs += [pl.BlockSpec((1, GROUP_W), lambda b, i: (0, 0)) for _ in gains]
    return pl.pallas_call(
        functools.partial(_in_proj_kernel, groups=groups, n_gain=len(gains)),
        grid=(bsz, t // tm),
        in_specs=in_specs,
        out_specs=[pl.BlockSpec((1, tm, GROUP_W), lambda b, i: (b, i, 0)) for _ in range(n_out)],
        out_shape=[jax.ShapeDtypeStruct((bsz, t, GROUP_W), BF16) for _ in range(n_out)],
        compiler_params=_cparams(("parallel", "parallel")),
        name="in_proj_ctx" if ctx else "in_proj",
    )(h, mod4, mod4, g, w_in, bd, cos, sin, *gains)


def _split_halves(q):
    qt = q.astype(F32).T
    row = lax.broadcasted_iota(jnp.int32, qt.shape, 0)
    lo = jnp.where(row < HEAD_DIM, qt, 0.0).astype(BF16)
    hi = jnp.where(row >= HEAD_DIM, qt, 0.0).astype(BF16)
    return lo, hi


DIFF_TQ = 256
DIFF_TK = 256
SUBLANES = 8
SCORE_PAD = 8


def _interleave(*stages):
    order = sorted(((k + 0.5) / len(st), n, k) for n, st in enumerate(stages) for k in range(len(st)))
    for _, n, k in order:
        stages[n][k]()


def _diff_pipeline(q_ref, kc_ref, kl_ref, vc_ref, vl_ref, lamv_ref, gcol_ref, o_ref,
                   kk_ref, vte_ref, qm_ref, s_ref, m_ref):
    n_ctx = kc_ref.shape[1]
    n_tick = kk_ref.shape[0] // DIFF_TK
    n_blk = q_ref.shape[1] // DIFF_TQ
    kk_ref[0:n_ctx] = kc_ref[0]
    kk_ref[n_ctx:] = kl_ref[0]
    for kc in range(n_tick):
        lo = kc * DIFF_TK
        v = vc_ref[0, lo:lo + DIFF_TK] if lo < n_ctx else vl_ref[0, lo - n_ctx:lo - n_ctx + DIFF_TK]
        vte_ref[kc] = v.astype(F32).T.astype(BF16)
    lv = lamv_ref[...]
    lam = (jnp.exp(jnp.sum(lv[0:1] * lv[1:2], axis=-1, keepdims=True))
           - jnp.exp(jnp.sum(lv[2:3] * lv[3:4], axis=-1, keepdims=True)) + LAM_INIT)

    def rows(i):
        return pl.ds(pl.multiple_of(i * DIFF_TQ, DIFF_TQ), DIFF_TQ)

    def keys(kc):
        return slice(kc * DIFF_TK, (kc + 1) * DIFF_TK)

    def prep(i, slot):
        lo, hi = _split_halves(q_ref[0, rows(i), :])
        qm_ref[slot, 0] = lo
        qm_ref[slot, 1] = hi
        m_ref[slot] = jnp.full(m_ref.shape[1:], -jnp.inf, F32)

    def score_tick(slot, kc):
        k = kk_ref[keys(kc), :]
        for c in range(2):
            s = jnp.dot(k, qm_ref[slot, c], preferred_element_type=F32)
            s_ref[slot, c, keys(kc), :] = s
            part = jnp.max(s.reshape(DIFF_TK // SUBLANES, SUBLANES, DIFF_TQ), axis=0)
            m_ref[slot, c] = jnp.maximum(m_ref[slot, c], part)

    def final_max(slot):
        return [jnp.broadcast_to(jnp.max(m_ref[slot, c], axis=0, keepdims=True), (SUBLANES, DIFF_TQ))
                for c in range(2)]

    def pv_tick(slot, kc, mx, acc):
        for c in range(2):
            s = s_ref[slot, c, keys(kc), :].reshape(DIFF_TK // SUBLANES, SUBLANES, DIFF_TQ)
            p = jnp.exp2(s - mx[c])
            d = [jnp.dot(vte_ref[kc], p.reshape(DIFF_TK, DIFF_TQ).astype(BF16), preferred_element_type=F32),
                 jnp.sum(p, axis=0)]
            acc[c] = d if acc[c] is None else [a + b for a, b in zip(acc[c], d)]

    def finish(i, acc):
        l0 = jnp.sum(acc[0][1], axis=0, keepdims=True)
        l1 = jnp.sum(acc[1][1], axis=0, keepdims=True)
        ot = acc[0][0] * (1.0 / l0) - acc[1][0] * (lam / l1)
        ms = jnp.mean(ot * ot, axis=0, keepdims=True)
        on = ot * lax.rsqrt(ms + NORM_EPS) * (gcol_ref[...] * (1.0 - LAM_INIT))
        o_ref[0, rows(i), :] = on.T.astype(BF16)

    def start():
        return ([functools.partial(prep, 0, 0)]
                + [functools.partial(score_tick, 0, kc) for kc in range(n_tick)])

    def block(i, slot, last=False):
        state = {}

        def head():
            state["mx"] = final_max(slot)
            state["acc"] = [None, None]
            if not last:
                prep(i + 1, 1 - slot)

        def tick(kc):
            pv_tick(slot, kc, state["mx"], state["acc"])
            if not last:
                score_tick(1 - slot, kc)

        return ([head] + [functools.partial(tick, kc) for kc in range(n_tick)]
                + [lambda: finish(i, state["acc"])])

    return start, block


def _diff_scratch(c, s):
    return [pltpu.VMEM((c + s, LANES), BF16),
            pltpu.VMEM(((c + s) // DIFF_TK, LANES, DIFF_TK), BF16),
            pltpu.VMEM((2, 2, LANES, DIFF_TQ), BF16),
            pltpu.VMEM((2, 2, c + s + SCORE_PAD, DIFF_TQ), F32),
            pltpu.VMEM((2, 2, SUBLANES, DIFF_TQ), F32)]


NA_QROWS = 4
NA_KROWS = 12
NA_TQ = NA_QROWS * GRID_W
NA_TK = 4 * GRID_W
NA_WIN_TICKS = NA_KROWS * GRID_W // NA_TK


def _na_variants(rows):
    n_rb = rows // NA_QROWS
    out = []
    for rb in (0, 1, n_rb - 1):
        ws = min(max(rb * NA_QROWS - NA_KH // 2, 0), rows - NA_KROWS)
        out.append((rb * NA_QROWS, ws))
    return out


def _na_build_bias(rpb_ref, bias_ref, rows):
    shape = (GRID_W, LANES)
    kc = lax.broadcasted_iota(jnp.int32, shape, 0)
    lane = lax.broadcasted_iota(jnp.int32, shape, 1)
    qc = lane & (GRID_W - 1)
    cs = jnp.clip(qc - NA_KW // 2, 0, GRID_W - NA_KW)
    col_ok = (kc >= cs) & (kc < cs + NA_KW)
    side_ok = {(True, True): col_ok, (True, False): col_ok & (lane < GRID_W),
               (False, True): col_ok & (lane >= GRID_W)}
    neg = jnp.full(shape, NEG_INF, F32)
    for hh in range(2):
        toeplitz = {}
        for v, (r0, ws) in enumerate(_na_variants(rows)):
            for i in range(NA_KROWS):
                kr = ws + i
                for jp in range(NA_QROWS // 2):
                    ok = []
                    for r in (r0 + 2 * jp, r0 + 2 * jp + 1):
                        rs = min(max(r - NA_KH // 2, 0), rows - NA_KH)
                        ok.append(rs <= kr < rs + NA_KH)
                    tile = neg
                    if any(ok):
                        e = kr - (r0 + 2 * jp) + NA_KH - 1
                        if e not in toeplitz:
                            vec = jnp.broadcast_to(rpb_ref[0, hh, e:e + 1, :] * LOG2E, shape)
                            toeplitz[e] = pltpu.roll(vec, LANES - (NA_KW - 1), 1, stride=1, stride_axis=0)
                        tile = jnp.where(side_ok[tuple(ok)], toeplitz[e], neg)
                    bias_ref[hh, v, i * GRID_W:(i + 1) * GRID_W, jp * LANES:(jp + 1) * LANES] = tile


def _na_pipeline(q_ref, kc_ref, k_ref, vc_ref, v_ref, rpb_ref, o_ref,
                 bias_ref, vte_ref, qm_ref, s_ref, m_ref, *, rows):
    n_blk = rows // NA_QROWS
    n_lat = v_ref.shape[1] // NA_TK
    assert kc_ref.shape[1] == NA_TK
    n_tick = NA_WIN_TICKS + 1

    @pl.when(pl.program_id(1) == 0)
    def _():
        _na_build_bias(rpb_ref, bias_ref, rows)

    for ch in range(n_lat + 1):
        v = v_ref[0, ch * NA_TK:(ch + 1) * NA_TK] if ch < n_lat else vc_ref[0]
        vt = v.astype(F32).T
        for hh in range(2):
            vte_ref[hh, ch] = vt[hh * HEAD_DIM:(hh + 1) * HEAD_DIM].astype(BF16)

    def rows_of(i):
        return pl.ds(pl.multiple_of(i * NA_TQ, NA_TQ), NA_TQ)

    def window(i):
        return (jnp.clip(i - 1, 0, n_lat - NA_WIN_TICKS),
                jnp.where(i == 0, 0, jnp.where(i == n_blk - 1, 2, 1)))

    def prep(i, slot):
        lo, hi = _split_halves(q_ref[0, rows_of(i), :])
        qm_ref[slot, 0] = lo
        qm_ref[slot, 1] = hi
        m_ref[slot] = jnp.full(m_ref.shape[1:], -jnp.inf, F32)

    def score_tick(slot, i, t):
        ch0, variant = window(i)
        if t < NA_WIN_TICKS:
            k = k_ref[0, pl.ds(pl.multiple_of((ch0 + t) * NA_TK, NA_TK), NA_TK), :]
        else:
            k = kc_ref[0]
        for hh in range(2):
            s = jnp.dot(k, qm_ref[slot, hh], preferred_element_type=F32)
            if t < NA_WIN_TICKS:
                s = s + bias_ref[hh, variant, t * NA_TK:(t + 1) * NA_TK, :]
            s_ref[slot, hh, t * NA_TK:(t + 1) * NA_TK, :] = s
            part = jnp.max(s.reshape(NA_TK // SUBLANES, SUBLANES, NA_TQ), axis=0)
            m_ref[slot, hh] = jnp.maximum(m_ref[slot, hh], part)

    def final_max(slot):
        return [jnp.broadcast_to(jnp.max(m_ref[slot, hh], axis=0, keepdims=True), (SUBLANES, NA_TQ))
                for hh in range(2)]

    def pv_tick(slot, i, t, mx, acc):
        ch = window(i)[0] + t if t < NA_WIN_TICKS else n_lat
        for hh in range(2):
            s = s_ref[slot, hh, t * NA_TK:(t + 1) * NA_TK, :].reshape(NA_TK // SUBLANES, SUBLANES, NA_TQ)
            p = jnp.exp2(s - mx[hh])
            d = [jnp.dot(vte_ref[hh, ch], p.reshape(NA_TK, NA_TQ).astype(BF16), preferred_element_type=F32),
                 jnp.sum(p, axis=0)]
            acc[hh] = d if acc[hh] is None else [a + b for a, b in zip(acc[hh], d)]

    def finish(i, acc):
        ot = jnp.concatenate(
            [acc[hh][0] * (1.0 / jnp.sum(acc[hh][1], axis=0, keepdims=True)) for hh in range(2)],
            axis=0)
        o_ref[0, rows_of(i), :] = ot.T.astype(BF16)

    def start():
        return ([functools.partial(prep, 0, 0)]
                + [functools.partial(score_tick, 0, 0, t) for t in range(n_tick)])

    def block(i, slot, last=False):
        state = {}

        def head():
            state["mx"] = final_max(slot)
            state["acc"] = [None, None]
            if not last:
                prep(i + 1, 1 - slot)

        def tick(t):
            pv_tick(slot, i, t, state["mx"], state["acc"])
            if not last:
                score_tick(1 - slot, i + 1, t)

        return ([head] + [functools.partial(tick, t) for t in range(n_tick)]
                + [lambda: finish(i, state["acc"])])

    return start, block


def _na_scratch(n_lat):
    return [pltpu.VMEM((2, 3, NA_KROWS * GRID_W, NA_TQ), F32),
            pltpu.VMEM((2, n_lat + 1, HEAD_DIM, NA_TK), BF16),
            pltpu.VMEM((2, 2, LANES, NA_TQ), BF16),
            pltpu.VMEM((2, 2, (NA_WIN_TICKS + 1) * NA_TK + SCORE_PAD, NA_TQ), F32),
            pltpu.VMEM((2, 2, SUBLANES, NA_TQ), F32)]


N_DIFF_IN, N_NA_IN = 7, 6


def _attn_kernel(*refs, rows):
    d_in = refs[:N_DIFF_IN]
    n_in = refs[N_DIFF_IN:N_DIFF_IN + N_NA_IN]
    oa_ref, ob_ref = refs[N_DIFF_IN + N_NA_IN:N_DIFF_IN + N_NA_IN + 2]
    scratch = refs[N_DIFF_IN + N_NA_IN + 2:]
    d_start, d_block = _diff_pipeline(*d_in, oa_ref, *scratch[:5])
    n_start, n_block = _na_pipeline(*n_in, ob_ref, *scratch[5:], rows=rows)
    n_blk = rows // NA_QROWS
    _interleave(d_start(), n_start())

    def pair(u, carry):
        for j in range(2):
            _interleave(d_block(2 * u + j, j), n_block(2 * u + j, j))
        return carry

    lax.fori_loop(0, n_blk // 2 - 1, pair, 0)
    _interleave(d_block(n_blk - 2, 0), n_block(n_blk - 2, 0))
    _interleave(d_block(n_blk - 1, 1, last=True), n_block(n_blk - 1, 1, last=True))


def _attention(aq, ak, av, akc, avc, lamv, gout, nq, nk, nv, nkc, nvc, rpb):
    bsz, s, _ = aq.shape
    c = akc.shape[1]
    rows = s // GRID_W
    n_blk = rows // NA_QROWS
    assert NA_HEADS // 2 == DIFF_HEADS and NA_TQ == DIFF_TQ and NA_QROWS * GRID_W == NA_TK and c == NA_TK
    assert n_blk >= 4 and n_blk % 2 == 0 and c % DIFF_TK == 0 and s % DIFF_TK == 0
    n_r, n_c = rpb.shape[1:]
    padded = jnp.pad(rpb[:, :, ::-1], ((0, 0), (1, 1), (0, GRID_W - n_c)))
    rpb2 = jnp.concatenate([padded[:, 1:], padded[:, :-1]], axis=-1).reshape(DIFF_HEADS, 2, n_r + 1, LANES)
    lat = pl.BlockSpec((1, s, LANES), lambda h, b: (b, 0, h))
    ctx = pl.BlockSpec((1, c, LANES), lambda h, b: (b, 0, h))
    return pl.pallas_call(
        functools.partial(_attn_kernel, rows=rows),
        grid=(DIFF_HEADS, bsz),
        in_specs=[lat, ctx, lat, ctx, lat,
                  pl.BlockSpec(lamv.shape, lambda h, b: (0, 0)),
                  pl.BlockSpec((LANES, 1), lambda h, b: (0, 0)),
                  lat, ctx, lat, ctx, lat,
                  pl.BlockSpec((1,) + rpb2.shape[1:], lambda h, b: (h, 0, 0, 0))],
        out_specs=[lat, lat],
        out_shape=[jax.ShapeDtypeStruct(aq.shape, BF16), jax.ShapeDtypeStruct(nq.shape, BF16)],
        scratch_shapes=_diff_scratch(c, s) + _na_scratch(s // NA_TK),
        compiler_params=_cparams(("arbitrary", "arbitrary")),
        name="attention",
    )(aq, akc, ak, avc, av, lamv, gout.reshape(LANES, 1), nq, nkc, nk, nvc, nv, rpb2)


def _rope_tables(n_tokens):
    t = jnp.arange(n_tokens, dtype=jnp.int32)
    row = (t // GRID_W).astype(F32)
    col = (t % GRID_W).astype(F32)
    n_freq = HEAD_DIM // 4
    inv_freq = ROPE_BASE ** (-jnp.arange(n_freq, dtype=F32) / n_freq)
    ar = row[:, None] * inv_freq
    ac = col[:, None] * inv_freq
    cos = jnp.concatenate([jnp.cos(ar), jnp.cos(ar), jnp.cos(ac), jnp.cos(ac)], axis=-1)
    sin = jnp.concatenate([-jnp.sin(ar), jnp.sin(ar), -jnp.sin(ac), jnp.sin(ac)], axis=-1)
    reps = LANES // HEAD_DIM
    return jnp.tile(cos, (1, reps)), jnp.tile(sin, (1, reps))


def kernel(x, c, ctx, c_ctx, w_ada, b_ada, norm1, norm2, norm3, ffn1_w_gu, ffn1_w_down, w_in,
           diff_q_norm, diff_k_norm, lam_q1, lam_k1, lam_q2, lam_k2, diff_out_norm,
           na_q_norm, na_k_norm, na_rpb, w_out, ffn2_w_gu, ffn2_w_down):
    bsz, s, d = x.shape
    assert w_ada.shape[0] == 1 and d == D_MODEL and s % (NA_QROWS * GRID_W) == 0
    rows = s // GRID_W

    cond = jnp.zeros((16, d), F32).at[:bsz].set(c).at[bsz].set(c_ctx)
    mod4 = _adaln(cond, w_ada[0], b_ada).reshape(16, N_MOD, 1, d)

    ffn1_w = _ffn_weights(ffn1_w_gu[0], ffn1_w_down[0])
    h, w_in_b, w_out_b, w_gu2, w_dn2 = _ffn(x, mod4, 0, False, norm1, ffn1_w, tm=1024,
                                            casts=(w_in[0], w_out[0], ffn2_w_gu[0], ffn2_w_down[0]))
    n_ctx = ctx.shape[1]
    hc = _ffn(ctx.reshape(1, bsz * n_ctx, d), mod4, 0, True, norm1, ffn1_w, tm=1024)

    seg = np.arange(MXU_W) // HEAD_DIM
    bd = jnp.asarray(seg[:, None] == seg[None, :], BF16)
    cos, sin = _rope_tables(s)
    reps = GROUP_W // HEAD_DIM
    gains = [jnp.tile(g, (1, reps)) for g in (diff_q_norm, diff_k_norm, na_q_norm, na_k_norm)]
    qs = HEAD_DIM ** -0.5
    lat_groups = ((0, 0, True, qs * LOG2E), (1, 1, True, 1.0), (2, None, False, 1.0),
                  (3, 2, False, qs * LOG2E), (4, 3, False, 1.0), (5, None, False, 1.0))
    aq, ak, av, nq, nk, nv = _in_proj(h, mod4, False, norm2, w_in_b, bd, cos, sin, gains,
                                      lat_groups, tm=1024)
    ctx_groups = ((1, 1, False, 1.0), (2, None, False, 1.0), (4, 3, False, 1.0), (5, None, False, 1.0))
    ctx_kv = _in_proj(hc, mod4, True, norm2, w_in_b, bd, cos, sin, gains, ctx_groups, tm=512)
    akc, avc, nkc, nvc = (a.reshape(bsz, n_ctx, GROUP_W) for a in ctx_kv)

    lamv = jnp.concatenate([lam_q1, lam_k1, lam_q2, lam_k2], axis=0)
    oa, ob = _attention(aq, ak, av, akc, avc, lamv, diff_out_norm, nq, nk, nv, nkc, nvc, na_rpb[0])

    return _ffn(h, mod4, 6, False, norm3, (w_gu2, w_dn2), tm=1024, attn=(5, oa, ob, w_out_b))
```

```python
import functools
import math

import numpy as np
import jax
import jax.numpy as jnp
from jax import lax
from jax.experimental import pallas as pl
from jax.experimental.pallas import tpu as pltpu

D_MODEL = 1024
GRID_W = 64
HEAD_DIM = 64
DIFF_HEADS = 4
NA_HEADS = 8
GROUP_W = 512
D_FF = 2816
N_MOD = 9
NA_KH = 8
NA_KW = 16
ROPE_BASE = 10000.0
NORM_EPS = 1e-6
NEG_INF = -1e30
LAM_INIT = 0.8 - 0.6 * math.exp(-0.3 * 0)
LOG2E = math.log2(math.e)

LANES = 128
MXU_W = 256
VMEM_LIMIT = 56 * 1024 * 1024

F32 = jnp.float32
BF16 = jnp.bfloat16


def _cparams(sem):
    return pltpu.CompilerParams(dimension_semantics=sem, vmem_limit_bytes=VMEM_LIMIT)


ADALN_TK = 128


def _adaln_kernel(cond_ref, w_ref, b_ref, o_ref):
    k = pl.program_id(0)
    c = cond_ref[...]
    a = (c * jax.nn.sigmoid(c)).astype(BF16)
    part = jnp.dot(a, w_ref[...].astype(BF16), preferred_element_type=F32)

    @pl.when(k == 0)
    def _():
        o_ref[...] = part + b_ref[...]

    @pl.when(k > 0)
    def _():
        o_ref[...] += part


def _adaln(cond, w_ada, b_ada):
    rows, d = cond.shape
    n = w_ada.shape[1]
    return pl.pallas_call(
        _adaln_kernel,
        grid=(d // ADALN_TK,),
        in_specs=[pl.BlockSpec((rows, ADALN_TK), lambda k: (0, k)),
                  pl.BlockSpec((ADALN_TK, n), lambda k: (k, 0)),
                  pl.BlockSpec((1, n), lambda k: (0, 0))],
        out_specs=pl.BlockSpec((rows, n), lambda k: (0, 0)),
        out_shape=jax.ShapeDtypeStruct((rows, n), F32),
        compiler_params=_cparams(("arbitrary",)),
        name="adaln",
    )(cond, w_ada, b_ada)


def _modulated_norm(x, g, shift, scale):
    ms = jnp.mean(x * x, axis=-1, keepdims=True)
    xn = x * lax.rsqrt(ms + NORM_EPS) * g
    return xn * (1.0 + scale) + shift


def _mod_spec(k, ctx):
    if ctx:
        return pl.BlockSpec((1, 1, 1, D_MODEL), lambda b, *_: (8, k, 0, 0))
    return pl.BlockSpec((1, 1, 1, D_MODEL), lambda b, *_: (b, k, 0, 0))


FFN_TF = 256


def _ffn_kernel(*refs, with_attn, n_cast):
    refs = list(refs)
    h_ref = refs.pop(0)
    if with_attn:
        x_ref = h_ref
        agate_ref, oa_ref, ob_ref, wo_ref = (refs.pop(0) for _ in range(4))
    shift_ref, scale_ref, gate_ref, g_ref, wgu_ref, wd_ref = (refs.pop(0) for _ in range(6))
    cast_in = [refs.pop(0) for _ in range(n_cast)]
    o_ref = refs.pop(0)
    cast_out = [refs.pop(0) for _ in range(n_cast)]
    xm_ref, acc_ref = refs[:2]
    for src, dst in zip(cast_in, cast_out):
        dst[...] = src[...].astype(BF16)
    if with_attn:
        h_ref = refs[2]
        half = oa_ref.shape[-1]
        attn = (jnp.dot(oa_ref[0], wo_ref[0:half, :], preferred_element_type=F32)
                + jnp.dot(ob_ref[0], wo_ref[half:, :], preferred_element_type=F32))
        h_ref[0] = x_ref[0] + agate_ref[0, 0] * attn
    f = wd_ref.shape[0]
    nj = f // FFN_TF
    xm_ref[...] = _modulated_norm(h_ref[0], g_ref[...], shift_ref[0, 0], scale_ref[0, 0]).astype(BF16)

    def act(j):
        xm = xm_ref[...]
        g = jnp.dot(xm, wgu_ref[:, j * FFN_TF:(j + 1) * FFN_TF], preferred_element_type=F32)
        u = jnp.dot(xm, wgu_ref[:, f + j * FFN_TF:f + (j + 1) * FFN_TF], preferred_element_type=F32)
        return (g * jax.nn.sigmoid(g) * u).astype(BF16)

    a = act(0)
    for j in range(nj):
        down = jnp.dot(a, wd_ref[j * FFN_TF:(j + 1) * FFN_TF, :], preferred_element_type=F32)
        if j + 1 < nj:
            a = act(j + 1)
        if j == 0:
            acc_ref[...] = down
        elif j + 1 < nj:
            acc_ref[...] += down
        else:
            o_ref[0] = h_ref[0] + (0.5 * gate_ref[0, 0]) * (acc_ref[...] + down)


def _ffn_weights(w_gu, w_down):
    return w_gu.astype(BF16), w_down.astype(BF16)


def _resident(shape):
    return pl.BlockSpec(shape, lambda *_: (0,) * len(shape), pipeline_mode=pl.Buffered(1))


def _ffn(h, mod4, k0, ctx, g, weights, *, tm, attn=None, casts=()):
    bsz, t, d = h.shape
    wgu, wd = weights
    assert wd.shape[0] % FFN_TF == 0
    n_i = t // tm
    tile = pl.BlockSpec((1, tm, d), lambda b, i: (b, i, 0))
    in_specs, args = [tile], [h]
    scratch = [pltpu.VMEM((tm, d), BF16), pltpu.VMEM((tm, d), F32)]
    if attn is not None:
        k_gate, oa, ob, w_out = attn
        half = oa.shape[-1]
        in_specs += [_mod_spec(k_gate, ctx), pl.BlockSpec((1, tm, half), lambda b, i: (b, i, 0)),
                     pl.BlockSpec((1, tm, half), lambda b, i: (b, i, 0)), _resident(w_out.shape)]
        args += [mod4, oa, ob, w_out]
        scratch.append(pltpu.VMEM((1, tm, d), F32))
    in_specs += [_mod_spec(k0, ctx), _mod_spec(k0 + 1, ctx), _mod_spec(k0 + 2, ctx),
                 pl.BlockSpec((1, d), lambda b, i: (0, 0)), _resident(wgu.shape), _resident(wd.shape)]
    args += [mod4, mod4, mod4, g, wgu, wd]
    slabs = []
    for w in casts:
        rows_per_step = w.shape[0] // (bsz * n_i)
        assert rows_per_step * bsz * n_i == w.shape[0] and rows_per_step % 16 == 0
        slabs.append(pl.BlockSpec((rows_per_step, w.shape[1]), lambda b, i: (b * n_i + i, 0)))
    out = pl.pallas_call(
        functools.partial(_ffn_kernel, with_attn=attn is not None, n_cast=len(casts)),
        grid=(bsz, n_i),
        in_specs=in_specs + slabs,
        out_specs=[tile] + slabs,
        out_shape=[jax.ShapeDtypeStruct(h.shape, F32)] + [jax.ShapeDtypeStruct(w.shape, BF16) for w in casts],
        scratch_shapes=scratch,
        compiler_params=_cparams(("parallel", "parallel")),
        name="ffn_ctx" if ctx else "ffn",
    )(*args, *casts)
    return out if casts else out[0]


def _head_rms_scale(y, bd_ref):
    y2 = (y * y).astype(BF16)
    bd = bd_ref[...]
    w = bd.shape[0]
    ss = jnp.concatenate(
        [jnp.dot(y2[:, c:c + w], bd, preferred_element_type=F32) for c in range(0, y.shape[1], w)], axis=-1)
    return lax.rsqrt(ss * (1.0 / HEAD_DIM) + NORM_EPS)


def _rope(z, cos, sin):
    lane = lax.broadcasted_iota(jnp.int32, z.shape, 1)
    up = pltpu.roll(z, LANES - 16, 1)
    dn = pltpu.roll(z, 16, 1)
    return z * cos + jnp.where((lane & 16) == 0, up, dn) * sin


def _in_proj_kernel(*refs, groups, n_gain):
    h_ref, shift_ref, scale_ref, g_ref, w_ref, bd_ref, cos_ref, sin_ref = refs[:8]
    gain_refs = refs[8:8 + n_gain]
    out_refs = refs[8 + n_gain:]
    xm = _modulated_norm(h_ref[0], g_ref[...], shift_ref[0, 0], scale_ref[0, 0]).astype(BF16)
    for (col, gain_idx, rope, qscale), o_ref in zip(groups, out_refs):
        y = jnp.dot(xm, w_ref[:, col * GROUP_W:(col + 1) * GROUP_W], preferred_element_type=F32)
        if gain_idx is not None:
            gain = gain_refs[gain_idx][...]
            if qscale != 1.0:
                gain = gain * qscale
            y = y * _head_rms_scale(y, bd_ref) * gain
        if rope:
            cos = cos_ref[...]
            sin = sin_ref[...]
            for c in range(GROUP_W // LANES):
                sl = slice(c * LANES, (c + 1) * LANES)
                o_ref[0, :, sl] = _rope(y[:, sl], cos, sin).astype(BF16)
        else:
            o_ref[0] = y.astype(BF16)


def _in_proj(h, mod4, ctx, g, w_in, bd, cos, sin, gains, groups, *, tm):
    bsz, t, d = h.shape
    n_out = len(groups)
    in_specs = [pl.BlockSpec((1, tm, d), lambda b, i: (b, i, 0)),
                _mod_spec(3, ctx), _mod_spec(4, ctx),
                pl.BlockSpec((1, d), lambda b, i: (0, 0)),
                pl.BlockSpec(w_in.shape, lambda b, i: (0, 0)),
                pl.BlockSpec(bd.shape, lambda b, i: (0, 0)),
                pl.BlockSpec((tm, LANES), lambda b, i: (i, 0)),
                pl.BlockSpec((tm, LANES), lambda b, i: (i, 0))]
    in_specs += [pl.BlockSpec((1, GROUP_W), lambda b, i: (0, 0)) for _ in gains]
    return pl.pallas_call(
        functools.partial(_in_proj_kernel, groups=groups, n_gain=len(gains)),
        grid=(bsz, t // tm),
        in_specs=in_specs,
        out_specs=[pl.BlockSpec((1, tm, GROUP_W), lambda b, i: (b, i, 0)) for _ in range(n_out)],
        out_shape=[jax.ShapeDtypeStruct((bsz, t, GROUP_W), BF16) for _ in range(n_out)],
        compiler_params=_cparams(("parallel", "parallel")),
        name="in_proj_ctx" if ctx else "in_proj",
    )(h, mod4, mod4, g, w_in, bd, cos, sin, *gains)


def _split_halves(q):
    qt = q.astype(F32).T
    row = lax.broadcasted_iota(jnp.int32, qt.shape, 0)
    lo = jnp.where(row < HEAD_DIM, qt, 0.0).astype(BF16)
    hi = jnp.where(row >= HEAD_DIM, qt, 0.0).astype(BF16)
    return lo, hi


DIFF_TQ = 256
DIFF_TK = 256
SUBLANES = 8


def _interleave(*stages):
    order = sorted(((k + 0.5) / len(st), n, k) for n, st in enumerate(stages) for k in range(len(st)))
    for _, n, k in order:
        stages[n][k]()


def _diff_pipeline(q_ref, kc_ref, kl_ref, vc_ref, vl_ref, lamv_ref, gcol_ref, o_ref,
                   kk_ref, vte_ref, qm_ref, s_ref, m_ref):
    n_ctx = kc_ref.shape[1]
    n_tick = kk_ref.shape[0] // DIFF_TK
    n_blk = q_ref.shape[1] // DIFF_TQ
    kk_ref[0:n_ctx] = kc_ref[0]
    kk_ref[n_ctx:] = kl_ref[0]

    def stage_v(kc):
        lo = kc * DIFF_TK
        v = vc_ref[0, lo:lo + DIFF_TK] if lo < n_ctx else vl_ref[0, lo - n_ctx:lo - n_ctx + DIFF_TK]
        vte_ref[kc] = v.astype(F32).T.astype(BF16)

    lv = lamv_ref[...]
    lam = (jnp.exp(jnp.sum(lv[0:1] * lv[1:2], axis=-1, keepdims=True))
           - jnp.exp(jnp.sum(lv[2:3] * lv[3:4], axis=-1, keepdims=True)) + LAM_INIT)

    def rows(i):
        return pl.ds(pl.multiple_of(i * DIFF_TQ, DIFF_TQ), DIFF_TQ)

    def keys(kc):
        return slice(kc * DIFF_TK, (kc + 1) * DIFF_TK)

    def prep(i, slot):
        lo, hi = _split_halves(q_ref[0, rows(i), :])
        qm_ref[slot, 0] = lo
        qm_ref[slot, 1] = hi
        m_ref[slot] = jnp.full(m_ref.shape[1:], -jnp.inf, F32)

    def score_tick(slot, kc):
        k = kk_ref[keys(kc), :]
        for c in range(2):
            s = jnp.dot(k, qm_ref[slot, c], preferred_element_type=F32)
            s_ref[slot, c, keys(kc), :] = s
            part = jnp.max(s.reshape(DIFF_TK // SUBLANES, SUBLANES, DIFF_TQ), axis=0)
            m_ref[slot, c] = jnp.maximum(m_ref[slot, c], part)

    def final_max(slot):
        return [jnp.broadcast_to(jnp.max(m_ref[slot, c], axis=0, keepdims=True), (SUBLANES, DIFF_TQ))
                for c in range(2)]

    def pv_tick(slot, kc, mx, acc):
        for c in range(2):
            s = s_ref[slot, c, keys(kc), :].reshape(DIFF_TK // SUBLANES, SUBLANES, DIFF_TQ)
            p = jnp.exp2(s - mx[c])
            d = [jnp.dot(vte_ref[kc], p.reshape(DIFF_TK, DIFF_TQ).astype(BF16), preferred_element_type=F32),
                 jnp.sum(p, axis=0)]
            acc[c] = d if acc[c] is None else [a + b for a, b in zip(acc[c], d)]

    def finish(i, acc):
        l0 = jnp.sum(acc[0][1], axis=0, keepdims=True)
        l1 = jnp.sum(acc[1][1], axis=0, keepdims=True)
        ot = acc[0][0] * (1.0 / l0) - acc[1][0] * (lam / l1)
        ms = jnp.mean(ot * ot, axis=0, keepdims=True)
        on = ot * lax.rsqrt(ms + NORM_EPS) * (gcol_ref[...] * (1.0 - LAM_INIT))
        o_ref[0, rows(i), :] = on.T.astype(BF16)

    def start():
        return ([functools.partial(prep, 0, 0)]
                + [functools.partial(score_tick, 0, kc) for kc in range(n_tick)],
                [functools.partial(stage_v, kc) for kc in range(n_tick)])

    def block(i, slot, last=False):
        state = {}

        def head():
            state["mx"] = final_max(slot)
            state["acc"] = [None, None]
            if not last:
                prep(i + 1, 1 - slot)

        def tick(kc):
            pv_tick(slot, kc, state["mx"], state["acc"])
            if not last:
                score_tick(1 - slot, kc)

        return ([head] + [functools.partial(tick, kc) for kc in range(n_tick)]
                + [lambda: finish(i, state["acc"])])

    return start, block


def _diff_scratch(c, s):
    return [pltpu.VMEM((c + s, LANES), BF16),
            pltpu.VMEM(((c + s) // DIFF_TK, LANES, DIFF_TK), BF16),
            pltpu.VMEM((2, 2, LANES, DIFF_TQ), BF16),
            pltpu.VMEM((2, 2, c + s, DIFF_TQ), F32),
            pltpu.VMEM((2, 2, SUBLANES, DIFF_TQ), F32)]


NA_QROWS = 4
NA_KROWS = 12
NA_TQ = NA_QROWS * GRID_W
NA_TK = 4 * GRID_W
NA_WIN_TICKS = NA_KROWS * GRID_W // NA_TK


def _na_variants(rows):
    n_rb = rows // NA_QROWS
    out = []
    for rb in (0, 1, n_rb - 1):
        ws = min(max(rb * NA_QROWS - NA_KH // 2, 0), rows - NA_KROWS)
        out.append((rb * NA_QROWS, ws))
    return out


def _na_build_bias(rpb_ref, bias_ref, rows):
    shape = (GRID_W, LANES)
    kc = lax.broadcasted_iota(jnp.int32, shape, 0)
    lane = lax.broadcasted_iota(jnp.int32, shape, 1)
    qc = lane & (GRID_W - 1)
    cs = jnp.clip(qc - NA_KW // 2, 0, GRID_W - NA_KW)
    col_ok = (kc >= cs) & (kc < cs + NA_KW)
    side_ok = {(True, True): col_ok, (True, False): col_ok & (lane < GRID_W),
               (False, True): col_ok & (lane >= GRID_W)}
    neg = jnp.full(shape, NEG_INF, F32)
    for hh in range(2):
        toeplitz = {}
        for v, (r0, ws) in enumerate(_na_variants(rows)):
            for i in range(NA_KROWS):
                kr = ws + i
                for jp in range(NA_QROWS // 2):
                    ok = []
                    for r in (r0 + 2 * jp, r0 + 2 * jp + 1):
                        rs = min(max(r - NA_KH // 2, 0), rows - NA_KH)
                        ok.append(rs <= kr < rs + NA_KH)
                    tile = neg
                    if any(ok):
                        e = kr - (r0 + 2 * jp) + NA_KH - 1
                        if e not in toeplitz:
                            vec = jnp.broadcast_to(rpb_ref[0, hh, e:e + 1, :] * LOG2E, shape)
                            toeplitz[e] = pltpu.roll(vec, LANES - (NA_KW - 1), 1, stride=1, stride_axis=0)
                        tile = jnp.where(side_ok[tuple(ok)], toeplitz[e], neg)
                    bias_ref[hh, v, i * GRID_W:(i + 1) * GRID_W, jp * LANES:(jp + 1) * LANES] = tile


def _na_pipeline(q_ref, kc_ref, k_ref, vc_ref, v_ref, rpb_ref, o_ref,
                 bias_ref, vte_ref, qm_ref, s_ref, m_ref, *, rows):
    n_blk = rows // NA_QROWS
    n_lat = v_ref.shape[1] // NA_TK
    assert kc_ref.shape[1] == NA_TK
    n_tick = NA_WIN_TICKS + 1

    @pl.when(pl.program_id(1) == 0)
    def _():
        _na_build_bias(rpb_ref, bias_ref, rows)

    def stage_v(ch):
        v = v_ref[0, ch * NA_TK:(ch + 1) * NA_TK] if ch < n_lat else vc_ref[0]
        vt = v.astype(F32).T
        for hh in range(2):
            vte_ref[hh, ch] = vt[hh * HEAD_DIM:(hh + 1) * HEAD_DIM].astype(BF16)

    def rows_of(i):
        return pl.ds(pl.multiple_of(i * NA_TQ, NA_TQ), NA_TQ)

    def window(i):
        return (jnp.clip(i - 1, 0, n_lat - NA_WIN_TICKS),
                jnp.where(i == 0, 0, jnp.where(i == n_blk - 1, 2, 1)))

    def prep(i, slot):
        lo, hi = _split_halves(q_ref[0, rows_of(i), :])
        qm_ref[slot, 0] = lo
        qm_ref[slot, 1] = hi
        m_ref[slot] = jnp.full(m_ref.shape[1:], -jnp.inf, F32)

    def score_tick(slot, i, t):
        ch0, variant = window(i)
        if t < NA_WIN_TICKS:
            k = k_ref[0, pl.ds(pl.multiple_of((ch0 + t) * NA_TK, NA_TK), NA_TK), :]
        else:
            k = kc_ref[0]
        for hh in range(2):
            s = jnp.dot(k, qm_ref[slot, hh], preferred_element_type=F32)
            if t < NA_WIN_TICKS:
                s = s + bias_ref[hh, variant, t * NA_TK:(t + 1) * NA_TK, :]
            s_ref[slot, hh, t * NA_TK:(t + 1) * NA_TK, :] = s
            part = jnp.max(s.reshape(NA_TK // SUBLANES, SUBLANES, NA_TQ), axis=0)
            m_ref[slot, hh] = jnp.maximum(m_ref[slot, hh], part)

    def final_max(slot):
        return [jnp.broadcast_to(jnp.max(m_ref[slot, hh], axis=0, keepdims=True), (SUBLANES, NA_TQ))
                for hh in range(2)]

    def pv_tick(slot, i, t, mx, acc):
        ch = window(i)[0] + t if t < NA_WIN_TICKS else n_lat
        for hh in range(2):
            s = s_ref[slot, hh, t * NA_TK:(t + 1) * NA_TK, :].reshape(NA_TK // SUBLANES, SUBLANES, NA_TQ)
            p = jnp.exp2(s - mx[hh])
            d = [jnp.dot(vte_ref[hh, ch], p.reshape(NA_TK, NA_TQ).astype(BF16), preferred_element_type=F32),
                 jnp.sum(p, axis=0)]
            acc[hh] = d if acc[hh] is None else [a + b for a, b in zip(acc[hh], d)]

    def finish(i, acc):
        ot = jnp.concatenate(
            [acc[hh][0] * (1.0 / jnp.sum(acc[hh][1], axis=0, keepdims=True)) for hh in range(2)],
            axis=0)
        o_ref[0, rows_of(i), :] = ot.T.astype(BF16)

    def start():
        return ([functools.partial(prep, 0, 0)]
                + [functools.partial(score_tick, 0, 0, t) for t in range(n_tick)],
                [functools.partial(stage_v, ch) for ch in range(n_lat + 1)])

    def block(i, slot, last=False):
        state = {}

        def head():
            state["mx"] = final_max(slot)
            state["acc"] = [None, None]
            if not last:
                prep(i + 1, 1 - slot)

        def tick(t):
            pv_tick(slot, i, t, state["mx"], state["acc"])
            if not last:
                score_tick(1 - slot, i + 1, t)

        return ([head] + [functools.partial(tick, t) for t in range(n_tick)]
                + [lambda: finish(i, state["acc"])])

    return start, block


def _na_scratch(n_lat):
    return [pltpu.VMEM((2, 3, NA_KROWS * GRID_W, NA_TQ), F32),
            pltpu.VMEM((2, n_lat + 1, HEAD_DIM, NA_TK), BF16),
            pltpu.VMEM((2, 2, LANES, NA_TQ), BF16),
            pltpu.VMEM((2, 2, (NA_WIN_TICKS + 1) * NA_TK, NA_TQ), F32),
            pltpu.VMEM((2, 2, SUBLANES, NA_TQ), F32)]


N_DIFF_IN, N_NA_IN = 7, 6


def _attn_kernel(*refs, rows):
    d_in = refs[:N_DIFF_IN]
    n_in = refs[N_DIFF_IN:N_DIFF_IN + N_NA_IN]
    oa_ref, ob_ref = refs[N_DIFF_IN + N_NA_IN:N_DIFF_IN + N_NA_IN + 2]
    scratch = refs[N_DIFF_IN + N_NA_IN + 2:]
    d_start, d_block = _diff_pipeline(*d_in, oa_ref, *scratch[:5])
    n_start, n_block = _na_pipeline(*n_in, ob_ref, *scratch[5:], rows=rows)
    n_blk = rows // NA_QROWS
    _interleave(*d_start(), *n_start())

    def pair(u, carry):
        for j in range(2):
            _interleave(d_block(2 * u + j, j), n_block(2 * u + j, j))
        return carry

    lax.fori_loop(0, n_blk // 2 - 1, pair, 0)
    _interleave(d_block(n_blk - 2, 0), n_block(n_blk - 2, 0))
    _interleave(d_block(n_blk - 1, 1, last=True), n_block(n_blk - 1, 1, last=True))


def _attention(aq, ak, av, akc, avc, lamv, gout, nq, nk, nv, nkc, nvc, rpb):
    bsz, s, _ = aq.shape
    c = akc.shape[1]
    rows = s // GRID_W
    n_blk = rows // NA_QROWS
    assert NA_HEADS // 2 == DIFF_HEADS and NA_TQ == DIFF_TQ and NA_QROWS * GRID_W == NA_TK and c == NA_TK
    assert n_blk >= 4 and n_blk % 2 == 0 and c % DIFF_TK == 0 and s % DIFF_TK == 0
    n_r, n_c = rpb.shape[1:]
    padded = jnp.pad(rpb[:, :, ::-1], ((0, 0), (1, 1), (0, GRID_W - n_c)))
    rpb2 = jnp.concatenate([padded[:, 1:], padded[:, :-1]], axis=-1).reshape(DIFF_HEADS, 2, n_r + 1, LANES)
    lat = pl.BlockSpec((1, s, LANES), lambda h, b: (b, 0, h))
    ctx = pl.BlockSpec((1, c, LANES), lambda h, b: (b, 0, h))
    return pl.pallas_call(
        functools.partial(_attn_kernel, rows=rows),
        grid=(DIFF_HEADS, bsz),
        in_specs=[lat, ctx, lat, ctx, lat,
                  pl.BlockSpec(lamv.shape, lambda h, b: (0, 0)),
                  pl.BlockSpec((LANES, 1), lambda h, b: (0, 0)),
                  lat, ctx, lat, ctx, lat,
                  pl.BlockSpec((1,) + rpb2.shape[1:], lambda h, b: (h, 0, 0, 0))],
        out_specs=[lat, lat],
        out_shape=[jax.ShapeDtypeStruct(aq.shape, BF16), jax.ShapeDtypeStruct(nq.shape, BF16)],
        scratch_shapes=_diff_scratch(c, s) + _na_scratch(s // NA_TK),
        compiler_params=_cparams(("arbitrary", "arbitrary")),
        name="attention",
    )(aq, akc, ak, avc, av, lamv, gout.reshape(LANES, 1), nq, nkc, nk, nvc, nv, rpb2)


def _rope_tables(n_tokens):
    t = jnp.arange(n_tokens, dtype=jnp.int32)
    row = (t // GRID_W).astype(F32)
    col = (t % GRID_W).astype(F32)
    n_freq = HEAD_DIM // 4
    inv_freq = ROPE_BASE ** (-jnp.arange(n_freq, dtype=F32) / n_freq)
    ar = row[:, None] * inv_freq
    ac = col[:, None] * inv_freq
    cos = jnp.concatenate([jnp.cos(ar), jnp.cos(ar), jnp.cos(ac), jnp.cos(ac)], axis=-1)
    sin = jnp.concatenate([-jnp.sin(ar), jnp.sin(ar), -jnp.sin(ac), jnp.sin(ac)], axis=-1)
    reps = LANES // HEAD_DIM
    return jnp.tile(cos, (1, reps)), jnp.tile(sin, (1, reps))


def kernel(x, c, ctx, c_ctx, w_ada, b_ada, norm1, norm2, norm3, ffn1_w_gu, ffn1_w_down, w_in,
           diff_q_norm, diff_k_norm, lam_q1, lam_k1, lam_q2, lam_k2, diff_out_norm,
           na_q_norm, na_k_norm, na_rpb, w_out, ffn2_w_gu, ffn2_w_down):
    bsz, s, d = x.shape
    assert w_ada.shape[0] == 1 and d == D_MODEL and s % (NA_QROWS * GRID_W) == 0
    rows = s // GRID_W

    cond = jnp.zeros((16, d), F32).at[:bsz].set(c).at[bsz].set(c_ctx)
    mod4 = _adaln(cond, w_ada[0], b_ada).reshape(16, N_MOD, 1, d)

    ffn1_w = _ffn_weights(ffn1_w_gu[0], ffn1_w_down[0])
    h, w_in_b, w_out_b, w_gu2, w_dn2 = _ffn(x, mod4, 0, False, norm1, ffn1_w, tm=1024,
                                            casts=(w_in[0], w_out[0], ffn2_w_gu[0], ffn2_w_down[0]))
    n_ctx = ctx.shape[1]
    hc = _ffn(ctx.reshape(1, bsz * n_ctx, d), mod4, 0, True, norm1, ffn1_w, tm=1024)

    seg = np.arange(MXU_W) // HEAD_DIM
    bd = jnp.asarray(seg[:, None] == seg[None, :], BF16)
    cos, sin = _rope_tables(s)
    reps = GROUP_W // HEAD_DIM
    gains = [jnp.tile(g, (1, reps)) for g in (diff_q_norm, diff_k_norm, na_q_norm, na_k_norm)]
    qs = HEAD_DIM ** -0.5
    lat_groups = ((0, 0, True, qs * LOG2E), (1, 1, True, 1.0), (2, None, False, 1.0),
                  (3, 2, False, qs * LOG2E), (4, 3, False, 1.0), (5, None, False, 1.0))
    aq, ak, av, nq, nk, nv = _in_proj(h, mod4, False, norm2, w_in_b, bd, cos, sin, gains,
                                      lat_groups, tm=1024)
    ctx_groups = ((1, 1, False, 1.0), (2, None, False, 1.0), (4, 3, False, 1.0), (5, None, False, 1.0))
    ctx_kv = _in_proj(hc, mod4, True, norm2, w_in_b, bd, cos, sin, gains, ctx_groups, tm=512)
    akc, avc, nkc, nvc = (a.reshape(bsz, n_ctx, GROUP_W) for a in ctx_kv)

    lamv = jnp.concatenate([lam_q1, lam_k1, lam_q2, lam_k2], axis=0)
    oa, ob = _attention(aq, ak, av, akc, avc, lamv, diff_out_norm, nq, nk, nv, nkc, nvc, na_rpb[0])

    return _ffn(h, mod4, 6, False, norm3, (w_gu2, w_dn2), tm=1024, attn=(5, oa, ob, w_out_b))
```

```python
import functools
import math

import numpy as np
import jax
import jax.numpy as jnp
from jax import lax
from jax.experimental import pallas as pl
from jax.experimental.pallas import tpu as pltpu

D_MODEL = 1024
GRID_W = 64
HEAD_DIM = 64
DIFF_HEADS = 4
NA_HEADS = 8
GROUP_W = 512
D_FF = 2816
N_MOD = 9
NA_KH = 8
NA_KW = 16
ROPE_BASE = 10000.0
NORM_EPS = 1e-6
NEG_INF = -1e30
LAM_INIT = 0.8 - 0.6 * math.exp(-0.3 * 0)
LOG2E = math.log2(math.e)

LANES = 128
MXU_W = 256
VMEM_LIMIT = 56 * 1024 * 1024

F32 = jnp.float32
BF16 = jnp.bfloat16


def _cparams(sem):
    return pltpu.CompilerParams(dimension_semantics=sem, vmem_limit_bytes=VMEM_LIMIT)


ADALN_TK = 128


def _adaln_kernel(cond_ref, w_ref, b_ref, o_ref):
    k = pl.program_id(0)
    c = cond_ref[...]
    a = (c * jax.nn.sigmoid(c)).astype(BF16)
    part = jnp.dot(a, w_ref[...].astype(BF16), preferred_element_type=F32)

    @pl.when(k == 0)
    def _():
        o_ref[...] = part + b_ref[...]

    @pl.when(k > 0)
    def _():
        o_ref[...] += part


def _adaln(cond, w_ada, b_ada):
    rows, d = cond.shape
    n = w_ada.shape[1]
    return pl.pallas_call(
        _adaln_kernel,
        grid=(d // ADALN_TK,),
        in_specs=[pl.BlockSpec((rows, ADALN_TK), lambda k: (0, k)),
                  pl.BlockSpec((ADALN_TK, n), lambda k: (k, 0)),
                  pl.BlockSpec((1, n), lambda k: (0, 0))],
        out_specs=pl.BlockSpec((rows, n), lambda k: (0, 0)),
        out_shape=jax.ShapeDtypeStruct((rows, n), F32),
        compiler_params=_cparams(("arbitrary",)),
        name="adaln",
    )(cond, w_ada, b_ada)


def _modulated_norm(x, g, shift, scale):
    ms = jnp.mean(x * x, axis=-1, keepdims=True)
    xn = x * lax.rsqrt(ms + NORM_EPS) * g
    return xn * (1.0 + scale) + shift


def _mod_spec(k, ctx):
    if ctx:
        return pl.BlockSpec((1, 1, 1, D_MODEL), lambda b, *_: (8, k, 0, 0))
    return pl.BlockSpec((1, 1, 1, D_MODEL), lambda b, *_: (b, k, 0, 0))


FFN_TF = 256


def _ffn_kernel(*refs, with_attn, n_cast):
    refs = list(refs)
    h_ref = refs.pop(0)
    if with_attn:
        x_ref = h_ref
        agate_ref, oa_ref, ob_ref, wo_ref = (refs.pop(0) for _ in range(4))
    shift_ref, scale_ref, gate_ref, g_ref, wgu_ref, wd_ref = (refs.pop(0) for _ in range(6))
    cast_in = [refs.pop(0) for _ in range(n_cast)]
    o_ref = refs.pop(0)
    cast_out = [refs.pop(0) for _ in range(n_cast)]
    xm_ref, acc_ref = refs[:2]
    for src, dst in zip(cast_in, cast_out):
        dst[...] = src[...].astype(BF16)
    if with_attn:
        h_ref = refs[2]
        half = oa_ref.shape[-1]
        attn = (jnp.dot(oa_ref[0], wo_ref[0:half, :], preferred_element_type=F32)
                + jnp.dot(ob_ref[0], wo_ref[half:, :], preferred_element_type=F32))
        h_ref[0] = x_ref[0] + agate_ref[0, 0] * attn
    f = wd_ref.shape[0]
    nj = f // FFN_TF
    xm_ref[...] = _modulated_norm(h_ref[0], g_ref[...], shift_ref[0, 0], scale_ref[0, 0]).astype(BF16)

    def act(j):
        xm = xm_ref[...]
        g = jnp.dot(xm, wgu_ref[:, j * FFN_TF:(j + 1) * FFN_TF], preferred_element_type=F32)
        u = jnp.dot(xm, wgu_ref[:, f + j * FFN_TF:f + (j + 1) * FFN_TF], preferred_element_type=F32)
        return (g * jax.nn.sigmoid(g) * u).astype(BF16)

    a = act(0)
    for j in range(nj):
        down = jnp.dot(a, wd_ref[j * FFN_TF:(j + 1) * FFN_TF, :], preferred_element_type=F32)
        if j + 1 < nj:
            a = act(j + 1)
        if j == 0:
            acc_ref[...] = down
        elif j + 1 < nj:
            acc_ref[...] += down
        else:
            o_ref[0] = h_ref[0] + (0.5 * gate_ref[0, 0]) * (acc_ref[...] + down)


def _ffn_weights(w_gu, w_down):
    return w_gu.astype(BF16), w_down.astype(BF16)


def _resident(shape):
    return pl.BlockSpec(shape, lambda *_: (0,) * len(shape), pipeline_mode=pl.Buffered(1))


def _ffn(h, mod4, k0, ctx, g, weights, *, tm, attn=None, casts=()):
    bsz, t, d = h.shape
    wgu, wd = weights
    assert wd.shape[0] % FFN_TF == 0
    n_i = t // tm
    tile = pl.BlockSpec((1, tm, d), lambda b, i: (b, i, 0))
    in_specs, args = [tile], [h]
    scratch = [pltpu.VMEM((tm, d), BF16), pltpu.VMEM((tm, d), F32)]
    if attn is not None:
        k_gate, oa, ob, w_out = attn
        half = oa.shape[-1]
        in_specs += [_mod_spec(k_gate, ctx), pl.BlockSpec((1, tm, half), lambda b, i: (b, i, 0)),
                     pl.BlockSpec((1, tm, half), lambda b, i: (b, i, 0)), _resident(w_out.shape)]
        args += [mod4, oa, ob, w_out]
        scratch.append(pltpu.VMEM((1, tm, d), F32))
    in_specs += [_mod_spec(k0, ctx), _mod_spec(k0 + 1, ctx), _mod_spec(k0 + 2, ctx),
                 pl.BlockSpec((1, d), lambda b, i: (0, 0)), _resident(wgu.shape), _resident(wd.shape)]
    args += [mod4, mod4, mod4, g, wgu, wd]
    slabs = []
    for w in casts:
        rows_per_step = w.shape[0] // (bsz * n_i)
        assert rows_per_step * bsz * n_i == w.shape[0] and rows_per_step % 16 == 0
        slabs.append(pl.BlockSpec((rows_per_step, w.shape[1]), lambda b, i: (b * n_i + i, 0)))
    out = pl.pallas_call(
        functools.partial(_ffn_kernel, with_attn=attn is not None, n_cast=len(casts)),
        grid=(bsz, n_i),
        in_specs=in_specs + slabs,
        out_specs=[tile] + slabs,
        out_shape=[jax.ShapeDtypeStruct(h.shape, F32)] + [jax.ShapeDtypeStruct(w.shape, BF16) for w in casts],
        scratch_shapes=scratch,
        compiler_params=_cparams(("parallel", "parallel")),
        name="ffn_ctx" if ctx else "ffn",
    )(*args, *casts)
    return out if casts else out[0]


def _head_rms_scale(y, bd_ref):
    y2 = (y * y).astype(BF16)
    bd = bd_ref[...]
    w = bd.shape[0]
    ss = jnp.concatenate(
        [jnp.dot(y2[:, c:c + w], bd, preferred_element_type=F32) for c in range(0, y.shape[1], w)], axis=-1)
    return lax.rsqrt(ss * (1.0 / HEAD_DIM) + NORM_EPS)


def _rope(z, cos, sin):
    lane = lax.broadcasted_iota(jnp.int32, z.shape, 1)
    up = pltpu.roll(z, LANES - 16, 1)
    dn = pltpu.roll(z, 16, 1)
    return z * cos + jnp.where((lane & 16) == 0, up, dn) * sin


def _in_proj_kernel(*refs, groups, n_gain):
    h_ref, shift_ref, scale_ref, g_ref, w_ref, bd_ref, cos_ref, sin_ref = refs[:8]
    gain_refs = refs[8:8 + n_gain]
    out_refs = refs[8 + n_gain:]
    xm = _modulated_norm(h_ref[0], g_ref[...], shift_ref[0, 0], scale_ref[0, 0]).astype(BF16)
    for (col, gain_idx, rope, qscale), o_ref in zip(groups, out_refs):
        y = jnp.dot(xm, w_ref[:, col * GROUP_W:(col + 1) * GROUP_W], preferred_element_type=F32)
        if gain_idx is not None:
            gain = gain_refs[gain_idx][...]
            if qscale != 1.0:
                gain = gain * qscale
            y = y * _head_rms_scale(y, bd_ref) * gain
        if rope:
            cos = cos_ref[...]
            sin = sin_ref[...]
            for c in range(GROUP_W // LANES):
                sl = slice(c * LANES, (c + 1) * LANES)
                o_ref[0, :, sl] = _rope(y[:, sl], cos, sin).astype(BF16)
        else:
            o_ref[0] = y.astype(BF16)


def _in_proj(h, mod4, ctx, g, w_in, bd, cos, sin, gains, groups, *, tm):
    bsz, t, d = h.shape
    n_out = len(groups)
    in_specs = [pl.BlockSpec((1, tm, d), lambda b, i: (b, i, 0)),
                _mod_spec(3, ctx), _mod_spec(4, ctx),
                pl.BlockSpec((1, d), lambda b, i: (0, 0)),
                pl.BlockSpec(w_in.shape, lambda b, i: (0, 0)),
                pl.BlockSpec(bd.shape, lambda b, i: (0, 0)),
                pl.BlockSpec((tm, LANES), lambda b, i: (i, 0)),
                pl.BlockSpec((tm, LANES), lambda b, i: (i, 0))]
    in_specs += [pl.BlockSpec((1, GROUP_W), lambda b, i: (0, 0)) for _ in gains]
    return pl.pallas_call(
        functools.partial(_in_proj_kernel, groups=groups, n_gain=len(gains)),
        grid=(bsz, t // tm),
        in_specs=in_specs,
        out_specs=[pl.BlockSpec((1, tm, GROUP_W), lambda b, i: (b, i, 0)) for _ in range(n_out)],
        out_shape=[jax.ShapeDtypeStruct((bsz, t, GROUP_W), BF16) for _ in range(n_out)],
        compiler_params=_cparams(("parallel", "parallel")),
        name="in_proj_ctx" if ctx else "in_proj",
    )(h, mod4, mod4, g, w_in, bd, cos, sin, *gains)


def _split_halves(q):
    qt = q.astype(F32).T
    row = lax.broadcasted_iota(jnp.int32, qt.shape, 0)
    lo = jnp.where(row < HEAD_DIM, qt, 0.0).astype(BF16)
    hi = jnp.where(row >= HEAD_DIM, qt, 0.0).astype(BF16)
    return lo, hi


DIFF_TQ = 256
DIFF_TK = 256
SUBLANES = 8


ATTN_SAMPLES = 2


def _locate(i, n_blk):
    if isinstance(i, int):
        return divmod(i, n_blk)
    return lax.shift_right_logical(i, n_blk.bit_length() - 1), i & (n_blk - 1)


def _interleave(*stages):
    order = sorted(((k + 0.5) / len(st), n, k) for n, st in enumerate(stages) for k in range(len(st)))
    for _, n, k in order:
        stages[n][k]()


def _diff_pipeline(q_ref, kc_ref, kl_ref, vc_ref, vl_ref, lamv_ref, gcol_ref, o_ref,
                   kk_ref, vte_ref, qm_ref, s_ref, m_ref):
    n_smp = q_ref.shape[0]
    n_ctx = kc_ref.shape[1]
    n_tick = kk_ref.shape[1] // DIFF_TK
    n_blk = q_ref.shape[1] // DIFF_TQ
    for smp in range(n_smp):
        kk_ref[smp, 0:n_ctx] = kc_ref[smp]
        kk_ref[smp, n_ctx:] = kl_ref[smp]

    def stage_v(smp, kc):
        lo = kc * DIFF_TK
        v = vc_ref[smp, lo:lo + DIFF_TK] if lo < n_ctx else vl_ref[smp, lo - n_ctx:lo - n_ctx + DIFF_TK]
        vte_ref[smp, kc] = v.astype(F32).T.astype(BF16)

    lv = lamv_ref[...]
    lam = (jnp.exp(jnp.sum(lv[0:1] * lv[1:2], axis=-1, keepdims=True))
           - jnp.exp(jnp.sum(lv[2:3] * lv[3:4], axis=-1, keepdims=True)) + LAM_INIT)

    def rows(i):
        smp, li = _locate(i, n_blk)
        return smp, pl.ds(pl.multiple_of(li * DIFF_TQ, DIFF_TQ), DIFF_TQ)

    def keys(kc):
        return slice(kc * DIFF_TK, (kc + 1) * DIFF_TK)

    def prep(i, slot):
        smp, r = rows(i)
        lo, hi = _split_halves(q_ref[smp, r, :])
        qm_ref[slot, 0] = lo
        qm_ref[slot, 1] = hi
        m_ref[slot] = jnp.full(m_ref.shape[1:], -jnp.inf, F32)

    def score_tick(slot, i, kc):
        k = kk_ref[_locate(i, n_blk)[0], keys(kc), :]
        for c in range(2):
            s = jnp.dot(k, qm_ref[slot, c], preferred_element_type=F32)
            s_ref[slot, c, keys(kc), :] = s
            part = jnp.max(s.reshape(DIFF_TK // SUBLANES, SUBLANES, DIFF_TQ), axis=0)
            m_ref[slot, c] = jnp.maximum(m_ref[slot, c], part)

    def final_max(slot):
        return [jnp.broadcast_to(jnp.max(m_ref[slot, c], axis=0, keepdims=True), (SUBLANES, DIFF_TQ))
                for c in range(2)]

    def pv_tick(slot, i, kc, mx, acc):
        vt = vte_ref[_locate(i, n_blk)[0], kc]
        for c in range(2):
            s = s_ref[slot, c, keys(kc), :].reshape(DIFF_TK // SUBLANES, SUBLANES, DIFF_TQ)
            p = jnp.exp2(s - mx[c])
            d = [jnp.dot(vt, p.reshape(DIFF_TK, DIFF_TQ).astype(BF16), preferred_element_type=F32),
                 jnp.sum(p, axis=0)]
            acc[c] = d if acc[c] is None else [a + b for a, b in zip(acc[c], d)]

    def finish(i, acc):
        l0 = jnp.sum(acc[0][1], axis=0, keepdims=True)
        l1 = jnp.sum(acc[1][1], axis=0, keepdims=True)
        ot = acc[0][0] * (1.0 / l0) - acc[1][0] * (lam / l1)
        ms = jnp.mean(ot * ot, axis=0, keepdims=True)
        on = ot * lax.rsqrt(ms + NORM_EPS) * (gcol_ref[...] * (1.0 - LAM_INIT))
        smp, r = rows(i)
        o_ref[smp, r, :] = on.T.astype(BF16)

    def start():
        return ([functools.partial(prep, 0, 0)]
                + [functools.partial(score_tick, 0, 0, kc) for kc in range(n_tick)],
                [functools.partial(stage_v, smp, kc) for smp in range(n_smp) for kc in range(n_tick)])

    def block(i, slot, last=False):
        state = {}

        def head():
            state["mx"] = final_max(slot)
            state["acc"] = [None, None]
            if not last:
                prep(i + 1, 1 - slot)

        def tick(kc):
            pv_tick(slot, i, kc, state["mx"], state["acc"])
            if not last:
                score_tick(1 - slot, i + 1, kc)

        return ([head] + [functools.partial(tick, kc) for kc in range(n_tick)]
                + [lambda: finish(i, state["acc"])])

    return start, block


def _diff_scratch(c, s):
    return [pltpu.VMEM((ATTN_SAMPLES, c + s, LANES), BF16),
            pltpu.VMEM((ATTN_SAMPLES, (c + s) // DIFF_TK, LANES, DIFF_TK), BF16),
            pltpu.VMEM((2, 2, LANES, DIFF_TQ), BF16),
            pltpu.VMEM((2, 2, c + s, DIFF_TQ), F32),
            pltpu.VMEM((2, 2, SUBLANES, DIFF_TQ), F32)]


NA_QROWS = 4
NA_KROWS = 12
NA_TQ = NA_QROWS * GRID_W
NA_TK = 4 * GRID_W
NA_WIN_TICKS = NA_KROWS * GRID_W // NA_TK


def _na_variants(rows):
    n_rb = rows // NA_QROWS
    out = []
    for rb in (0, 1, n_rb - 1):
        ws = min(max(rb * NA_QROWS - NA_KH // 2, 0), rows - NA_KROWS)
        out.append((rb * NA_QROWS, ws))
    return out


def _na_build_bias(rpb_ref, bias_ref, rows):
    shape = (GRID_W, LANES)
    kc = lax.broadcasted_iota(jnp.int32, shape, 0)
    lane = lax.broadcasted_iota(jnp.int32, shape, 1)
    qc = lane & (GRID_W - 1)
    cs = jnp.clip(qc - NA_KW // 2, 0, GRID_W - NA_KW)
    col_ok = (kc >= cs) & (kc < cs + NA_KW)
    side_ok = {(True, True): col_ok, (True, False): col_ok & (lane < GRID_W),
               (False, True): col_ok & (lane >= GRID_W)}
    neg = jnp.full(shape, NEG_INF, F32)
    for hh in range(2):
        toeplitz = {}
        for v, (r0, ws) in enumerate(_na_variants(rows)):
            for i in range(NA_KROWS):
                kr = ws + i
                for jp in range(NA_QROWS // 2):
                    ok = []
                    for r in (r0 + 2 * jp, r0 + 2 * jp + 1):
                        rs = min(max(r - NA_KH // 2, 0), rows - NA_KH)
                        ok.append(rs <= kr < rs + NA_KH)
                    tile = neg
                    if any(ok):
                        e = kr - (r0 + 2 * jp) + NA_KH - 1
                        if e not in toeplitz:
                            vec = jnp.broadcast_to(rpb_ref[0, hh, e:e + 1, :] * LOG2E, shape)
                            toeplitz[e] = pltpu.roll(vec, LANES - (NA_KW - 1), 1, stride=1, stride_axis=0)
                        tile = jnp.where(side_ok[tuple(ok)], toeplitz[e], neg)
                    bias_ref[hh, v, i * GRID_W:(i + 1) * GRID_W, jp * LANES:(jp + 1) * LANES] = tile


def _na_pipeline(q_ref, kc_ref, k_ref, vc_ref, v_ref, rpb_ref, o_ref,
                 bias_ref, vte_ref, qm_ref, s_ref, m_ref, *, rows):
    n_blk = rows // NA_QROWS
    n_lat = v_ref.shape[1] // NA_TK
    assert kc_ref.shape[1] == NA_TK
    n_tick = NA_WIN_TICKS + 1

    @pl.when(pl.program_id(1) == 0)
    def _():
        _na_build_bias(rpb_ref, bias_ref, rows)

    n_smp = q_ref.shape[0]

    def stage_v(smp, ch):
        v = v_ref[smp, ch * NA_TK:(ch + 1) * NA_TK] if ch < n_lat else vc_ref[smp]
        vt = v.astype(F32).T
        for hh in range(2):
            vte_ref[smp, hh, ch] = vt[hh * HEAD_DIM:(hh + 1) * HEAD_DIM].astype(BF16)

    def rows_of(i):
        smp, li = _locate(i, n_blk)
        return smp, pl.ds(pl.multiple_of(li * NA_TQ, NA_TQ), NA_TQ)

    def window(i):
        smp, li = _locate(i, n_blk)
        return (smp, jnp.clip(li - 1, 0, n_lat - NA_WIN_TICKS),
                jnp.where(li == 0, 0, jnp.where(li == n_blk - 1, 2, 1)))

    def prep(i, slot):
        smp, r = rows_of(i)
        lo, hi = _split_halves(q_ref[smp, r, :])
        qm_ref[slot, 0] = lo
        qm_ref[slot, 1] = hi
        m_ref[slot] = jnp.full(m_ref.shape[1:], -jnp.inf, F32)

    def score_tick(slot, i, t):
        smp, ch0, variant = window(i)
        if t < NA_WIN_TICKS:
            k = k_ref[smp, pl.ds(pl.multiple_of((ch0 + t) * NA_TK, NA_TK), NA_TK), :]
        else:
            k = kc_ref[smp]
        for hh in range(2):
            s = jnp.dot(k, qm_ref[slot, hh], preferred_element_type=F32)
            if t < NA_WIN_TICKS:
                s = s + bias_ref[hh, variant, t * NA_TK:(t + 1) * NA_TK, :]
            s_ref[slot, hh, t * NA_TK:(t + 1) * NA_TK, :] = s
            part = jnp.max(s.reshape(NA_TK // SUBLANES, SUBLANES, NA_TQ), axis=0)
            m_ref[slot, hh] = jnp.maximum(m_ref[slot, hh], part)

    def final_max(slot):
        return [jnp.broadcast_to(jnp.max(m_ref[slot, hh], axis=0, keepdims=True), (SUBLANES, NA_TQ))
                for hh in range(2)]

    def pv_tick(slot, i, t, mx, acc):
        smp, ch0, _ = window(i)
        ch = ch0 + t if t < NA_WIN_TICKS else n_lat
        for hh in range(2):
            s = s_ref[slot, hh, t * NA_TK:(t + 1) * NA_TK, :].reshape(NA_TK // SUBLANES, SUBLANES, NA_TQ)
            p = jnp.exp2(s - mx[hh])
            d = [jnp.dot(vte_ref[smp, hh, ch], p.reshape(NA_TK, NA_TQ).astype(BF16), preferred_element_type=F32),
                 jnp.sum(p, axis=0)]
            acc[hh] = d if acc[hh] is None else [a + b for a, b in zip(acc[hh], d)]

    def finish(i, acc):
        ot = jnp.concatenate(
            [acc[hh][0] * (1.0 / jnp.sum(acc[hh][1], axis=0, keepdims=True)) for hh in range(2)],
            axis=0)
        smp, r = rows_of(i)
        o_ref[smp, r, :] = ot.T.astype(BF16)

    def start():
        return ([functools.partial(prep, 0, 0)]
                + [functools.partial(score_tick, 0, 0, t) for t in range(n_tick)],
                [functools.partial(stage_v, smp, ch) for smp in range(n_smp) for ch in range(n_lat + 1)])

    def block(i, slot, last=False):
        state = {}

        def head():
            state["mx"] = final_max(slot)
            state["acc"] = [None, None]
            if not last:
                prep(i + 1, 1 - slot)

        def tick(t):
            pv_tick(slot, i, t, state["mx"], state["acc"])
            if not last:
                score_tick(1 - slot, i + 1, t)

        return ([head] + [functools.partial(tick, t) for t in range(n_tick)]
                + [lambda: finish(i, state["acc"])])

    return start, block


def _na_scratch(n_lat):
    return [pltpu.VMEM((2, 3, NA_KROWS * GRID_W, NA_TQ), F32),
            pltpu.VMEM((ATTN_SAMPLES, 2, n_lat + 1, HEAD_DIM, NA_TK), BF16),
            pltpu.VMEM((2, 2, LANES, NA_TQ), BF16),
            pltpu.VMEM((2, 2, (NA_WIN_TICKS + 1) * NA_TK, NA_TQ), F32),
            pltpu.VMEM((2, 2, SUBLANES, NA_TQ), F32)]


N_DIFF_IN, N_NA_IN = 7, 6


def _attn_kernel(*refs, rows):
    d_in = refs[:N_DIFF_IN]
    n_in = refs[N_DIFF_IN:N_DIFF_IN + N_NA_IN]
    oa_ref, ob_ref = refs[N_DIFF_IN + N_NA_IN:N_DIFF_IN + N_NA_IN + 2]
    scratch = refs[N_DIFF_IN + N_NA_IN + 2:]
    d_start, d_block = _diff_pipeline(*d_in, oa_ref, *scratch[:5])
    n_start, n_block = _na_pipeline(*n_in, ob_ref, *scratch[5:], rows=rows)
    n_blk = oa_ref.shape[0] * (rows // NA_QROWS)
    _interleave(*d_start(), *n_start())

    def pair(u, carry):
        for j in range(2):
            _interleave(d_block(2 * u + j, j), n_block(2 * u + j, j))
        return carry

    lax.fori_loop(0, n_blk // 2 - 1, pair, 0)
    _interleave(d_block(n_blk - 2, 0), n_block(n_blk - 2, 0))
    _interleave(d_block(n_blk - 1, 1, last=True), n_block(n_blk - 1, 1, last=True))


def _attention(aq, ak, av, akc, avc, lamv, gout, nq, nk, nv, nkc, nvc, rpb):
    bsz, s, _ = aq.shape
    c = akc.shape[1]
    rows = s // GRID_W
    n_blk = rows // NA_QROWS
    assert NA_HEADS // 2 == DIFF_HEADS and NA_TQ == DIFF_TQ and NA_QROWS * GRID_W == NA_TK and c == NA_TK
    assert n_blk >= 4 and n_blk & (n_blk - 1) == 0 and c % DIFF_TK == 0 and s % DIFF_TK == 0
    assert bsz % ATTN_SAMPLES == 0
    n_r, n_c = rpb.shape[1:]
    padded = jnp.pad(rpb[:, :, ::-1], ((0, 0), (1, 1), (0, GRID_W - n_c)))
    rpb2 = jnp.concatenate([padded[:, 1:], padded[:, :-1]], axis=-1).reshape(DIFF_HEADS, 2, n_r + 1, LANES)
    lat = pl.BlockSpec((ATTN_SAMPLES, s, LANES), lambda h, b: (b, 0, h))
    ctx = pl.BlockSpec((ATTN_SAMPLES, c, LANES), lambda h, b: (b, 0, h))
    return pl.pallas_call(
        functools.partial(_attn_kernel, rows=rows),
        grid=(DIFF_HEADS, bsz // ATTN_SAMPLES),
        in_specs=[lat, ctx, lat, ctx, lat,
                  pl.BlockSpec(lamv.shape, lambda h, b: (0, 0)),
                  pl.BlockSpec((LANES, 1), lambda h, b: (0, 0)),
                  lat, ctx, lat, ctx, lat,
                  pl.BlockSpec((1,) + rpb2.shape[1:], lambda h, b: (h, 0, 0, 0))],
        out_specs=[lat, lat],
        out_shape=[jax.ShapeDtypeStruct(aq.shape, BF16), jax.ShapeDtypeStruct(nq.shape, BF16)],
        scratch_shapes=_diff_scratch(c, s) + _na_scratch(s // NA_TK),
        compiler_params=_cparams(("arbitrary", "arbitrary")),
        name="attention",
    )(aq, akc, ak, avc, av, lamv, gout.reshape(LANES, 1), nq, nkc, nk, nvc, nv, rpb2)


def _rope_tables(n_tokens):
    t = jnp.arange(n_tokens, dtype=jnp.int32)
    row = (t // GRID_W).astype(F32)
    col = (t % GRID_W).astype(F32)
    n_freq = HEAD_DIM // 4
    inv_freq = ROPE_BASE ** (-jnp.arange(n_freq, dtype=F32) / n_freq)
    ar = row[:, None] * inv_freq
    ac = col[:, None] * inv_freq
    cos = jnp.concatenate([jnp.cos(ar), jnp.cos(ar), jnp.cos(ac), jnp.cos(ac)], axis=-1)
    sin = jnp.concatenate([-jnp.sin(ar), jnp.sin(ar), -jnp.sin(ac), jnp.sin(ac)], axis=-1)
    reps = LANES // HEAD_DIM
    return jnp.tile(cos, (1, reps)), jnp.tile(sin, (1, reps))


def kernel(x, c, ctx, c_ctx, w_ada, b_ada, norm1, norm2, norm3, ffn1_w_gu, ffn1_w_down, w_in,
           diff_q_norm, diff_k_norm, lam_q1, lam_k1, lam_q2, lam_k2, diff_out_norm,
           na_q_norm, na_k_norm, na_rpb, w_out, ffn2_w_gu, ffn2_w_down):
    bsz, s, d = x.shape
    assert w_ada.shape[0] == 1 and d == D_MODEL and s % (NA_QROWS * GRID_W) == 0
    rows = s // GRID_W

    cond = jnp.zeros((16, d), F32).at[:bsz].set(c).at[bsz].set(c_ctx)
    mod4 = _adaln(cond, w_ada[0], b_ada).reshape(16, N_MOD, 1, d)

    ffn1_w = _ffn_weights(ffn1_w_gu[0], ffn1_w_down[0])
    h, w_in_b, w_out_b, w_gu2, w_dn2 = _ffn(x, mod4, 0, False, norm1, ffn1_w, tm=1024,
                                            casts=(w_in[0], w_out[0], ffn2_w_gu[0], ffn2_w_down[0]))
    n_ctx = ctx.shape[1]
    hc = _ffn(ctx.reshape(1, bsz * n_ctx, d), mod4, 0, True, norm1, ffn1_w, tm=1024)

    seg = np.arange(MXU_W) // HEAD_DIM
    bd = jnp.asarray(seg[:, None] == seg[None, :], BF16)
    cos, sin = _rope_tables(s)
    reps = GROUP_W // HEAD_DIM
    gains = [jnp.tile(g, (1, reps)) for g in (diff_q_norm, diff_k_norm, na_q_norm, na_k_norm)]
    qs = HEAD_DIM ** -0.5
    lat_groups = ((0, 0, True, qs * LOG2E), (1, 1, True, 1.0), (2, None, False, 1.0),
                  (3, 2, False, qs * LOG2E), (4, 3, False, 1.0), (5, None, False, 1.0))
    aq, ak, av, nq, nk, nv = _in_proj(h, mod4, False, norm2, w_in_b, bd, cos, sin, gains,
                                      lat_groups, tm=1024)
    ctx_groups = ((1, 1, False, 1.0), (2, None, False, 1.0), (4, 3, False, 1.0), (5, None, False, 1.0))
    ctx_kv = _in_proj(hc, mod4, True, norm2, w_in_b, bd, cos, sin, gains, ctx_groups, tm=512)
    akc, avc, nkc, nvc = (a.reshape(bsz, n_ctx, GROUP_W) for a in ctx_kv)

    lamv = jnp.concatenate([lam_q1, lam_k1, lam_q2, lam_k2], axis=0)
    oa, ob = _attention(aq, ak, av, akc, avc, lamv, diff_out_norm, nq, nk, nv, nkc, nvc, na_rpb[0])

    return _ffn(h, mod4, 6, False, norm3, (w_gu2, w_dn2), tm=1024, attn=(5, oa, ob, w_out_b))
```

```python
import functools
import math

import numpy as np
import jax
import jax.numpy as jnp
from jax import lax
from jax.experimental import pallas as pl
from jax.experimental.pallas import tpu as pltpu

D_MODEL = 1024
GRID_W = 64
HEAD_DIM = 64
DIFF_HEADS = 4
NA_HEADS = 8
GROUP_W = 512
N_MOD = 9
NA_KH = 8
NA_KW = 16
ROPE_BASE = 10000.0
NORM_EPS = 1e-6
NEG_INF = -1e30
LAM_INIT = 0.8 - 0.6 * math.exp(-0.3 * 0)
LOG2E = math.log2(math.e)

LANES = 128
BF16_ROWS = 16
MOD_ROWS = 16
CTX_ROW = 8
TOKEN_TILE = 1024
CTX_PROJ_TILE = 512
MXU_W = 256
VMEM_LIMIT = 56 * 1024 * 1024

F32 = jnp.float32
BF16 = jnp.bfloat16


def _cparams(sem):
    return pltpu.CompilerParams(dimension_semantics=sem, vmem_limit_bytes=VMEM_LIMIT)


ADALN_TK = 128


def _adaln_kernel(cond_ref, w_ref, b_ref, o_ref):
    k = pl.program_id(0)
    c = cond_ref[...]
    a = (c * jax.nn.sigmoid(c)).astype(BF16)
    part = jnp.dot(a, w_ref[...].astype(BF16), preferred_element_type=F32)

    @pl.when(k == 0)
    def _():
        o_ref[...] = part + b_ref[...]

    @pl.when(k > 0)
    def _():
        o_ref[...] += part


def _adaln(cond, w_ada, b_ada):
    rows, d = cond.shape
    n = w_ada.shape[1]
    return pl.pallas_call(
        _adaln_kernel,
        grid=(d // ADALN_TK,),
        in_specs=[pl.BlockSpec((rows, ADALN_TK), lambda k: (0, k)),
                  pl.BlockSpec((ADALN_TK, n), lambda k: (k, 0)),
                  pl.BlockSpec((1, n), lambda k: (0, 0))],
        out_specs=pl.BlockSpec((rows, n), lambda k: (0, 0)),
        out_shape=jax.ShapeDtypeStruct((rows, n), F32),
        compiler_params=_cparams(("arbitrary",)),
        name="adaln",
    )(cond, w_ada, b_ada)


def _modulated_norm(x, g, shift, scale):
    ms = jnp.mean(x * x, axis=-1, keepdims=True)
    xn = x * lax.rsqrt(ms + NORM_EPS) * g
    return xn * (1.0 + scale) + shift


def _mod_spec(k, ctx):
    if ctx:
        return pl.BlockSpec((1, 1, 1, D_MODEL), lambda b, *_: (CTX_ROW, k, 0, 0))
    return pl.BlockSpec((1, 1, 1, D_MODEL), lambda b, *_: (b, k, 0, 0))


FFN_TF = 256


def _ffn_kernel(*refs, with_attn, n_cast):
    refs = list(refs)
    h_ref = refs.pop(0)
    if with_attn:
        x_ref = h_ref
        agate_ref, oa_ref, ob_ref, wo_ref = (refs.pop(0) for _ in range(4))
    shift_ref, scale_ref, gate_ref, g_ref, wgu_ref, wd_ref = (refs.pop(0) for _ in range(6))
    cast_in = [refs.pop(0) for _ in range(n_cast)]
    o_ref = refs.pop(0)
    cast_out = [refs.pop(0) for _ in range(n_cast)]
    xm_ref, acc_ref = refs[:2]
    for src, dst in zip(cast_in, cast_out):
        dst[...] = src[...].astype(BF16)
    if with_attn:
        h_ref = refs[2]
        half = oa_ref.shape[-1]
        attn = (jnp.dot(oa_ref[0], wo_ref[0:half, :], preferred_element_type=F32)
                + jnp.dot(ob_ref[0], wo_ref[half:, :], preferred_element_type=F32))
        h_ref[0] = x_ref[0] + agate_ref[0, 0] * attn
    f = wd_ref.shape[0]
    nj = f // FFN_TF
    xm_ref[...] = _modulated_norm(h_ref[0], g_ref[...], shift_ref[0, 0], scale_ref[0, 0]).astype(BF16)

    def act(j):
        xm = xm_ref[...]
        g = jnp.dot(xm, wgu_ref[:, j * FFN_TF:(j + 1) * FFN_TF], preferred_element_type=F32)
        u = jnp.dot(xm, wgu_ref[:, f + j * FFN_TF:f + (j + 1) * FFN_TF], preferred_element_type=F32)
        return (g * jax.nn.sigmoid(g) * u).astype(BF16)

    a = act(0)
    for j in range(nj):
        down = jnp.dot(a, wd_ref[j * FFN_TF:(j + 1) * FFN_TF, :], preferred_element_type=F32)
        if j + 1 < nj:
            a = act(j + 1)
        if j == 0:
            acc_ref[...] = down
        elif j + 1 < nj:
            acc_ref[...] += down
        else:
            o_ref[0] = h_ref[0] + (0.5 * gate_ref[0, 0]) * (acc_ref[...] + down)


def _ffn_weights(w_gu, w_down):
    return w_gu.astype(BF16), w_down.astype(BF16)


def _resident(shape):
    return pl.BlockSpec(shape, lambda *_: (0,) * len(shape), pipeline_mode=pl.Buffered(1))


def _ffn(h, mod4, k0, ctx, g, weights, *, tm, attn=None, casts=()):
    bsz, t, d = h.shape
    wgu, wd = weights
    assert wd.shape[0] % FFN_TF == 0
    n_i = t // tm
    tile = pl.BlockSpec((1, tm, d), lambda b, i: (b, i, 0))
    in_specs, args = [tile], [h]
    scratch = [pltpu.VMEM((tm, d), BF16), pltpu.VMEM((tm, d), F32)]
    if attn is not None:
        k_gate, oa, ob, w_out = attn
        half = oa.shape[-1]
        in_specs += [_mod_spec(k_gate, ctx), pl.BlockSpec((1, tm, half), lambda b, i: (b, i, 0)),
                     pl.BlockSpec((1, tm, half), lambda b, i: (b, i, 0)), _resident(w_out.shape)]
        args += [mod4, oa, ob, w_out]
        scratch.append(pltpu.VMEM((1, tm, d), F32))
    in_specs += [_mod_spec(k0, ctx), _mod_spec(k0 + 1, ctx), _mod_spec(k0 + 2, ctx),
                 pl.BlockSpec((1, d), lambda b, i: (0, 0)), _resident(wgu.shape), _resident(wd.shape)]
    args += [mod4, mod4, mod4, g, wgu, wd]
    slabs = []
    for w in casts:
        rows_per_step = w.shape[0] // (bsz * n_i)
        assert rows_per_step * bsz * n_i == w.shape[0] and rows_per_step % BF16_ROWS == 0
        slabs.append(pl.BlockSpec((rows_per_step, w.shape[1]), lambda b, i: (b * n_i + i, 0)))
    out = pl.pallas_call(
        functools.partial(_ffn_kernel, with_attn=attn is not None, n_cast=len(casts)),
        grid=(bsz, n_i),
        in_specs=in_specs + slabs,
        out_specs=[tile] + slabs,
        out_shape=[jax.ShapeDtypeStruct(h.shape, F32)] + [jax.ShapeDtypeStruct(w.shape, BF16) for w in casts],
        scratch_shapes=scratch,
        compiler_params=_cparams(("parallel", "parallel")),
        name="ffn_ctx" if ctx else "ffn",
    )(*args, *casts)
    return out if casts else out[0]


def _head_rms_scale(y, bd_ref):
    y2 = (y * y).astype(BF16)
    bd = bd_ref[...]
    w = bd.shape[0]
    ss = jnp.concatenate(
        [jnp.dot(y2[:, c:c + w], bd, preferred_element_type=F32) for c in range(0, y.shape[1], w)], axis=-1)
    return lax.rsqrt(ss * (1.0 / HEAD_DIM) + NORM_EPS)


def _rope(z, cos, sin):
    lane = lax.broadcasted_iota(jnp.int32, z.shape, 1)
    up = pltpu.roll(z, LANES - 16, 1)
    dn = pltpu.roll(z, 16, 1)
    return z * cos + jnp.where((lane & 16) == 0, up, dn) * sin


def _in_proj_kernel(*refs, groups, n_gain):
    h_ref, shift_ref, scale_ref, g_ref, w_ref, bd_ref, cos_ref, sin_ref = refs[:8]
    gain_refs = refs[8:8 + n_gain]
    out_refs = refs[8 + n_gain:]
    xm = _modulated_norm(h_ref[0], g_ref[...], shift_ref[0, 0], scale_ref[0, 0]).astype(BF16)
    for (col, gain_idx, rope, qscale), o_ref in zip(groups, out_refs):
        y = jnp.dot(xm, w_ref[:, col * GROUP_W:(col + 1) * GROUP_W], preferred_element_type=F32)
        if gain_idx is not None:
            gain = gain_refs[gain_idx][...]
            if qscale != 1.0:
                gain = gain * qscale
            y = y * _head_rms_scale(y, bd_ref) * gain
        if rope:
            cos = cos_ref[...]
            sin = sin_ref[...]
            for c in range(GROUP_W // LANES):
                sl = slice(c * LANES, (c + 1) * LANES)
                o_ref[0, :, sl] = _rope(y[:, sl], cos, sin).astype(BF16)
        else:
            o_ref[0] = y.astype(BF16)


def _in_proj(h, mod4, ctx, g, w_in, bd, cos, sin, gains, groups, *, tm):
    bsz, t, d = h.shape
    n_out = len(groups)
    in_specs = [pl.BlockSpec((1, tm, d), lambda b, i: (b, i, 0)),
                _mod_spec(3, ctx), _mod_spec(4, ctx),
                pl.BlockSpec((1, d), lambda b, i: (0, 0)),
                pl.BlockSpec(w_in.shape, lambda b, i: (0, 0)),
                pl.BlockSpec(bd.shape, lambda b, i: (0, 0)),
                pl.BlockSpec((tm, LANES), lambda b, i: (i, 0)),
                pl.BlockSpec((tm, LANES), lambda b, i: (i, 0))]
    in_specs += [pl.BlockSpec((1, GROUP_W), lambda b, i: (0, 0)) for _ in gains]
    return pl.pallas_call(
        functools.partial(_in_proj_kernel, groups=groups, n_gain=len(gains)),
        grid=(bsz, t // tm),
        in_specs=in_specs,
        out_specs=[pl.BlockSpec((1, tm, GROUP_W), lambda b, i: (b, i, 0)) for _ in range(n_out)],
        out_shape=[jax.ShapeDtypeStruct((bsz, t, GROUP_W), BF16) for _ in range(n_out)],
        compiler_params=_cparams(("parallel", "parallel")),
        name="in_proj_ctx" if ctx else "in_proj",
    )(h, mod4, mod4, g, w_in, bd, cos, sin, *gains)


def _split_halves(q):
    qt = q.astype(F32).T
    row = lax.broadcasted_iota(jnp.int32, qt.shape, 0)
    lo = jnp.where(row < HEAD_DIM, qt, 0.0).astype(BF16)
    hi = jnp.where(row >= HEAD_DIM, qt, 0.0).astype(BF16)
    return lo, hi


DIFF_TQ = 256
DIFF_TK = 256
SUBLANES = 8


ATTN_SAMPLES = 2


def _locate(i, n_blk):
    if isinstance(i, int):
        return divmod(i, n_blk)
    return lax.shift_right_logical(i, n_blk.bit_length() - 1), i & (n_blk - 1)


def _interleave(*stages):
    order = sorted(((k + 0.5) / len(st), n, k) for n, st in enumerate(stages) for k in range(len(st)))
    for _, n, k in order:
        stages[n][k]()


def _diff_pipeline(q_ref, kc_ref, kl_ref, vc_ref, vl_ref, lamv_ref, gcol_ref, o_ref,
                   kk_ref, vte_ref, qm_ref, s_ref, m_ref):
    n_smp = q_ref.shape[0]
    n_ctx = kc_ref.shape[1]
    n_tick = kk_ref.shape[1] // DIFF_TK
    n_blk = q_ref.shape[1] // DIFF_TQ
    for smp in range(n_smp):
        kk_ref[smp, 0:n_ctx] = kc_ref[smp]
        kk_ref[smp, n_ctx:] = kl_ref[smp]

    def stage_v(smp, kc):
        lo = kc * DIFF_TK
        v = vc_ref[smp, lo:lo + DIFF_TK] if lo < n_ctx else vl_ref[smp, lo - n_ctx:lo - n_ctx + DIFF_TK]
        vte_ref[smp, kc] = v.astype(F32).T.astype(BF16)

    lv = lamv_ref[...]
    lam = (jnp.exp(jnp.sum(lv[0:1] * lv[1:2], axis=-1, keepdims=True))
           - jnp.exp(jnp.sum(lv[2:3] * lv[3:4], axis=-1, keepdims=True)) + LAM_INIT)

    def rows(i):
        smp, li = _locate(i, n_blk)
        return smp, pl.ds(pl.multiple_of(li * DIFF_TQ, DIFF_TQ), DIFF_TQ)

    def keys(kc):
        return slice(kc * DIFF_TK, (kc + 1) * DIFF_TK)

    def prep(i, slot):
        smp, r = rows(i)
        lo, hi = _split_halves(q_ref[smp, r, :])
        qm_ref[slot, 0] = lo
        qm_ref[slot, 1] = hi
        m_ref[slot] = jnp.full(m_ref.shape[1:], -jnp.inf, F32)

    def score_tick(slot, i, kc):
        k = kk_ref[_locate(i, n_blk)[0], keys(kc), :]
        for c in range(2):
            s = jnp.dot(k, qm_ref[slot, c], preferred_element_type=F32)
            s_ref[slot, c, keys(kc), :] = s
            part = jnp.max(s.reshape(DIFF_TK // SUBLANES, SUBLANES, DIFF_TQ), axis=0)
            m_ref[slot, c] = jnp.maximum(m_ref[slot, c], part)

    def final_max(slot):
        return [jnp.broadcast_to(jnp.max(m_ref[slot, c], axis=0, keepdims=True), (SUBLANES, DIFF_TQ))
                for c in range(2)]

    def pv_tick(slot, i, kc, mx, acc):
        vt = vte_ref[_locate(i, n_blk)[0], kc]
        for c in range(2):
            s = s_ref[slot, c, keys(kc), :].reshape(DIFF_TK // SUBLANES, SUBLANES, DIFF_TQ)
            p = jnp.exp2(s - mx[c])
            d = [jnp.dot(vt, p.reshape(DIFF_TK, DIFF_TQ).astype(BF16), preferred_element_type=F32),
                 jnp.sum(p, axis=0)]
            acc[c] = d if acc[c] is None else [a + b for a, b in zip(acc[c], d)]

    def finish(i, acc):
        l0 = jnp.sum(acc[0][1], axis=0, keepdims=True)
        l1 = jnp.sum(acc[1][1], axis=0, keepdims=True)
        ot = acc[0][0] * (1.0 / l0) - acc[1][0] * (lam / l1)
        ms = jnp.mean(ot * ot, axis=0, keepdims=True)
        on = ot * lax.rsqrt(ms + NORM_EPS) * (gcol_ref[...] * (1.0 - LAM_INIT))
        smp, r = rows(i)
        o_ref[smp, r, :] = on.T.astype(BF16)

    def start():
        return ([functools.partial(prep, 0, 0)]
                + [functools.partial(score_tick, 0, 0, kc) for kc in range(n_tick)],
                [functools.partial(stage_v, smp, kc) for smp in range(n_smp) for kc in range(n_tick)])

    def block(i, slot, last=False):
        state = {}

        def head():
            state["mx"] = final_max(slot)
            state["acc"] = [None, None]
            if not last:
                prep(i + 1, 1 - slot)

        def tick(kc):
            pv_tick(slot, i, kc, state["mx"], state["acc"])
            if not last:
                score_tick(1 - slot, i + 1, kc)

        return ([head] + [functools.partial(tick, kc) for kc in range(n_tick)]
                + [lambda: finish(i, state["acc"])])

    return start, block


def _diff_scratch(c, s):
    return [pltpu.VMEM((ATTN_SAMPLES, c + s, LANES), BF16),
            pltpu.VMEM((ATTN_SAMPLES, (c + s) // DIFF_TK, LANES, DIFF_TK), BF16),
            pltpu.VMEM((2, 2, LANES, DIFF_TQ), BF16),
            pltpu.VMEM((2, 2, c + s, DIFF_TQ), F32),
            pltpu.VMEM((2, 2, SUBLANES, DIFF_TQ), F32)]


NA_QROWS = 4
NA_KROWS = 12
NA_TQ = NA_QROWS * GRID_W
NA_TK = 4 * GRID_W
NA_WIN_TICKS = NA_KROWS * GRID_W // NA_TK


def _na_variants(rows):
    n_rb = rows // NA_QROWS
    out = []
    for rb in (0, 1, n_rb - 1):
        ws = min(max(rb * NA_QROWS - NA_KH // 2, 0), rows - NA_KROWS)
        out.append((rb * NA_QROWS, ws))
    return out


def _na_build_bias(rpb_ref, bias_ref, rows):
    shape = (GRID_W, LANES)
    kc = lax.broadcasted_iota(jnp.int32, shape, 0)
    lane = lax.broadcasted_iota(jnp.int32, shape, 1)
    qc = lane & (GRID_W - 1)
    cs = jnp.clip(qc - NA_KW // 2, 0, GRID_W - NA_KW)
    col_ok = (kc >= cs) & (kc < cs + NA_KW)
    side_ok = {(True, True): col_ok, (True, False): col_ok & (lane < GRID_W),
               (False, True): col_ok & (lane >= GRID_W)}
    neg = jnp.full(shape, NEG_INF, F32)
    for hh in range(2):
        toeplitz = {}
        for v, (r0, ws) in enumerate(_na_variants(rows)):
            for i in range(NA_KROWS):
                kr = ws + i
                for jp in range(NA_QROWS // 2):
                    ok = []
                    for r in (r0 + 2 * jp, r0 + 2 * jp + 1):
                        rs = min(max(r - NA_KH // 2, 0), rows - NA_KH)
                        ok.append(rs <= kr < rs + NA_KH)
                    tile = neg
                    if any(ok):
                        e = kr - (r0 + 2 * jp) + NA_KH - 1
                        if e not in toeplitz:
                            vec = jnp.broadcast_to(rpb_ref[0, hh, e:e + 1, :] * LOG2E, shape)
                            toeplitz[e] = pltpu.roll(vec, LANES - (NA_KW - 1), 1, stride=1, stride_axis=0)
                        tile = jnp.where(side_ok[tuple(ok)], toeplitz[e], neg)
                    bias_ref[hh, v, i * GRID_W:(i + 1) * GRID_W, jp * LANES:(jp + 1) * LANES] = tile


def _na_pipeline(q_ref, kc_ref, k_ref, vc_ref, v_ref, rpb_ref, o_ref,
                 bias_ref, vte_ref, qm_ref, s_ref, m_ref, *, rows):
    n_blk = rows // NA_QROWS
    n_lat = v_ref.shape[1] // NA_TK
    assert kc_ref.shape[1] == NA_TK
    n_tick = NA_WIN_TICKS + 1

    @pl.when(pl.program_id(1) == 0)
    def _():
        _na_build_bias(rpb_ref, bias_ref, rows)

    n_smp = q_ref.shape[0]

    def stage_v(smp, ch):
        v = v_ref[smp, ch * NA_TK:(ch + 1) * NA_TK] if ch < n_lat else vc_ref[smp]
        vt = v.astype(F32).T
        for hh in range(2):
            vte_ref[smp, hh, ch] = vt[hh * HEAD_DIM:(hh + 1) * HEAD_DIM].astype(BF16)

    def rows_of(i):
        smp, li = _locate(i, n_blk)
        return smp, pl.ds(pl.multiple_of(li * NA_TQ, NA_TQ), NA_TQ)

    def window(i):
        smp, li = _locate(i, n_blk)
        return (smp, jnp.clip(li - 1, 0, n_lat - NA_WIN_TICKS),
                jnp.where(li == 0, 0, jnp.where(li == n_blk - 1, 2, 1)))

    def prep(i, slot):
        smp, r = rows_of(i)
        lo, hi = _split_halves(q_ref[smp, r, :])
        qm_ref[slot, 0] = lo
        qm_ref[slot, 1] = hi
        m_ref[slot] = jnp.full(m_ref.shape[1:], -jnp.inf, F32)

    def score_tick(slot, i, t):
        smp, ch0, variant = window(i)
        if t < NA_WIN_TICKS:
            k = k_ref[smp, pl.ds(pl.multiple_of((ch0 + t) * NA_TK, NA_TK), NA_TK), :]
        else:
            k = kc_ref[smp]
        for hh in range(2):
            s = jnp.dot(k, qm_ref[slot, hh], preferred_element_type=F32)
            if t < NA_WIN_TICKS:
                s = s + bias_ref[hh, variant, t * NA_TK:(t + 1) * NA_TK, :]
            s_ref[slot, hh, t * NA_TK:(t + 1) * NA_TK, :] = s
            part = jnp.max(s.reshape(NA_TK // SUBLANES, SUBLANES, NA_TQ), axis=0)
            m_ref[slot, hh] = jnp.maximum(m_ref[slot, hh], part)

    def final_max(slot):
        return [jnp.broadcast_to(jnp.max(m_ref[slot, hh], axis=0, keepdims=True), (SUBLANES, NA_TQ))
                for hh in range(2)]

    def pv_tick(slot, i, t, mx, acc):
        smp, ch0, _ = window(i)
        ch = ch0 + t if t < NA_WIN_TICKS else n_lat
        for hh in range(2):
            s = s_ref[slot, hh, t * NA_TK:(t + 1) * NA_TK, :].reshape(NA_TK // SUBLANES, SUBLANES, NA_TQ)
            p = jnp.exp2(s - mx[hh])
            d = [jnp.dot(vte_ref[smp, hh, ch], p.reshape(NA_TK, NA_TQ).astype(BF16), preferred_element_type=F32),
                 jnp.sum(p, axis=0)]
            acc[hh] = d if acc[hh] is None else [a + b for a, b in zip(acc[hh], d)]

    def finish(i, acc):
        ot = jnp.concatenate(
            [acc[hh][0] * (1.0 / jnp.sum(acc[hh][1], axis=0, keepdims=True)) for hh in range(2)],
            axis=0)
        smp, r = rows_of(i)
        o_ref[smp, r, :] = ot.T.astype(BF16)

    def start():
        return ([functools.partial(prep, 0, 0)]
                + [functools.partial(score_tick, 0, 0, t) for t in range(n_tick)],
                [functools.partial(stage_v, smp, ch) for smp in range(n_smp) for ch in range(n_lat + 1)])

    def block(i, slot, last=False):
        state = {}

        def head():
            state["mx"] = final_max(slot)
            state["acc"] = [None, None]
            if not last:
                prep(i + 1, 1 - slot)

        def tick(t):
            pv_tick(slot, i, t, state["mx"], state["acc"])
            if not last:
                score_tick(1 - slot, i + 1, t)

        return ([head] + [functools.partial(tick, t) for t in range(n_tick)]
                + [lambda: finish(i, state["acc"])])

    return start, block


def _na_scratch(n_lat):
    return [pltpu.VMEM((2, 3, NA_KROWS * GRID_W, NA_TQ), F32),
            pltpu.VMEM((ATTN_SAMPLES, 2, n_lat + 1, HEAD_DIM, NA_TK), BF16),
            pltpu.VMEM((2, 2, LANES, NA_TQ), BF16),
            pltpu.VMEM((2, 2, (NA_WIN_TICKS + 1) * NA_TK, NA_TQ), F32),
            pltpu.VMEM((2, 2, SUBLANES, NA_TQ), F32)]


N_DIFF_IN, N_NA_IN = 7, 6


def _attn_kernel(*refs, rows):
    d_in = refs[:N_DIFF_IN]
    n_in = refs[N_DIFF_IN:N_DIFF_IN + N_NA_IN]
    oa_ref, ob_ref = refs[N_DIFF_IN + N_NA_IN:N_DIFF_IN + N_NA_IN + 2]
    scratch = refs[N_DIFF_IN + N_NA_IN + 2:]
    d_start, d_block = _diff_pipeline(*d_in, oa_ref, *scratch[:5])
    n_start, n_block = _na_pipeline(*n_in, ob_ref, *scratch[5:], rows=rows)
    n_blk = oa_ref.shape[0] * (rows // NA_QROWS)
    _interleave(*d_start(), *n_start())

    def pair(u, carry):
        for j in range(2):
            _interleave(d_block(2 * u + j, j), n_block(2 * u + j, j))
        return carry

    lax.fori_loop(0, n_blk // 2 - 1, pair, 0)
    _interleave(d_block(n_blk - 2, 0), n_block(n_blk - 2, 0))
    _interleave(d_block(n_blk - 1, 1, last=True), n_block(n_blk - 1, 1, last=True))


def _attention(aq, ak, av, akc, avc, lamv, gout, nq, nk, nv, nkc, nvc, rpb):
    bsz, s, _ = aq.shape
    c = akc.shape[1]
    rows = s // GRID_W
    n_blk = rows // NA_QROWS
    assert NA_HEADS // 2 == DIFF_HEADS and NA_TQ == DIFF_TQ and NA_QROWS * GRID_W == NA_TK and c == NA_TK
    assert n_blk >= 4 and n_blk & (n_blk - 1) == 0 and c % DIFF_TK == 0 and s % DIFF_TK == 0
    assert bsz % ATTN_SAMPLES == 0
    n_r, n_c = rpb.shape[1:]
    padded = jnp.pad(rpb[:, :, ::-1], ((0, 0), (1, 1), (0, GRID_W - n_c)))
    rpb2 = jnp.concatenate([padded[:, 1:], padded[:, :-1]], axis=-1).reshape(DIFF_HEADS, 2, n_r + 1, LANES)
    lat = pl.BlockSpec((ATTN_SAMPLES, s, LANES), lambda h, b: (b, 0, h))
    ctx = pl.BlockSpec((ATTN_SAMPLES, c, LANES), lambda h, b: (b, 0, h))
    return pl.pallas_call(
        functools.partial(_attn_kernel, rows=rows),
        grid=(DIFF_HEADS, bsz // ATTN_SAMPLES),
        in_specs=[lat, ctx, lat, ctx, lat,
                  pl.BlockSpec(lamv.shape, lambda h, b: (0, 0)),
                  pl.BlockSpec((LANES, 1), lambda h, b: (0, 0)),
                  lat, ctx, lat, ctx, lat,
                  pl.BlockSpec((1,) + rpb2.shape[1:], lambda h, b: (h, 0, 0, 0))],
        out_specs=[lat, lat],
        out_shape=[jax.ShapeDtypeStruct(aq.shape, BF16), jax.ShapeDtypeStruct(nq.shape, BF16)],
        scratch_shapes=_diff_scratch(c, s) + _na_scratch(s // NA_TK),
        compiler_params=_cparams(("arbitrary", "arbitrary")),
        name="attention",
    )(aq, akc, ak, avc, av, lamv, gout.reshape(LANES, 1), nq, nkc, nk, nvc, nv, rpb2)


def _rope_tables(n_tokens):
    t = jnp.arange(n_tokens, dtype=jnp.int32)
    row = (t // GRID_W).astype(F32)
    col = (t % GRID_W).astype(F32)
    n_freq = HEAD_DIM // 4
    inv_freq = ROPE_BASE ** (-jnp.arange(n_freq, dtype=F32) / n_freq)
    ar = row[:, None] * inv_freq
    ac = col[:, None] * inv_freq
    cos = jnp.concatenate([jnp.cos(ar), jnp.cos(ar), jnp.cos(ac), jnp.cos(ac)], axis=-1)
    sin = jnp.concatenate([-jnp.sin(ar), jnp.sin(ar), -jnp.sin(ac), jnp.sin(ac)], axis=-1)
    reps = LANES // HEAD_DIM
    return jnp.tile(cos, (1, reps)), jnp.tile(sin, (1, reps))


def kernel(x, c, ctx, c_ctx, w_ada, b_ada, norm1, norm2, norm3, ffn1_w_gu, ffn1_w_down, w_in,
           diff_q_norm, diff_k_norm, lam_q1, lam_k1, lam_q2, lam_k2, diff_out_norm,
           na_q_norm, na_k_norm, na_rpb, w_out, ffn2_w_gu, ffn2_w_down):
    bsz, s, d = x.shape
    assert w_ada.shape[0] == 1 and d == D_MODEL and s % (NA_QROWS * GRID_W) == 0
    rows = s // GRID_W

    assert bsz == CTX_ROW < MOD_ROWS
    cond = jnp.zeros((MOD_ROWS, d), F32).at[:bsz].set(c).at[CTX_ROW].set(c_ctx)
    mod4 = _adaln(cond, w_ada[0], b_ada).reshape(MOD_ROWS, N_MOD, 1, d)

    ffn1_w = _ffn_weights(ffn1_w_gu[0], ffn1_w_down[0])
    h, w_in_b, w_out_b, w_gu2, w_dn2 = _ffn(x, mod4, 0, False, norm1, ffn1_w, tm=TOKEN_TILE,
                                            casts=(w_in[0], w_out[0], ffn2_w_gu[0], ffn2_w_down[0]))
    n_ctx = ctx.shape[1]
    hc = _ffn(ctx.reshape(1, bsz * n_ctx, d), mod4, 0, True, norm1, ffn1_w, tm=TOKEN_TILE)

    seg = np.arange(MXU_W) // HEAD_DIM
    bd = jnp.asarray(seg[:, None] == seg[None, :], BF16)
    cos, sin = _rope_tables(s)
    reps = GROUP_W // HEAD_DIM
    gains = [jnp.tile(g, (1, reps)) for g in (diff_q_norm, diff_k_norm, na_q_norm, na_k_norm)]
    qs = HEAD_DIM ** -0.5
    lat_groups = ((0, 0, True, qs * LOG2E), (1, 1, True, 1.0), (2, None, False, 1.0),
                  (3, 2, False, qs * LOG2E), (4, 3, False, 1.0), (5, None, False, 1.0))
    aq, ak, av, nq, nk, nv = _in_proj(h, mod4, False, norm2, w_in_b, bd, cos, sin, gains,
                                      lat_groups, tm=TOKEN_TILE)
    ctx_groups = ((1, 1, False, 1.0), (2, None, False, 1.0), (4, 3, False, 1.0), (5, None, False, 1.0))
    ctx_kv = _in_proj(hc, mod4, True, norm2, w_in_b, bd, cos, sin, gains, ctx_groups, tm=CTX_PROJ_TILE)
    akc, avc, nkc, nvc = (a.reshape(bsz, n_ctx, GROUP_W) for a in ctx_kv)

    lamv = jnp.concatenate([lam_q1, lam_k1, lam_q2, lam_k2], axis=0)
    oa, ob = _attention(aq, ak, av, akc, avc, lamv, diff_out_norm, nq, nk, nv, nkc, nvc, na_rpb[0])

    return _ffn(h, mod4, 6, False, norm3, (w_gu2, w_dn2), tm=TOKEN_TILE, attn=(5, oa, ob, w_out_b))
```

```python
import functools
import math

import numpy as np
import jax
import jax.numpy as jnp
from jax import lax
from jax.experimental import pallas as pl
from jax.experimental.pallas import tpu as pltpu

D_MODEL = 1024
GRID_W = 64
HEAD_DIM = 64
DIFF_HEADS = 4
NA_HEADS = 8
GROUP_W = 512
N_MOD = 9
NA_KH = 8
NA_KW = 16
ROPE_BASE = 10000.0
NORM_EPS = 1e-6
NEG_INF = -1e30
LAM_INIT = 0.8 - 0.6 * math.exp(-0.3 * 0)
LOG2E = math.log2(math.e)

LANES = 128
BF16_ROWS = 16
MOD_ROWS = 16
CTX_ROW = 8
TOKEN_TILE = 1024
CTX_PROJ_TILE = 512
MXU_W = 256
VMEM_LIMIT = 56 * 1024 * 1024

F32 = jnp.float32
BF16 = jnp.bfloat16


def _cparams(sem):
    return pltpu.CompilerParams(dimension_semantics=sem, vmem_limit_bytes=VMEM_LIMIT)


ADALN_TK = 128


def _adaln_kernel(cond_ref, w_ref, b_ref, o_ref):
    k = pl.program_id(0)
    c = cond_ref[...]
    a = (c * jax.nn.sigmoid(c)).astype(BF16)
    part = jnp.dot(a, w_ref[...].astype(BF16), preferred_element_type=F32)

    @pl.when(k == 0)
    def _():
        o_ref[...] = part + b_ref[...]

    @pl.when(k > 0)
    def _():
        o_ref[...] += part


def _adaln(cond, w_ada, b_ada):
    rows, d = cond.shape
    n = w_ada.shape[1]
    return pl.pallas_call(
        _adaln_kernel,
        grid=(d // ADALN_TK,),
        in_specs=[pl.BlockSpec((rows, ADALN_TK), lambda k: (0, k)),
                  pl.BlockSpec((ADALN_TK, n), lambda k: (k, 0)),
                  pl.BlockSpec((1, n), lambda k: (0, 0))],
        out_specs=pl.BlockSpec((rows, n), lambda k: (0, 0)),
        out_shape=jax.ShapeDtypeStruct((rows, n), F32),
        compiler_params=_cparams(("arbitrary",)),
        name="adaln",
    )(cond, w_ada, b_ada)


def _modulated_norm(x, g, shift, scale):
    ms = jnp.mean(x * x, axis=-1, keepdims=True)
    xn = x * lax.rsqrt(ms + NORM_EPS) * g
    return xn * (1.0 + scale) + shift


def _mod_spec(k, ctx):
    if ctx:
        return pl.BlockSpec((1, 1, 1, D_MODEL), lambda b, *_: (CTX_ROW, k, 0, 0))
    return pl.BlockSpec((1, 1, 1, D_MODEL), lambda b, *_: (b, k, 0, 0))


FFN_TF = 256


def _ffn_kernel(*refs, with_attn, n_cast):
    refs = list(refs)
    h_ref = refs.pop(0)
    if with_attn:
        x_ref = h_ref
        agate_ref, oa_ref, ob_ref, wo_ref = (refs.pop(0) for _ in range(4))
    shift_ref, scale_ref, gate_ref, g_ref, wgu_ref, wd_ref = (refs.pop(0) for _ in range(6))
    cast_in = [refs.pop(0) for _ in range(n_cast)]
    o_ref = refs.pop(0)
    cast_out = [refs.pop(0) for _ in range(n_cast)]
    xm_ref, acc_ref = refs[:2]
    for src, dst in zip(cast_in, cast_out):
        dst[...] = src[...].astype(BF16)
    if with_attn:
        h_ref = refs[2]
        half = oa_ref.shape[-1]
        attn = (jnp.dot(oa_ref[0], wo_ref[0:half, :], preferred_element_type=F32)
                + jnp.dot(ob_ref[0], wo_ref[half:, :], preferred_element_type=F32))
        h_ref[0] = x_ref[0] + agate_ref[0, 0] * attn
    f = wd_ref.shape[0]
    nj = f // FFN_TF
    xm_ref[...] = _modulated_norm(h_ref[0], g_ref[...], shift_ref[0, 0], scale_ref[0, 0]).astype(BF16)

    def act(j):
        xm = xm_ref[...]
        g = jnp.dot(xm, wgu_ref[:, j * FFN_TF:(j + 1) * FFN_TF], preferred_element_type=F32)
        u = jnp.dot(xm, wgu_ref[:, f + j * FFN_TF:f + (j + 1) * FFN_TF], preferred_element_type=F32)
        return (g * jax.nn.sigmoid(g) * u).astype(BF16)

    a = act(0)
    for j in range(nj):
        down = jnp.dot(a, wd_ref[j * FFN_TF:(j + 1) * FFN_TF, :], preferred_element_type=F32)
        if j + 1 < nj:
            a = act(j + 1)
        if j == 0:
            acc_ref[...] = down
        elif j + 1 < nj:
            acc_ref[...] += down
        else:
            o_ref[0] = h_ref[0] + (0.5 * gate_ref[0, 0]) * (acc_ref[...] + down)


def _ffn_weights(w_gu, w_down):
    return w_gu.astype(BF16), w_down.astype(BF16)


def _resident(shape):
    return pl.BlockSpec(shape, lambda *_: (0,) * len(shape), pipeline_mode=pl.Buffered(1))


def _ffn(h, mod4, k0, ctx, g, weights, *, tm, attn=None, casts=()):
    bsz, t, d = h.shape
    wgu, wd = weights
    assert wd.shape[0] % FFN_TF == 0
    n_i = t // tm
    tile = pl.BlockSpec((1, tm, d), lambda b, i: (b, i, 0))
    in_specs, args = [tile], [h]
    scratch = [pltpu.VMEM((tm, d), BF16), pltpu.VMEM((tm, d), F32)]
    if attn is not None:
        k_gate, oa, ob, w_out = attn
        half = oa.shape[-1]
        in_specs += [_mod_spec(k_gate, ctx), pl.BlockSpec((1, tm, half), lambda b, i: (b, i, 0)),
                     pl.BlockSpec((1, tm, half), lambda b, i: (b, i, 0)), _resident(w_out.shape)]
        args += [mod4, oa, ob, w_out]
        scratch.append(pltpu.VMEM((1, tm, d), F32))
    in_specs += [_mod_spec(k0, ctx), _mod_spec(k0 + 1, ctx), _mod_spec(k0 + 2, ctx),
                 pl.BlockSpec((1, d), lambda b, i: (0, 0)), _resident(wgu.shape), _resident(wd.shape)]
    args += [mod4, mod4, mod4, g, wgu, wd]
    slabs = []
    for w in casts:
        rows_per_step = w.shape[0] // (bsz * n_i)
        assert rows_per_step * bsz * n_i == w.shape[0] and rows_per_step % BF16_ROWS == 0
        slabs.append(pl.BlockSpec((rows_per_step, w.shape[1]), lambda b, i: (b * n_i + i, 0)))
    out = pl.pallas_call(
        functools.partial(_ffn_kernel, with_attn=attn is not None, n_cast=len(casts)),
        grid=(bsz, n_i),
        in_specs=in_specs + slabs,
        out_specs=[tile] + slabs,
        out_shape=[jax.ShapeDtypeStruct(h.shape, F32)] + [jax.ShapeDtypeStruct(w.shape, BF16) for w in casts],
        scratch_shapes=scratch,
        compiler_params=_cparams(("parallel", "parallel")),
        name="ffn_ctx" if ctx else "ffn",
    )(*args, *casts)
    return out if casts else out[0]


def _head_rms_scale(y, bd_ref):
    y2 = (y * y).astype(BF16)
    bd = bd_ref[...]
    w = bd.shape[0]
    ss = jnp.concatenate(
        [jnp.dot(y2[:, c:c + w], bd, preferred_element_type=F32) for c in range(0, y.shape[1], w)], axis=-1)
    return lax.rsqrt(ss * (1.0 / HEAD_DIM) + NORM_EPS)


def _rope(z, cos, sin):
    lane = lax.broadcasted_iota(jnp.int32, z.shape, 1)
    up = pltpu.roll(z, LANES - 16, 1)
    dn = pltpu.roll(z, 16, 1)
    return z * cos + jnp.where((lane & 16) == 0, up, dn) * sin


def _in_proj_kernel(*refs, groups, n_gain):
    h_ref, shift_ref, scale_ref, g_ref, w_ref, bd_ref, cos_ref, sin_ref = refs[:8]
    gain_refs = refs[8:8 + n_gain]
    out_refs = refs[8 + n_gain:]
    xm = _modulated_norm(h_ref[0], g_ref[...], shift_ref[0, 0], scale_ref[0, 0]).astype(BF16)
    for (col, gain_idx, rope, qscale), o_ref in zip(groups, out_refs):
        y = jnp.dot(xm, w_ref[:, col * GROUP_W:(col + 1) * GROUP_W], preferred_element_type=F32)
        if gain_idx is not None:
            gain = gain_refs[gain_idx][...]
            if qscale != 1.0:
                gain = gain * qscale
            y = y * _head_rms_scale(y, bd_ref) * gain
        if rope:
            cos = cos_ref[...]
            sin = sin_ref[...]
            for c in range(GROUP_W // LANES):
                sl = slice(c * LANES, (c + 1) * LANES)
                o_ref[0, :, sl] = _rope(y[:, sl], cos, sin).astype(BF16)
        else:
            o_ref[0] = y.astype(BF16)


def _in_proj(h, mod4, ctx, g, w_in, bd, cos, sin, gains, groups, *, tm):
    bsz, t, d = h.shape
    n_out = len(groups)
    in_specs = [pl.BlockSpec((1, tm, d), lambda b, i: (b, i, 0)),
                _mod_spec(3, ctx), _mod_spec(4, ctx),
                pl.BlockSpec((1, d), lambda b, i: (0, 0)),
                pl.BlockSpec(w_in.shape, lambda b, i: (0, 0)),
                pl.BlockSpec(bd.shape, lambda b, i: (0, 0)),
                pl.BlockSpec((tm, LANES), lambda b, i: (i, 0)),
                pl.BlockSpec((tm, LANES), lambda b, i: (i, 0))]
    in_specs += [pl.BlockSpec((1, GROUP_W), lambda b, i: (0, 0)) for _ in gains]
    return pl.pallas_call(
        functools.partial(_in_proj_kernel, groups=groups, n_gain=len(gains)),
        grid=(bsz, t // tm),
        in_specs=in_specs,
        out_specs=[pl.BlockSpec((1, tm, GROUP_W), lambda b, i: (b, i, 0)) for _ in range(n_out)],
        out_shape=[jax.ShapeDtypeStruct((bsz, t, GROUP_W), BF16) for _ in range(n_out)],
        compiler_params=_cparams(("parallel", "parallel")),
        name="in_proj_ctx" if ctx else "in_proj",
    )(h, mod4, mod4, g, w_in, bd, cos, sin, *gains)


def _split_halves(q):
    qt = q.astype(F32).T
    row = lax.broadcasted_iota(jnp.int32, qt.shape, 0)
    lo = jnp.where(row < HEAD_DIM, qt, 0.0).astype(BF16)
    hi = jnp.where(row >= HEAD_DIM, qt, 0.0).astype(BF16)
    return lo, hi


DIFF_TQ = 256
DIFF_TK = 256
SUBLANES = 8


ATTN_SAMPLES = 2


def _locate(i, n_blk):
    if isinstance(i, int):
        return divmod(i, n_blk)
    return lax.shift_right_logical(i, n_blk.bit_length() - 1), i & (n_blk - 1)


def _interleave(*stages):
    order = sorted(((k + 0.5) / len(st), n, k) for n, st in enumerate(stages) for k in range(len(st)))
    for _, n, k in order:
        stages[n][k]()


def _diff_pipeline(q_ref, kc_ref, kl_ref, vc_ref, vl_ref, lamv_ref, gcol_ref, o_ref,
                   kk_ref, vte_ref, qm_ref, s_ref, m_ref):
    n_smp = q_ref.shape[0]
    n_ctx = kc_ref.shape[1]
    n_tick = kk_ref.shape[1] // DIFF_TK
    n_blk = q_ref.shape[1] // DIFF_TQ

    def stage_k(smp):
        kk_ref[smp, 0:n_ctx] = kc_ref[smp]
        kk_ref[smp, n_ctx:] = kl_ref[smp]

    def stage_v(smp, kc):
        lo = kc * DIFF_TK
        v = vc_ref[smp, lo:lo + DIFF_TK] if lo < n_ctx else vl_ref[smp, lo - n_ctx:lo - n_ctx + DIFF_TK]
        vte_ref[smp, kc] = v.astype(F32).T.astype(BF16)

    stage_k(0)

    lv = lamv_ref[...]
    lam = (jnp.exp(jnp.sum(lv[0:1] * lv[1:2], axis=-1, keepdims=True))
           - jnp.exp(jnp.sum(lv[2:3] * lv[3:4], axis=-1, keepdims=True)) + LAM_INIT)

    def rows(i):
        smp, li = _locate(i, n_blk)
        return smp, pl.ds(pl.multiple_of(li * DIFF_TQ, DIFF_TQ), DIFF_TQ)

    def keys(kc):
        return slice(kc * DIFF_TK, (kc + 1) * DIFF_TK)

    def prep(i, slot):
        smp, r = rows(i)
        lo, hi = _split_halves(q_ref[smp, r, :])
        qm_ref[slot, 0] = lo
        qm_ref[slot, 1] = hi
        m_ref[slot] = jnp.full(m_ref.shape[1:], -jnp.inf, F32)

    def score_tick(slot, i, kc):
        k = kk_ref[_locate(i, n_blk)[0], keys(kc), :]
        for c in range(2):
            s = jnp.dot(k, qm_ref[slot, c], preferred_element_type=F32)
            s_ref[slot, c, keys(kc), :] = s
            part = jnp.max(s.reshape(DIFF_TK // SUBLANES, SUBLANES, DIFF_TQ), axis=0)
            m_ref[slot, c] = jnp.maximum(m_ref[slot, c], part)

    def final_max(slot):
        return [jnp.broadcast_to(jnp.max(m_ref[slot, c], axis=0, keepdims=True), (SUBLANES, DIFF_TQ))
                for c in range(2)]

    def pv_tick(slot, i, kc, mx, acc):
        vt = vte_ref[_locate(i, n_blk)[0], kc]
        for c in range(2):
            s = s_ref[slot, c, keys(kc), :].reshape(DIFF_TK // SUBLANES, SUBLANES, DIFF_TQ)
            p = jnp.exp2(s - mx[c])
            d = [jnp.dot(vt, p.reshape(DIFF_TK, DIFF_TQ).astype(BF16), preferred_element_type=F32),
                 jnp.sum(p, axis=0)]
            acc[c] = d if acc[c] is None else [a + b for a, b in zip(acc[c], d)]

    def finish(i, acc):
        l0 = jnp.sum(acc[0][1], axis=0, keepdims=True)
        l1 = jnp.sum(acc[1][1], axis=0, keepdims=True)
        ot = acc[0][0] * (1.0 / l0) - acc[1][0] * (lam / l1)
        ms = jnp.mean(ot * ot, axis=0, keepdims=True)
        on = ot * lax.rsqrt(ms + NORM_EPS) * (gcol_ref[...] * (1.0 - LAM_INIT))
        smp, r = rows(i)
        o_ref[smp, r, :] = on.T.astype(BF16)

    def start():
        return ([functools.partial(prep, 0, 0)]
                + [functools.partial(score_tick, 0, 0, kc) for kc in range(n_tick)],
                [functools.partial(stage_v, smp, kc) for smp in range(n_smp) for kc in range(n_tick)]
                + [functools.partial(stage_k, smp) for smp in range(1, n_smp)])

    def block(i, slot, last=False):
        state = {}

        def head():
            state["mx"] = final_max(slot)
            state["acc"] = [None, None]
            if not last:
                prep(i + 1, 1 - slot)

        def tick(kc):
            pv_tick(slot, i, kc, state["mx"], state["acc"])
            if not last:
                score_tick(1 - slot, i + 1, kc)

        return ([head] + [functools.partial(tick, kc) for kc in range(n_tick)]
                + [lambda: finish(i, state["acc"])])

    return start, block


def _diff_scratch(c, s):
    return [pltpu.VMEM((ATTN_SAMPLES, c + s, LANES), BF16),
            pltpu.VMEM((ATTN_SAMPLES, (c + s) // DIFF_TK, LANES, DIFF_TK), BF16),
            pltpu.VMEM((2, 2, LANES, DIFF_TQ), BF16),
            pltpu.VMEM((2, 2, c + s, DIFF_TQ), F32),
            pltpu.VMEM((2, 2, SUBLANES, DIFF_TQ), F32)]


NA_QROWS = 4
NA_KROWS = 12
NA_TQ = NA_QROWS * GRID_W
NA_TK = 4 * GRID_W
NA_WIN_TICKS = NA_KROWS * GRID_W // NA_TK


def _na_variants(rows):
    n_rb = rows // NA_QROWS
    out = []
    for rb in (0, 1, n_rb - 1):
        ws = min(max(rb * NA_QROWS - NA_KH // 2, 0), rows - NA_KROWS)
        out.append((rb * NA_QROWS, ws))
    return out


def _na_build_bias(rpb_ref, bias_ref, rows):
    shape = (GRID_W, LANES)
    kc = lax.broadcasted_iota(jnp.int32, shape, 0)
    lane = lax.broadcasted_iota(jnp.int32, shape, 1)
    qc = lane & (GRID_W - 1)
    cs = jnp.clip(qc - NA_KW // 2, 0, GRID_W - NA_KW)
    col_ok = (kc >= cs) & (kc < cs + NA_KW)
    side_ok = {(True, True): col_ok, (True, False): col_ok & (lane < GRID_W),
               (False, True): col_ok & (lane >= GRID_W)}
    neg = jnp.full(shape, NEG_INF, F32)
    for hh in range(2):
        toeplitz = {}
        for v, (r0, ws) in enumerate(_na_variants(rows)):
            for i in range(NA_KROWS):
                kr = ws + i
                for jp in range(NA_QROWS // 2):
                    ok = []
                    for r in (r0 + 2 * jp, r0 + 2 * jp + 1):
                        rs = min(max(r - NA_KH // 2, 0), rows - NA_KH)
                        ok.append(rs <= kr < rs + NA_KH)
                    tile = neg
                    if any(ok):
                        e = kr - (r0 + 2 * jp) + NA_KH - 1
                        if e not in toeplitz:
                            vec = jnp.broadcast_to(rpb_ref[0, hh, e:e + 1, :] * LOG2E, shape)
                            toeplitz[e] = pltpu.roll(vec, LANES - (NA_KW - 1), 1, stride=1, stride_axis=0)
                        tile = jnp.where(side_ok[tuple(ok)], toeplitz[e], neg)
                    bias_ref[hh, v, i * GRID_W:(i + 1) * GRID_W, jp * LANES:(jp + 1) * LANES] = tile


def _na_pipeline(q_ref, kc_ref, k_ref, vc_ref, v_ref, rpb_ref, o_ref,
                 bias_ref, vte_ref, qm_ref, s_ref, m_ref, *, rows):
    n_blk = rows // NA_QROWS
    n_lat = v_ref.shape[1] // NA_TK
    assert kc_ref.shape[1] == NA_TK
    n_tick = NA_WIN_TICKS + 1

    @pl.when(pl.program_id(1) == 0)
    def _():
        _na_build_bias(rpb_ref, bias_ref, rows)

    n_smp = q_ref.shape[0]

    def stage_v(smp, ch):
        v = v_ref[smp, ch * NA_TK:(ch + 1) * NA_TK] if ch < n_lat else vc_ref[smp]
        vt = v.astype(F32).T
        for hh in range(2):
            vte_ref[smp, hh, ch] = vt[hh * HEAD_DIM:(hh + 1) * HEAD_DIM].astype(BF16)

    def rows_of(i):
        smp, li = _locate(i, n_blk)
        return smp, pl.ds(pl.multiple_of(li * NA_TQ, NA_TQ), NA_TQ)

    def window(i):
        smp, li = _locate(i, n_blk)
        return (smp, jnp.clip(li - 1, 0, n_lat - NA_WIN_TICKS),
                jnp.where(li == 0, 0, jnp.where(li == n_blk - 1, 2, 1)))

    def prep(i, slot):
        smp, r = rows_of(i)
        lo, hi = _split_halves(q_ref[smp, r, :])
        qm_ref[slot, 0] = lo
        qm_ref[slot, 1] = hi
        m_ref[slot] = jnp.full(m_ref.shape[1:], -jnp.inf, F32)

    def score_tick(slot, i, t):
        smp, ch0, variant = window(i)
        if t < NA_WIN_TICKS:
            k = k_ref[smp, pl.ds(pl.multiple_of((ch0 + t) * NA_TK, NA_TK), NA_TK), :]
        else:
            k = kc_ref[smp]
        for hh in range(2):
            s = jnp.dot(k, qm_ref[slot, hh], preferred_element_type=F32)
            if t < NA_WIN_TICKS:
                s = s + bias_ref[hh, variant, t * NA_TK:(t + 1) * NA_TK, :]
            s_ref[slot, hh, t * NA_TK:(t + 1) * NA_TK, :] = s
            part = jnp.max(s.reshape(NA_TK // SUBLANES, SUBLANES, NA_TQ), axis=0)
            m_ref[slot, hh] = jnp.maximum(m_ref[slot, hh], part)

    def final_max(slot):
        return [jnp.broadcast_to(jnp.max(m_ref[slot, hh], axis=0, keepdims=True), (SUBLANES, NA_TQ))
                for hh in range(2)]

    def pv_tick(slot, i, t, mx, acc):
        smp, ch0, _ = window(i)
        ch = ch0 + t if t < NA_WIN_TICKS else n_lat
        for hh in range(2):
            s = s_ref[slot, hh, t * NA_TK:(t + 1) * NA_TK, :].reshape(NA_TK // SUBLANES, SUBLANES, NA_TQ)
            p = jnp.exp2(s - mx[hh])
            d = [jnp.dot(vte_ref[smp, hh, ch], p.reshape(NA_TK, NA_TQ).astype(BF16), preferred_element_type=F32),
                 jnp.sum(p, axis=0)]
            acc[hh] = d if acc[hh] is None else [a + b for a, b in zip(acc[hh], d)]

    def finish(i, acc):
        ot = jnp.concatenate(
            [acc[hh][0] * (1.0 / jnp.sum(acc[hh][1], axis=0, keepdims=True)) for hh in range(2)],
            axis=0)
        smp, r = rows_of(i)
        o_ref[smp, r, :] = ot.T.astype(BF16)

    def start():
        return ([functools.partial(prep, 0, 0)]
                + [functools.partial(score_tick, 0, 0, t) for t in range(n_tick)],
                [functools.partial(stage_v, smp, ch) for smp in range(n_smp) for ch in range(n_lat + 1)])

    def block(i, slot, last=False):
        state = {}

        def head():
            state["mx"] = final_max(slot)
            state["acc"] = [None, None]
            if not last:
                prep(i + 1, 1 - slot)

        def tick(t):
            pv_tick(slot, i, t, state["mx"], state["acc"])
            if not last:
                score_tick(1 - slot, i + 1, t)

        return ([head] + [functools.partial(tick, t) for t in range(n_tick)]
                + [lambda: finish(i, state["acc"])])

    return start, block


def _na_scratch(n_lat):
    return [pltpu.VMEM((2, 3, NA_KROWS * GRID_W, NA_TQ), F32),
            pltpu.VMEM((ATTN_SAMPLES, 2, n_lat + 1, HEAD_DIM, NA_TK), BF16),
            pltpu.VMEM((2, 2, LANES, NA_TQ), BF16),
            pltpu.VMEM((2, 2, (NA_WIN_TICKS + 1) * NA_TK, NA_TQ), F32),
            pltpu.VMEM((2, 2, SUBLANES, NA_TQ), F32)]


N_DIFF_IN, N_NA_IN = 7, 6


def _attn_kernel(*refs, rows):
    d_in = refs[:N_DIFF_IN]
    n_in = refs[N_DIFF_IN:N_DIFF_IN + N_NA_IN]
    oa_ref, ob_ref = refs[N_DIFF_IN + N_NA_IN:N_DIFF_IN + N_NA_IN + 2]
    scratch = refs[N_DIFF_IN + N_NA_IN + 2:]
    d_start, d_block = _diff_pipeline(*d_in, oa_ref, *scratch[:5])
    n_start, n_block = _na_pipeline(*n_in, ob_ref, *scratch[5:], rows=rows)
    n_blk = oa_ref.shape[0] * (rows // NA_QROWS)
    _interleave(*d_start(), *n_start())

    def pair(u, carry):
        for j in range(2):
            _interleave(d_block(2 * u + j, j), n_block(2 * u + j, j))
        return carry

    lax.fori_loop(0, n_blk // 2 - 1, pair, 0)
    _interleave(d_block(n_blk - 2, 0), n_block(n_blk - 2, 0))
    _interleave(d_block(n_blk - 1, 1, last=True), n_block(n_blk - 1, 1, last=True))


def _attention(aq, ak, av, akc, avc, lamv, gout, nq, nk, nv, nkc, nvc, rpb):
    bsz, s, _ = aq.shape
    c = akc.shape[1]
    rows = s // GRID_W
    n_blk = rows // NA_QROWS
    assert NA_HEADS // 2 == DIFF_HEADS and NA_TQ == DIFF_TQ and NA_QROWS * GRID_W == NA_TK and c == NA_TK
    assert n_blk >= 4 and n_blk & (n_blk - 1) == 0 and c % DIFF_TK == 0 and s % DIFF_TK == 0
    assert bsz % ATTN_SAMPLES == 0
    n_r, n_c = rpb.shape[1:]
    padded = jnp.pad(rpb[:, :, ::-1], ((0, 0), (1, 1), (0, GRID_W - n_c)))
    rpb2 = jnp.concatenate([padded[:, 1:], padded[:, :-1]], axis=-1).reshape(DIFF_HEADS, 2, n_r + 1, LANES)
    lat = pl.BlockSpec((ATTN_SAMPLES, s, LANES), lambda h, b: (b, 0, h))
    ctx = pl.BlockSpec((ATTN_SAMPLES, c, LANES), lambda h, b: (b, 0, h))
    return pl.pallas_call(
        functools.partial(_attn_kernel, rows=rows),
        grid=(DIFF_HEADS, bsz // ATTN_SAMPLES),
        in_specs=[lat, ctx, lat, ctx, lat,
                  pl.BlockSpec(lamv.shape, lambda h, b: (0, 0)),
                  pl.BlockSpec((LANES, 1), lambda h, b: (0, 0)),
                  lat, ctx, lat, ctx, lat,
                  pl.BlockSpec((1,) + rpb2.shape[1:], lambda h, b: (h, 0, 0, 0))],
        out_specs=[lat, lat],
        out_shape=[jax.ShapeDtypeStruct(aq.shape, BF16), jax.ShapeDtypeStruct(nq.shape, BF16)],
        scratch_shapes=_diff_scratch(c, s) + _na_scratch(s // NA_TK),
        compiler_params=_cparams(("arbitrary", "arbitrary")),
        name="attention",
    )(aq, akc, ak, avc, av, lamv, gout.reshape(LANES, 1), nq, nkc, nk, nvc, nv, rpb2)


def _rope_tables(n_tokens):
    t = jnp.arange(n_tokens, dtype=jnp.int32)
    row = (t // GRID_W).astype(F32)
    col = (t % GRID_W).astype(F32)
    n_freq = HEAD_DIM // 4
    inv_freq = ROPE_BASE ** (-jnp.arange(n_freq, dtype=F32) / n_freq)
    ar = row[:, None] * inv_freq
    ac = col[:, None] * inv_freq
    cos = jnp.concatenate([jnp.cos(ar), jnp.cos(ar), jnp.cos(ac), jnp.cos(ac)], axis=-1)
    sin = jnp.concatenate([-jnp.sin(ar), jnp.sin(ar), -jnp.sin(ac), jnp.sin(ac)], axis=-1)
    reps = LANES // HEAD_DIM
    return jnp.tile(cos, (1, reps)), jnp.tile(sin, (1, reps))


def kernel(x, c, ctx, c_ctx, w_ada, b_ada, norm1, norm2, norm3, ffn1_w_gu, ffn1_w_down, w_in,
           diff_q_norm, diff_k_norm, lam_q1, lam_k1, lam_q2, lam_k2, diff_out_norm,
           na_q_norm, na_k_norm, na_rpb, w_out, ffn2_w_gu, ffn2_w_down):
    bsz, s, d = x.shape
    assert w_ada.shape[0] == 1 and d == D_MODEL and s % (NA_QROWS * GRID_W) == 0
    rows = s // GRID_W

    assert bsz == CTX_ROW < MOD_ROWS
    cond = jnp.zeros((MOD_ROWS, d), F32).at[:bsz].set(c).at[CTX_ROW].set(c_ctx)
    mod4 = _adaln(cond, w_ada[0], b_ada).reshape(MOD_ROWS, N_MOD, 1, d)

    ffn1_w = _ffn_weights(ffn1_w_gu[0], ffn1_w_down[0])
    h, w_in_b, w_out_b, w_gu2, w_dn2 = _ffn(x, mod4, 0, False, norm1, ffn1_w, tm=TOKEN_TILE,
                                            casts=(w_in[0], w_out[0], ffn2_w_gu[0], ffn2_w_down[0]))
    n_ctx = ctx.shape[1]
    hc = _ffn(ctx.reshape(1, bsz * n_ctx, d), mod4, 0, True, norm1, ffn1_w, tm=TOKEN_TILE)

    seg = np.arange(MXU_W) // HEAD_DIM
    bd = jnp.asarray(seg[:, None] == seg[None, :], BF16)
    cos, sin = _rope_tables(s)
    reps = GROUP_W // HEAD_DIM
    gains = [jnp.tile(g, (1, reps)) for g in (diff_q_norm, diff_k_norm, na_q_norm, na_k_norm)]
    qs = HEAD_DIM ** -0.5
    lat_groups = ((0, 0, True, qs * LOG2E), (1, 1, True, 1.0), (2, None, False, 1.0),
                  (3, 2, False, qs * LOG2E), (4, 3, False, 1.0), (5, None, False, 1.0))
    aq, ak, av, nq, nk, nv = _in_proj(h, mod4, False, norm2, w_in_b, bd, cos, sin, gains,
                                      lat_groups, tm=TOKEN_TILE)
    ctx_groups = ((1, 1, False, 1.0), (2, None, False, 1.0), (4, 3, False, 1.0), (5, None, False, 1.0))
    ctx_kv = _in_proj(hc, mod4, True, norm2, w_in_b, bd, cos, sin, gains, ctx_groups, tm=CTX_PROJ_TILE)
    akc, avc, nkc, nvc = (a.reshape(bsz, n_ctx, GROUP_W) for a in ctx_kv)

    lamv = jnp.concatenate([lam_q1, lam_k1, lam_q2, lam_k2], axis=0)
    oa, ob = _attention(aq, ak, av, akc, avc, lamv, diff_out_norm, nq, nk, nv, nkc, nvc, na_rpb[0])

    return _ffn(h, mod4, 6, False, norm3, (w_gu2, w_dn2), tm=TOKEN_TILE, attn=(5, oa, ob, w_out_b))
```

```python
import functools
import math

import numpy as np
import jax
import jax.numpy as jnp
from jax import lax
from jax.experimental import pallas as pl
from jax.experimental.pallas import tpu as pltpu

D_MODEL = 1024
GRID_W = 64
HEAD_DIM = 64
DIFF_HEADS = 4
NA_HEADS = 8
GROUP_W = 512
N_MOD = 9
NA_KH = 8
NA_KW = 16
ROPE_BASE = 10000.0
NORM_EPS = 1e-6
NEG_INF = -1e30
LAM_INIT = 0.8 - 0.6 * math.exp(-0.3 * 0)
LOG2E = math.log2(math.e)

LANES = 128
BF16_ROWS = 16
MOD_ROWS = 16
CTX_ROW = 8
TOKEN_TILE = 1024
CTX_PROJ_TILE = 512
MXU_W = 256
VMEM_LIMIT = 56 * 1024 * 1024

F32 = jnp.float32
BF16 = jnp.bfloat16


def _cparams(sem):
    return pltpu.CompilerParams(dimension_semantics=sem, vmem_limit_bytes=VMEM_LIMIT)


ADALN_TK = 128


def _adaln_kernel(cond_ref, w_ref, b_ref, o_ref):
    k = pl.program_id(0)
    c = cond_ref[...]
    a = (c * jax.nn.sigmoid(c)).astype(BF16)
    part = jnp.dot(a, w_ref[...].astype(BF16), preferred_element_type=F32)

    @pl.when(k == 0)
    def _():
        o_ref[...] = part + b_ref[...]

    @pl.when(k > 0)
    def _():
        o_ref[...] += part


def _adaln(cond, w_ada, b_ada):
    rows, d = cond.shape
    n = w_ada.shape[1]
    return pl.pallas_call(
        _adaln_kernel,
        grid=(d // ADALN_TK,),
        in_specs=[pl.BlockSpec((rows, ADALN_TK), lambda k: (0, k)),
                  pl.BlockSpec((ADALN_TK, n), lambda k: (k, 0)),
                  pl.BlockSpec((1, n), lambda k: (0, 0))],
        out_specs=pl.BlockSpec((rows, n), lambda k: (0, 0)),
        out_shape=jax.ShapeDtypeStruct((rows, n), F32),
        compiler_params=_cparams(("arbitrary",)),
        name="adaln",
    )(cond, w_ada, b_ada)


def _modulated_norm(x, g, shift, scale):
    ms = jnp.mean(x * x, axis=-1, keepdims=True)
    xn = x * lax.rsqrt(ms + NORM_EPS) * g
    return xn * (1.0 + scale) + shift


def _mod_spec(k, ctx):
    if ctx:
        return pl.BlockSpec((1, 1, 1, D_MODEL), lambda b, *_: (CTX_ROW, k, 0, 0))
    return pl.BlockSpec((1, 1, 1, D_MODEL), lambda b, *_: (b, k, 0, 0))


FFN_TF = 256


def _ffn_kernel(*refs, with_attn, n_cast):
    refs = list(refs)
    h_ref = refs.pop(0)
    if with_attn:
        x_ref = h_ref
        agate_ref, oa_ref, ob_ref, wo_ref = (refs.pop(0) for _ in range(4))
    shift_ref, scale_ref, gate_ref, g_ref, wgu_ref, wd_ref = (refs.pop(0) for _ in range(6))
    cast_in = [refs.pop(0) for _ in range(n_cast)]
    o_ref = refs.pop(0)
    cast_out = [refs.pop(0) for _ in range(n_cast)]
    xm_ref, acc_ref = refs[:2]
    for src, dst in zip(cast_in, cast_out):
        dst[...] = src[...].astype(BF16)
    if with_attn:
        h_ref = refs[2]
        half = oa_ref.shape[-1]
        attn = (jnp.dot(oa_ref[0], wo_ref[0:half, :], preferred_element_type=F32)
                + jnp.dot(ob_ref[0], wo_ref[half:, :], preferred_element_type=F32))
        h_ref[0] = x_ref[0] + agate_ref[0, 0] * attn
    f = wd_ref.shape[0]
    nj = f // FFN_TF
    xm_ref[...] = _modulated_norm(h_ref[0], g_ref[...], shift_ref[0, 0], scale_ref[0, 0]).astype(BF16)

    def act(j):
        xm = xm_ref[...]
        g = jnp.dot(xm, wgu_ref[:, j * FFN_TF:(j + 1) * FFN_TF], preferred_element_type=F32)
        u = jnp.dot(xm, wgu_ref[:, f + j * FFN_TF:f + (j + 1) * FFN_TF], preferred_element_type=F32)
        return (g * jax.nn.sigmoid(g) * u).astype(BF16)

    a = act(0)
    for j in range(nj):
        down = jnp.dot(a, wd_ref[j * FFN_TF:(j + 1) * FFN_TF, :], preferred_element_type=F32)
        if j + 1 < nj:
            a = act(j + 1)
        if j == 0:
            acc_ref[...] = down
        elif j + 1 < nj:
            acc_ref[...] += down
        else:
            o_ref[0] = h_ref[0] + (0.5 * gate_ref[0, 0]) * (acc_ref[...] + down)


def _ffn_weights(w_gu, w_down):
    return w_gu.astype(BF16), w_down.astype(BF16)


def _resident(shape):
    return pl.BlockSpec(shape, lambda *_: (0,) * len(shape), pipeline_mode=pl.Buffered(1))


def _ffn(h, mod4, k0, ctx, g, weights, *, tm, attn=None, casts=()):
    bsz, t, d = h.shape
    wgu, wd = weights
    assert wd.shape[0] % FFN_TF == 0
    n_i = t // tm
    tile = pl.BlockSpec((1, tm, d), lambda b, i: (b, i, 0))
    in_specs, args = [tile], [h]
    scratch = [pltpu.VMEM((tm, d), BF16), pltpu.VMEM((tm, d), F32)]
    if attn is not None:
        k_gate, oa, ob, w_out = attn
        half = oa.shape[-1]
        in_specs += [_mod_spec(k_gate, ctx), pl.BlockSpec((1, tm, half), lambda b, i: (b, i, 0)),
                     pl.BlockSpec((1, tm, half), lambda b, i: (b, i, 0)), _resident(w_out.shape)]
        args += [mod4, oa, ob, w_out]
        scratch.append(pltpu.VMEM((1, tm, d), F32))
    in_specs += [_mod_spec(k0, ctx), _mod_spec(k0 + 1, ctx), _mod_spec(k0 + 2, ctx),
                 pl.BlockSpec((1, d), lambda b, i: (0, 0)), _resident(wgu.shape), _resident(wd.shape)]
    args += [mod4, mod4, mod4, g, wgu, wd]
    slabs = []
    for w in casts:
        rows_per_step = w.shape[0] // (bsz * n_i)
        assert rows_per_step * bsz * n_i == w.shape[0] and rows_per_step % BF16_ROWS == 0
        slabs.append(pl.BlockSpec((rows_per_step, w.shape[1]), lambda b, i: (b * n_i + i, 0)))
    out = pl.pallas_call(
        functools.partial(_ffn_kernel, with_attn=attn is not None, n_cast=len(casts)),
        grid=(bsz, n_i),
        in_specs=in_specs + slabs,
        out_specs=[tile] + slabs,
        out_shape=[jax.ShapeDtypeStruct(h.shape, F32)] + [jax.ShapeDtypeStruct(w.shape, BF16) for w in casts],
        scratch_shapes=scratch,
        compiler_params=_cparams(("parallel", "parallel")),
        name="ffn_ctx" if ctx else "ffn",
    )(*args, *casts)
    return out if casts else out[0]


def _head_rms_scale(y, bd_ref):
    y2 = (y * y).astype(BF16)
    bd = bd_ref[...]
    w = bd.shape[0]
    ss = jnp.concatenate(
        [jnp.dot(y2[:, c:c + w], bd, preferred_element_type=F32) for c in range(0, y.shape[1], w)], axis=-1)
    return lax.rsqrt(ss * (1.0 / HEAD_DIM) + NORM_EPS)


def _rope(z, cos, sin):
    lane = lax.broadcasted_iota(jnp.int32, z.shape, 1)
    up = pltpu.roll(z, LANES - 16, 1)
    dn = pltpu.roll(z, 16, 1)
    return z * cos + jnp.where((lane & 16) == 0, up, dn) * sin


def _in_proj_kernel(*refs, groups, n_gain):
    h_ref, shift_ref, scale_ref, g_ref, w_ref, bd_ref, cos_ref, sin_ref = refs[:8]
    gain_refs = refs[8:8 + n_gain]
    out_refs = refs[8 + n_gain:]
    xm = _modulated_norm(h_ref[0], g_ref[...], shift_ref[0, 0], scale_ref[0, 0]).astype(BF16)
    for (col, gain_idx, rope, qscale), o_ref in zip(groups, out_refs):
        y = jnp.dot(xm, w_ref[:, col * GROUP_W:(col + 1) * GROUP_W], preferred_element_type=F32)
        if gain_idx is not None:
            gain = gain_refs[gain_idx][...]
            if qscale != 1.0:
                gain = gain * qscale
            y = y * _head_rms_scale(y, bd_ref) * gain
        if rope:
            cos = cos_ref[...]
            sin = sin_ref[...]
            for c in range(GROUP_W // LANES):
                sl = slice(c * LANES, (c + 1) * LANES)
                o_ref[0, :, sl] = _rope(y[:, sl], cos, sin).astype(BF16)
        else:
            o_ref[0] = y.astype(BF16)


def _in_proj(h, mod4, ctx, g, w_in, bd, cos, sin, gains, groups, *, tm):
    bsz, t, d = h.shape
    n_out = len(groups)
    in_specs = [pl.BlockSpec((1, tm, d), lambda b, i: (b, i, 0)),
                _mod_spec(3, ctx), _mod_spec(4, ctx),
                pl.BlockSpec((1, d), lambda b, i: (0, 0)),
                pl.BlockSpec(w_in.shape, lambda b, i: (0, 0)),
                pl.BlockSpec(bd.shape, lambda b, i: (0, 0)),
                pl.BlockSpec((tm, LANES), lambda b, i: (i, 0)),
                pl.BlockSpec((tm, LANES), lambda b, i: (i, 0))]
    in_specs += [pl.BlockSpec((1, GROUP_W), lambda b, i: (0, 0)) for _ in gains]
    return pl.pallas_call(
        functools.partial(_in_proj_kernel, groups=groups, n_gain=len(gains)),
        grid=(bsz, t // tm),
        in_specs=in_specs,
        out_specs=[pl.BlockSpec((1, tm, GROUP_W), lambda b, i: (b, i, 0)) for _ in range(n_out)],
        out_shape=[jax.ShapeDtypeStruct((bsz, t, GROUP_W), BF16) for _ in range(n_out)],
        compiler_params=_cparams(("parallel", "parallel")),
        name="in_proj_ctx" if ctx else "in_proj",
    )(h, mod4, mod4, g, w_in, bd, cos, sin, *gains)


def _split_halves(q):
    qt = q.astype(F32).T
    row = lax.broadcasted_iota(jnp.int32, qt.shape, 0)
    lo = jnp.where(row < HEAD_DIM, qt, 0.0).astype(BF16)
    hi = jnp.where(row >= HEAD_DIM, qt, 0.0).astype(BF16)
    return lo, hi


DIFF_TQ = 256
DIFF_TK = 256
SUBLANES = 8


ATTN_SAMPLES = 2


def _locate(i, n_blk):
    if isinstance(i, int):
        return divmod(i, n_blk)
    return lax.shift_right_logical(i, n_blk.bit_length() - 1), i & (n_blk - 1)


def _interleave(*stages):
    order = sorted(((k + 0.5) / len(st), n, k) for n, st in enumerate(stages) for k in range(len(st)))
    for _, n, k in order:
        stages[n][k]()


def _diff_pipeline(q_ref, kc_ref, kl_ref, vc_ref, vl_ref, lamv_ref, gcol_ref, o_ref,
                   kk_ref, vte_ref, qm_ref, s_ref, m_ref):
    n_smp = q_ref.shape[0]
    n_ctx = kc_ref.shape[1]
    n_tick = kk_ref.shape[1] // DIFF_TK
    n_blk = q_ref.shape[1] // DIFF_TQ

    def stage_k(smp):
        kk_ref[smp, 0:n_ctx] = kc_ref[smp]
        kk_ref[smp, n_ctx:] = kl_ref[smp]

    def stage_v(smp, kc):
        lo = kc * DIFF_TK
        v = vc_ref[smp, lo:lo + DIFF_TK] if lo < n_ctx else vl_ref[smp, lo - n_ctx:lo - n_ctx + DIFF_TK]
        vte_ref[smp, kc] = v.astype(F32).T.astype(BF16)

    stage_k(0)

    lv = lamv_ref[...]
    lam = (jnp.exp(jnp.sum(lv[0:1] * lv[1:2], axis=-1, keepdims=True))
           - jnp.exp(jnp.sum(lv[2:3] * lv[3:4], axis=-1, keepdims=True)) + LAM_INIT)

    def rows(i):
        smp, li = _locate(i, n_blk)
        return smp, pl.ds(pl.multiple_of(li * DIFF_TQ, DIFF_TQ), DIFF_TQ)

    def keys(kc):
        return slice(kc * DIFF_TK, (kc + 1) * DIFF_TK)

    def prep(i, slot):
        smp, r = rows(i)
        lo, hi = _split_halves(q_ref[smp, r, :])
        qm_ref[slot, 0] = lo
        qm_ref[slot, 1] = hi
        m_ref[slot] = jnp.full(m_ref.shape[1:], -jnp.inf, F32)

    def score_tick(slot, i, kc):
        k = kk_ref[_locate(i, n_blk)[0], keys(kc), :]
        for c in range(2):
            s = jnp.dot(k, qm_ref[slot, c], preferred_element_type=F32)
            s_ref[slot, c, keys(kc), :] = s
            part = jnp.max(s.reshape(DIFF_TK // SUBLANES, SUBLANES, DIFF_TQ), axis=0)
            m_ref[slot, c] = jnp.maximum(m_ref[slot, c], part)

    def final_max(slot):
        return [jnp.broadcast_to(jnp.max(m_ref[slot, c], axis=0, keepdims=True), (SUBLANES, DIFF_TQ))
                for c in range(2)]

    def pv_tick(slot, i, kc, mx, acc):
        vt = vte_ref[_locate(i, n_blk)[0], kc]
        for c in range(2):
            s = s_ref[slot, c, keys(kc), :].reshape(DIFF_TK // SUBLANES, SUBLANES, DIFF_TQ)
            p = jnp.exp2(s - mx[c])
            d = [jnp.dot(vt, p.reshape(DIFF_TK, DIFF_TQ).astype(BF16), preferred_element_type=F32),
                 jnp.sum(p, axis=0)]
            acc[c] = d if acc[c] is None else [a + b for a, b in zip(acc[c], d)]

    def finish(i, acc):
        l0 = jnp.sum(acc[0][1], axis=0, keepdims=True)
        l1 = jnp.sum(acc[1][1], axis=0, keepdims=True)
        ot = acc[0][0] * (1.0 / l0) - acc[1][0] * (lam / l1)
        ms = jnp.mean(ot * ot, axis=0, keepdims=True)
        on = ot * lax.rsqrt(ms + NORM_EPS) * (gcol_ref[...] * (1.0 - LAM_INIT))
        smp, r = rows(i)
        o_ref[smp, r, :] = on.T.astype(BF16)

    def start():
        return ([functools.partial(prep, 0, 0)]
                + [functools.partial(score_tick, 0, 0, kc) for kc in range(n_tick)],
                [functools.partial(stage_v, smp, kc) for smp in range(n_smp) for kc in range(n_tick)]
                + [functools.partial(stage_k, smp) for smp in range(1, n_smp)])

    def block(i, slot, last=False):
        state = {}

        def head():
            state["mx"] = final_max(slot)
            state["acc"] = [None, None]
            if not last:
                prep(i + 1, 1 - slot)

        def tick(kc):
            pv_tick(slot, i, kc, state["mx"], state["acc"])
            if not last:
                score_tick(1 - slot, i + 1, kc)

        return ([head] + [functools.partial(tick, kc) for kc in range(n_tick)]
                + [lambda: finish(i, state["acc"])])

    return start, block


def _diff_scratch(c, s):
    return [pltpu.VMEM((ATTN_SAMPLES, c + s, LANES), BF16),
            pltpu.VMEM((ATTN_SAMPLES, (c + s) // DIFF_TK, LANES, DIFF_TK), BF16),
            pltpu.VMEM((2, 2, LANES, DIFF_TQ), BF16),
            pltpu.VMEM((2, 2, c + s, DIFF_TQ), F32),
            pltpu.VMEM((2, 2, SUBLANES, DIFF_TQ), F32)]


NA_QROWS = 4
NA_KROWS = 12
NA_TQ = NA_QROWS * GRID_W
NA_TK = 4 * GRID_W
NA_WIN_TICKS = NA_KROWS * GRID_W // NA_TK


def _na_variants(rows):
    n_rb = rows // NA_QROWS
    out = []
    for rb in (0, 1, n_rb - 1):
        ws = min(max(rb * NA_QROWS - NA_KH // 2, 0), rows - NA_KROWS)
        out.append((rb * NA_QROWS, ws))
    return out


def _na_build_bias(rpb_ref, bias_ref, rows):
    shape = (GRID_W, LANES)
    kc = lax.broadcasted_iota(jnp.int32, shape, 0)
    lane = lax.broadcasted_iota(jnp.int32, shape, 1)
    qc = lane & (GRID_W - 1)
    cs = jnp.clip(qc - NA_KW // 2, 0, GRID_W - NA_KW)
    col_ok = (kc >= cs) & (kc < cs + NA_KW)
    side_ok = {(True, True): col_ok, (True, False): col_ok & (lane < GRID_W),
               (False, True): col_ok & (lane >= GRID_W)}
    neg = jnp.full(shape, NEG_INF, F32)
    for hh in range(2):
        toeplitz = {}
        for v, (r0, ws) in enumerate(_na_variants(rows)):
            for i in range(NA_KROWS):
                kr = ws + i
                for jp in range(NA_QROWS // 2):
                    ok = []
                    for r in (r0 + 2 * jp, r0 + 2 * jp + 1):
                        rs = min(max(r - NA_KH // 2, 0), rows - NA_KH)
                        ok.append(rs <= kr < rs + NA_KH)
                    tile = neg
                    if any(ok):
                        e = kr - (r0 + 2 * jp) + NA_KH - 1
                        if e not in toeplitz:
                            vec = jnp.broadcast_to(rpb_ref[0, hh, e:e + 1, :] * LOG2E, shape)
                            toeplitz[e] = pltpu.roll(vec, LANES - (NA_KW - 1), 1, stride=1, stride_axis=0)
                        tile = jnp.where(side_ok[tuple(ok)], toeplitz[e], neg)
                    bias_ref[hh, v, i * GRID_W:(i + 1) * GRID_W, jp * LANES:(jp + 1) * LANES] = tile


def _na_pipeline(q_ref, kc_ref, k_ref, vc_ref, v_ref, rpb_ref, o_ref,
                 bias_ref, vte_ref, qm_ref, s_ref, m_ref, *, rows):
    n_blk = rows // NA_QROWS
    n_lat = v_ref.shape[1] // NA_TK
    assert kc_ref.shape[1] == NA_TK
    n_tick = NA_WIN_TICKS + 1

    @pl.when(pl.program_id(1) == 0)
    def _():
        _na_build_bias(rpb_ref, bias_ref, rows)

    n_smp = q_ref.shape[0]

    def stage_v(smp, ch):
        v = v_ref[smp, ch * NA_TK:(ch + 1) * NA_TK] if ch < n_lat else vc_ref[smp]
        vt = v.astype(F32).T
        one_row = lax.broadcasted_iota(jnp.int32, (BF16_ROWS, NA_TK), 0) == 0
        for hh in range(2):
            vte_ref[smp, hh, ch, 0:HEAD_DIM, :] = vt[hh * HEAD_DIM:(hh + 1) * HEAD_DIM].astype(BF16)
            vte_ref[smp, hh, ch, HEAD_DIM:, :] = jnp.where(one_row, 1.0, 0.0).astype(BF16)

    def rows_of(i):
        smp, li = _locate(i, n_blk)
        return smp, pl.ds(pl.multiple_of(li * NA_TQ, NA_TQ), NA_TQ)

    def window(i):
        smp, li = _locate(i, n_blk)
        return (smp, jnp.clip(li - 1, 0, n_lat - NA_WIN_TICKS),
                jnp.where(li == 0, 0, jnp.where(li == n_blk - 1, 2, 1)))

    def prep(i, slot):
        smp, r = rows_of(i)
        lo, hi = _split_halves(q_ref[smp, r, :])
        qm_ref[slot, 0] = lo
        qm_ref[slot, 1] = hi
        m_ref[slot] = jnp.full(m_ref.shape[1:], -jnp.inf, F32)

    def score_tick(slot, i, t):
        smp, ch0, variant = window(i)
        if t < NA_WIN_TICKS:
            k = k_ref[smp, pl.ds(pl.multiple_of((ch0 + t) * NA_TK, NA_TK), NA_TK), :]
        else:
            k = kc_ref[smp]
        for hh in range(2):
            s = jnp.dot(k, qm_ref[slot, hh], preferred_element_type=F32)
            if t < NA_WIN_TICKS:
                s = s + bias_ref[hh, variant, t * NA_TK:(t + 1) * NA_TK, :]
            s_ref[slot, hh, t * NA_TK:(t + 1) * NA_TK, :] = s
            part = jnp.max(s.reshape(NA_TK // SUBLANES, SUBLANES, NA_TQ), axis=0)
            m_ref[slot, hh] = jnp.maximum(m_ref[slot, hh], part)

    def final_max(slot):
        return [jnp.broadcast_to(jnp.max(m_ref[slot, hh], axis=0, keepdims=True), (SUBLANES, NA_TQ))
                for hh in range(2)]

    def pv_tick(slot, i, t, mx, acc):
        smp, ch0, _ = window(i)
        ch = ch0 + t if t < NA_WIN_TICKS else n_lat
        for hh in range(2):
            s = s_ref[slot, hh, t * NA_TK:(t + 1) * NA_TK, :].reshape(NA_TK // SUBLANES, SUBLANES, NA_TQ)
            p = jnp.exp2(s - mx[hh])
            d = jnp.dot(vte_ref[smp, hh, ch], p.reshape(NA_TK, NA_TQ).astype(BF16), preferred_element_type=F32)
            acc[hh] = d if acc[hh] is None else acc[hh] + d

    def finish(i, acc):
        ot = jnp.concatenate(
            [acc[hh][0:HEAD_DIM] * (1.0 / acc[hh][HEAD_DIM:HEAD_DIM + 1]) for hh in range(2)],
            axis=0)
        smp, r = rows_of(i)
        o_ref[smp, r, :] = ot.T.astype(BF16)

    def start():
        return ([functools.partial(prep, 0, 0)]
                + [functools.partial(score_tick, 0, 0, t) for t in range(n_tick)],
                [functools.partial(stage_v, smp, ch) for smp in range(n_smp) for ch in range(n_lat + 1)])

    def block(i, slot, last=False):
        state = {}

        def head():
            state["mx"] = final_max(slot)
            state["acc"] = [None, None]
            if not last:
                prep(i + 1, 1 - slot)

        def tick(t):
            pv_tick(slot, i, t, state["mx"], state["acc"])
            if not last:
                score_tick(1 - slot, i + 1, t)

        return ([head] + [functools.partial(tick, t) for t in range(n_tick)]
                + [lambda: finish(i, state["acc"])])

    return start, block


def _na_scratch(n_lat):
    return [pltpu.VMEM((2, 3, NA_KROWS * GRID_W, NA_TQ), F32),
            pltpu.VMEM((ATTN_SAMPLES, 2, n_lat + 1, HEAD_DIM + BF16_ROWS, NA_TK), BF16),
            pltpu.VMEM((2, 2, LANES, NA_TQ), BF16),
            pltpu.VMEM((2, 2, (NA_WIN_TICKS + 1) * NA_TK, NA_TQ), F32),
            pltpu.VMEM((2, 2, SUBLANES, NA_TQ), F32)]


N_DIFF_IN, N_NA_IN = 7, 6


def _attn_kernel(*refs, rows):
    d_in = refs[:N_DIFF_IN]
    n_in = refs[N_DIFF_IN:N_DIFF_IN + N_NA_IN]
    oa_ref, ob_ref = refs[N_DIFF_IN + N_NA_IN:N_DIFF_IN + N_NA_IN + 2]
    scratch = refs[N_DIFF_IN + N_NA_IN + 2:]
    d_start, d_block = _diff_pipeline(*d_in, oa_ref, *scratch[:5])
    n_start, n_block = _na_pipeline(*n_in, ob_ref, *scratch[5:], rows=rows)
    n_blk = oa_ref.shape[0] * (rows // NA_QROWS)
    _interleave(*d_start(), *n_start())

    def pair(u, carry):
        for j in range(2):
            _interleave(d_block(2 * u + j, j), n_block(2 * u + j, j))
        return carry

    lax.fori_loop(0, n_blk // 2 - 1, pair, 0)
    _interleave(d_block(n_blk - 2, 0), n_block(n_blk - 2, 0))
    _interleave(d_block(n_blk - 1, 1, last=True), n_block(n_blk - 1, 1, last=True))


def _attention(aq, ak, av, akc, avc, lamv, gout, nq, nk, nv, nkc, nvc, rpb):
    bsz, s, _ = aq.shape
    c = akc.shape[1]
    rows = s // GRID_W
    n_blk = rows // NA_QROWS
    assert NA_HEADS // 2 == DIFF_HEADS and NA_TQ == DIFF_TQ and NA_QROWS * GRID_W == NA_TK and c == NA_TK
    assert n_blk >= 4 and n_blk & (n_blk - 1) == 0 and c % DIFF_TK == 0 and s % DIFF_TK == 0
    assert bsz % ATTN_SAMPLES == 0
    n_r, n_c = rpb.shape[1:]
    padded = jnp.pad(rpb[:, :, ::-1], ((0, 0), (1, 1), (0, GRID_W - n_c)))
    rpb2 = jnp.concatenate([padded[:, 1:], padded[:, :-1]], axis=-1).reshape(DIFF_HEADS, 2, n_r + 1, LANES)
    lat = pl.BlockSpec((ATTN_SAMPLES, s, LANES), lambda h, b: (b, 0, h))
    ctx = pl.BlockSpec((ATTN_SAMPLES, c, LANES), lambda h, b: (b, 0, h))
    return pl.pallas_call(
        functools.partial(_attn_kernel, rows=rows),
        grid=(DIFF_HEADS, bsz // ATTN_SAMPLES),
        in_specs=[lat, ctx, lat, ctx, lat,
                  pl.BlockSpec(lamv.shape, lambda h, b: (0, 0)),
                  pl.BlockSpec((LANES, 1), lambda h, b: (0, 0)),
                  lat, ctx, lat, ctx, lat,
                  pl.BlockSpec((1,) + rpb2.shape[1:], lambda h, b: (h, 0, 0, 0))],
        out_specs=[lat, lat],
        out_shape=[jax.ShapeDtypeStruct(aq.shape, BF16), jax.ShapeDtypeStruct(nq.shape, BF16)],
        scratch_shapes=_diff_scratch(c, s) + _na_scratch(s // NA_TK),
        compiler_params=_cparams(("arbitrary", "arbitrary")),
        name="attention",
    )(aq, akc, ak, avc, av, lamv, gout.reshape(LANES, 1), nq, nkc, nk, nvc, nv, rpb2)


def _rope_tables(n_tokens):
    t = jnp.arange(n_tokens, dtype=jnp.int32)
    row = (t // GRID_W).astype(F32)
    col = (t % GRID_W).astype(F32)
    n_freq = HEAD_DIM // 4
    inv_freq = ROPE_BASE ** (-jnp.arange(n_freq, dtype=F32) / n_freq)
    ar = row[:, None] * inv_freq
    ac = col[:, None] * inv_freq
    cos = jnp.concatenate([jnp.cos(ar), jnp.cos(ar), jnp.cos(ac), jnp.cos(ac)], axis=-1)
    sin = jnp.concatenate([-jnp.sin(ar), jnp.sin(ar), -jnp.sin(ac), jnp.sin(ac)], axis=-1)
    reps = LANES // HEAD_DIM
    return jnp.tile(cos, (1, reps)), jnp.tile(sin, (1, reps))


def kernel(x, c, ctx, c_ctx, w_ada, b_ada, norm1, norm2, norm3, ffn1_w_gu, ffn1_w_down, w_in,
           diff_q_norm, diff_k_norm, lam_q1, lam_k1, lam_q2, lam_k2, diff_out_norm,
           na_q_norm, na_k_norm, na_rpb, w_out, ffn2_w_gu, ffn2_w_down):
    bsz, s, d = x.shape
    assert w_ada.shape[0] == 1 and d == D_MODEL and s % (NA_QROWS * GRID_W) == 0
    rows = s // GRID_W

    assert bsz == CTX_ROW < MOD_ROWS
    cond = jnp.zeros((MOD_ROWS, d), F32).at[:bsz].set(c).at[CTX_ROW].set(c_ctx)
    mod4 = _adaln(cond, w_ada[0], b_ada).reshape(MOD_ROWS, N_MOD, 1, d)

    ffn1_w = _ffn_weights(ffn1_w_gu[0], ffn1_w_down[0])
    h, w_in_b, w_out_b, w_gu2, w_dn2 = _ffn(x, mod4, 0, False, norm1, ffn1_w, tm=TOKEN_TILE,
                                            casts=(w_in[0], w_out[0], ffn2_w_gu[0], ffn2_w_down[0]))
    n_ctx = ctx.shape[1]
    hc = _ffn(ctx.reshape(1, bsz * n_ctx, d), mod4, 0, True, norm1, ffn1_w, tm=TOKEN_TILE)

    seg = np.arange(MXU_W) // HEAD_DIM
    bd = jnp.asarray(seg[:, None] == seg[None, :], BF16)
    cos, sin = _rope_tables(s)
    reps = GROUP_W // HEAD_DIM
    gains = [jnp.tile(g, (1, reps)) for g in (diff_q_norm, diff_k_norm, na_q_norm, na_k_norm)]
    qs = HEAD_DIM ** -0.5
    lat_groups = ((0, 0, True, qs * LOG2E), (1, 1, True, 1.0), (2, None, False, 1.0),
                  (3, 2, False, qs * LOG2E), (4, 3, False, 1.0), (5, None, False, 1.0))
    aq, ak, av, nq, nk, nv = _in_proj(h, mod4, False, norm2, w_in_b, bd, cos, sin, gains,
                                      lat_groups, tm=TOKEN_TILE)
    ctx_groups = ((1, 1, False, 1.0), (2, None, False, 1.0), (4, 3, False, 1.0), (5, None, False, 1.0))
    ctx_kv = _in_proj(hc, mod4, True, norm2, w_in_b, bd, cos, sin, gains, ctx_groups, tm=CTX_PROJ_TILE)
    akc, avc, nkc, nvc = (a.reshape(bsz, n_ctx, GROUP_W) for a in ctx_kv)

    lamv = jnp.concatenate([lam_q1, lam_k1, lam_q2, lam_k2], axis=0)
    oa, ob = _attention(aq, ak, av, akc, avc, lamv, diff_out_norm, nq, nk, nv, nkc, nvc, na_rpb[0])

    return _ffn(h, mod4, 6, False, norm3, (w_gu2, w_dn2), tm=TOKEN_TILE, attn=(5, oa, ob, w_out_b))
```
